```python
import numpy as np
import jax
import jax.numpy as jnp
from jax import lax

D_MODEL = 2048
BATCH = 4
SEQ = 4096
DEPTH = 4

N_MIXERS = 3
PLE_DIM = 256
FFN_DIM = ((8 * D_MODEL + 3 * 256 - 1) // (3 * 256)) * 256
RMS_EPS = 1e-6
ROPE_THETA = 500000.0
NEG_INF = -1e30
FORCED_SCORE = 1e9

SSD_INNER = 2 * D_MODEL
SSD_HEAD_DIM = 64
SSD_HEADS = SSD_INNER // SSD_HEAD_DIM
SSD_GROUPS = 8
SSD_STATE = 128
SSD_CONV = 4
SSD_CHUNK = 128
SSD_CONV_DIM = SSD_INNER + 2 * SSD_GROUPS * SSD_STATE
SSD_PROJ = SSD_INNER + SSD_CONV_DIM + SSD_HEADS

LRU_WIDTH = D_MODEL
LRU_BLOCK_DIM = 256
LRU_BLOCKS = LRU_WIDTH // LRU_BLOCK_DIM
LRU_CONV = 4
LRU_C = 8.0

NSA_HEAD_DIM = 128
NSA_HEADS = D_MODEL // NSA_HEAD_DIM
NSA_KV_GROUPS = 4
NSA_HPG = NSA_HEADS // NSA_KV_GROUPS
NSA_QDIM = NSA_HEADS * NSA_HEAD_DIM
NSA_KVDIM = NSA_KV_GROUPS * NSA_HEAD_DIM
NSA_PROJ = NSA_QDIM + 6 * NSA_KVDIM + 3 * NSA_HEADS
ROT_DIM = NSA_HEAD_DIM // 4
CMP_LEN = 32
CMP_STRIDE = 16
SEL_LEN = 64
SEL_TOPK = 16
WINDOW = 512
Q_BLOCK = 128

N_SSD = (DEPTH + N_MIXERS - 1) // N_MIXERS
N_LRU = (DEPTH + N_MIXERS - 2) // N_MIXERS
N_NSA = DEPTH // N_MIXERS

kernel_name = "hybrid_ssd_rglru_nsa_trunk"


def rms_norm(x, w):
    x32 = x.astype(jnp.float32)
    y = x32 * lax.rsqrt(jnp.mean(x32 * x32, axis=-1, keepdims=True) + RMS_EPS)
    return (y * w.astype(jnp.float32)).astype(x.dtype)


def causal_dwconv(x, w, b):
    k = w.shape[0]
    y = lax.conv_general_dilated(x, w[:, None, :].astype(x.dtype), (1,), [(k - 1, 0)],
                                 dimension_numbers=("NWC", "WIO", "NWC"),
                                 feature_group_count=x.shape[-1])
    return y + b.astype(x.dtype)


def partial_rope(x, pos):
    half = ROT_DIM // 2
    inv = ROPE_THETA ** (-jnp.arange(half, dtype=jnp.float32) * 2.0 / ROT_DIM)
    ang = pos.astype(jnp.float32)[..., None] * inv
    cos = jnp.cos(ang)[:, :, None, :]
    sin = jnp.sin(ang)[:, :, None, :]
    xr = x[..., :ROT_DIM].astype(jnp.float32)
    x1, x2 = xr[..., :half], xr[..., half:]
    rot = jnp.concatenate([x1 * cos - x2 * sin, x2 * cos + x1 * sin], axis=-1).astype(x.dtype)
    return jnp.concatenate([rot, x[..., ROT_DIM:]], axis=-1)


def ssd_chunked_scan(xdt, adt, bm, cm):
    b, s = xdt.shape[:2]
    nc, q, r = s // SSD_CHUNK, SSD_CHUNK, SSD_HEADS // SSD_GROUPS
    x = xdt.reshape(b, nc, q, SSD_GROUPS, r, SSD_HEAD_DIM)
    a = adt.reshape(b, nc, q, SSD_GROUPS, r).transpose(0, 3, 4, 1, 2)
    bm = bm.reshape(b, nc, q, SSD_GROUPS, SSD_STATE)
    cm = cm.reshape(b, nc, q, SSD_GROUPS, SSD_STATE)
    a_cs = jnp.cumsum(a, axis=-1)
    causal = jnp.tril(jnp.ones((q, q), dtype=bool))
    seg = a_cs[..., :, None] - a_cs[..., None, :]
    decay_in = jnp.exp(jnp.where(causal, seg, -jnp.inf))
    cb = jnp.einsum("bclgn,bcsgn->bcgls", cm, bm)
    y_diag = jnp.einsum("bcgls,bgrcls,bcsgrp->bclgrp", cb, decay_in, x)
    decay_to_end = jnp.exp(a_cs[..., -1:] - a_cs)
    states = jnp.einsum("bclgn,bgrcl,bclgrp->bcgrpn", bm, decay_to_end, x)
    chunk_decay = jnp.exp(a_cs[..., -1])

    def step(carry, inp):
        st, dec = inp
        return carry * dec[..., None, None] + st, carry

    init = jnp.zeros_like(states[:, 0])
    _, prev = lax.scan(step, init, (jnp.moveaxis(states, 1, 0), jnp.moveaxis(chunk_decay, -1, 0)))
    prev = jnp.moveaxis(prev, 0, 1)
    y_off = jnp.einsum("bclgn,bcgrpn,bgrcl->bclgrp", cm, prev, jnp.exp(a_cs))
    return (y_diag + y_off).reshape(b, s, SSD_HEADS, SSD_HEAD_DIM)


def ssd_mixer(u, in_proj, conv_w, conv_b, dt_bias, a_log, d_skip, norm_w, out_proj):
    b, s, _ = u.shape
    f32 = jnp.float32
    z, xbc, dt = jnp.split(u @ in_proj, [SSD_INNER, SSD_INNER + SSD_CONV_DIM], axis=-1)
    xbc = jax.nn.silu(causal_dwconv(xbc, conv_w, conv_b))
    xs, bm, cm = jnp.split(xbc, [SSD_INNER, SSD_INNER + SSD_GROUPS * SSD_STATE], axis=-1)
    xs = xs.reshape(b, s, SSD_HEADS, SSD_HEAD_DIM).astype(f32)
    bm = bm.reshape(b, s, SSD_GROUPS, SSD_STATE).astype(f32)
    cm = cm.reshape(b, s, SSD_GROUPS, SSD_STATE).astype(f32)
    dt = jax.nn.softplus((dt + dt_bias).astype(f32))
    a = -jnp.exp(a_log.astype(f32))
    y = ssd_chunked_scan(xs * dt[..., None], dt * a, bm, cm) + d_skip.astype(f32)[:, None] * xs
    y = y.reshape(b, s, SSD_INNER) * jax.nn.silu(z.astype(f32))
    yg = y.reshape(b, s, SSD_GROUPS, SSD_INNER // SSD_GROUPS)
    yg = yg * lax.rsqrt(jnp.mean(yg * yg, axis=-1, keepdims=True) + RMS_EPS)
    y = yg.reshape(b, s, SSD_INNER) * norm_w.astype(f32)
    return y.astype(u.dtype) @ out_proj


def rglru_mixer(u, in_proj, conv_w, conv_b, wa, ba, wx, bx, a_param, out_proj):
    b, s, _ = u.shape
    f32 = jnp.float32
    gate, xr = jnp.split(u @ in_proj, 2, axis=-1)
    xr = causal_dwconv(xr, conv_w, conv_b)
    xb = xr.reshape(b, s, LRU_BLOCKS, LRU_BLOCK_DIM)
    r_t = jax.nn.sigmoid((jnp.einsum("bskc,kcd->bskd", xb, wa).reshape(b, s, LRU_WIDTH) + ba).astype(f32))
    i_t = jax.nn.sigmoid((jnp.einsum("bskc,kcd->bskd", xb, wx).reshape(b, s, LRU_WIDTH) + bx).astype(f32))
    log_a = -LRU_C * r_t * jax.nn.softplus(-a_param.astype(f32))
    a_t = jnp.exp(log_a)
    b_t = jnp.sqrt(-jnp.expm1(2.0 * log_a)) * (i_t * xr.astype(f32))

    def combine(left, right):
        a1, b1 = left
        a2, b2 = right
        return a1 * a2, a2 * b1 + b2

    _, h = lax.associative_scan(combine, (a_t, b_t), axis=1)
    y = jax.nn.gelu(gate.astype(f32)) * h
    return y.astype(u.dtype) @ out_proj


def nsa_mixer(u, positions, in_proj, cmp_pe, cmp_w1, cmp_w2, out_proj):
    b, s, _ = u.shape
    f32 = jnp.float32
    G, R, HD = NSA_KV_GROUPS, NSA_HPG, NSA_HEAD_DIM
    offs = [NSA_QDIM + k * NSA_KVDIM for k in range(7)]
    q, kc, vc, ks, vs, kw, vw, g = jnp.split(u @ in_proj, offs, axis=-1)
    kv_shape = (b, s, G, HD)
    q = partial_rope(q.reshape(b, s, NSA_HEADS, HD), positions)
    kc, vc, vs, vw = (t.reshape(kv_shape) for t in (kc, vc, vs, vw))
    ks = partial_rope(ks.reshape(kv_shape), positions)
    kw = partial_rope(kw.reshape(kv_shape), positions)
    gates = jax.nn.sigmoid(g.astype(f32)).reshape(b, s, G, R, 3)

    n_cmp = (s - CMP_LEN) // CMP_STRIDE + 1
    cmp_idx = np.arange(n_cmp)[:, None] * CMP_STRIDE + np.arange(CMP_LEN)[None, :]
    cmp_end = cmp_idx[:, -1]

    def compress(tok, pe, w1, w2):
        blk = tok[:, cmp_idx] + pe[:, None, :]
        flat = jnp.moveaxis(blk, 3, 2).reshape(b, n_cmp, G, CMP_LEN * HD)
        return jax.nn.gelu(flat @ w1) @ w2

    k_cmp = partial_rope(compress(kc, cmp_pe[0], cmp_w1[0], cmp_w2[0]), positions[:, cmp_end])
    v_cmp = compress(vc, cmp_pe[1], cmp_w1[1], cmp_w2[1]).astype(f32)
    cmp_end_j = jnp.asarray(cmp_end, dtype=jnp.int32)

    n_sel = s // SEL_LEN
    top_n = min(SEL_TOPK, n_sel)
    c_start = np.arange(n_cmp)[:, None] * CMP_STRIDE
    s_start = np.arange(n_sel)[None, :] * SEL_LEN
    overlap = np.clip(np.minimum(c_start + CMP_LEN, s_start + SEL_LEN) - np.maximum(c_start, s_start), 0, None)
    cmp_to_sel = jnp.asarray(overlap / CMP_STRIDE, dtype=f32)
    k_sel_blk = jnp.moveaxis(ks.reshape(b, n_sel, SEL_LEN, G, HD), 3, 1)
    v_sel_blk = jnp.moveaxis(vs.reshape(b, n_sel, SEL_LEN, G, HD), 3, 1)
    k_win = jnp.pad(kw, ((0, 0), (WINDOW, 0), (0, 0), (0, 0)))
    v_win = jnp.pad(vw, ((0, 0), (WINDOW, 0), (0, 0), (0, 0)))
    scale = HD ** -0.5
    bi = jnp.arange(b)[:, None, None, None]
    gi = jnp.arange(G)[None, :, None, None]
    jsel = jnp.arange(n_sel)

    def attend_block(blk):
        t0 = blk * Q_BLOCK
        t = t0 + jnp.arange(Q_BLOCK)
        qb = lax.dynamic_slice_in_dim(q, t0, Q_BLOCK, 1).reshape(b, Q_BLOCK, G, R, HD)
        gb = lax.dynamic_slice_in_dim(gates, t0, Q_BLOCK, 1)
        s_c = jnp.einsum("btgrd,bngd->bgrtn", qb, k_cmp).astype(f32) * scale
        ok_c = cmp_end_j[None, :] <= t[:, None]
        p_c = jax.nn.softmax(jnp.where(ok_c, s_c, NEG_INF), axis=-1) * ok_c
        o_c = jnp.einsum("bgrtn,bngd->btgrd", p_c, v_cmp)
        imp = jnp.einsum("bgrtn,nj->bgtj", p_c, cmp_to_sel)
        cur = t // SEL_LEN
        forced = (jsel[None, :] == 0) | (jsel[None, :] == cur[:, None]) | (jsel[None, :] == cur[:, None] - 1)
        imp = jnp.where(forced, FORCED_SCORE, imp)
        imp = jnp.where(jsel[None, :] <= cur[:, None], imp, -jnp.inf)
        _, sel = lax.top_k(imp, top_n)
        kg = k_sel_blk[bi, gi, sel]
        vg = v_sel_blk[bi, gi, sel].astype(f32)
        s_s = jnp.einsum("btgrd,bgtnld->bgrtnl", qb, kg).astype(f32) * scale
        kpos = sel[..., None] * SEL_LEN + jnp.arange(SEL_LEN)
        ok_s = (kpos <= t[:, None, None])[:, :, None]
        s_s = jnp.where(ok_s, s_s, NEG_INF).reshape(b, G, R, Q_BLOCK, top_n * SEL_LEN)
        p_s = jax.nn.softmax(s_s, axis=-1).reshape(b, G, R, Q_BLOCK, top_n, SEL_LEN)
        o_s = jnp.einsum("bgrtnl,bgtnld->btgrd", p_s, vg)
        kwb = lax.dynamic_slice_in_dim(k_win, t0, Q_BLOCK + WINDOW, 1)
        vwb = lax.dynamic_slice_in_dim(v_win, t0, Q_BLOCK + WINDOW, 1).astype(f32)
        kp = t0 - WINDOW + jnp.arange(Q_BLOCK + WINDOW)
        ok_w = (kp[None, :] <= t[:, None]) & (kp[None, :] > t[:, None] - WINDOW) & (kp[None, :] >= 0)
        s_w = jnp.einsum("btgrd,bkgd->bgrtk", qb, kwb).astype(f32) * scale
        p_w = jax.nn.softmax(jnp.where(ok_w, s_w, NEG_INF), axis=-1)
        o_w = jnp.einsum("bgrtk,bkgd->btgrd", p_w, vwb)
        o = gb[..., 0:1] * o_c + gb[..., 1:2] * o_s + gb[..., 2:3] * o_w
        return o.reshape(b, Q_BLOCK, NSA_QDIM).astype(u.dtype)

    out = lax.map(attend_block, jnp.arange(s // Q_BLOCK))
    out = jnp.moveaxis(out, 0, 1).reshape(b, s, NSA_QDIM)
    return out @ out_proj


def swiglu(v, w_in, w_out):
    gate, up = jnp.split(v @ w_in, 2, axis=-1)
    return (jax.nn.silu(gate) * up) @ w_out


def setup_inputs(seed: int = 0) -> dict:
    key = jax.random.key(seed)
    keys = iter(jax.random.split(key, 48))
    f32 = jnp.float32

    def nrm(shape, scale):
        return jax.random.normal(next(keys), shape, f32) * scale

    def unif(shape, lo, hi):
        return jax.random.uniform(next(keys), shape, f32, lo, hi)

    def gain(shape):
        return 1.0 + nrm(shape, 0.02)

    x = nrm((BATCH, SEQ, D_MODEL), 1.0)
    p = nrm((DEPTH, BATCH, SEQ, PLE_DIM), 1.0)
    offset = jax.random.randint(next(keys), (BATCH, 1), 0, 1024, jnp.int32)
    positions = offset + jnp.arange(SEQ, dtype=jnp.int32)[None, :]
    dt0 = jnp.exp(unif((N_SSD, SSD_HEADS), float(np.log(1e-3)), float(np.log(1e-1))))
    a0 = unif((N_LRU, LRU_WIDTH), 0.9, 0.999)
    s0 = a0 ** (1.0 / LRU_C)
    return {
        "x": x,
        "p": p,
        "positions": positions,
        "norm_mix": gain((DEPTH, D_MODEL)),
        "norm_ffn": gain((DEPTH, D_MODEL)),
        "norm_ple": gain((DEPTH, D_MODEL)),
        "w_ple_up": nrm((DEPTH, PLE_DIM, D_MODEL), PLE_DIM ** -0.5),
        "w_ple_gate": nrm((DEPTH, D_MODEL, D_MODEL), D_MODEL ** -0.5),
        "w_ffn_in": nrm((DEPTH, D_MODEL, 2 * FFN_DIM), D_MODEL ** -0.5),
        "w_ffn_out": nrm((DEPTH, FFN_DIM, D_MODEL), FFN_DIM ** -0.5),
        "norm_final": gain((D_MODEL,)),
        "ssd_in_proj": nrm((N_SSD, D_MODEL, SSD_PROJ), D_MODEL ** -0.5),
        "ssd_conv_w": nrm((N_SSD, SSD_CONV, SSD_CONV_DIM), SSD_CONV ** -0.5),
        "ssd_conv_b": nrm((N_SSD, SSD_CONV_DIM), 0.01),
        "ssd_dt_bias": dt0 + jnp.log(-jnp.expm1(-dt0)),
        "ssd_a_log": jnp.log(unif((N_SSD, SSD_HEADS), 1.0, 16.0)),
        "ssd_d": gain((N_SSD, SSD_HEADS)),
        "ssd_norm": gain((N_SSD, SSD_INNER)),
        "ssd_out_proj": nrm((N_SSD, SSD_INNER, D_MODEL), SSD_INNER ** -0.5),
        "lru_in_proj": nrm((N_LRU, D_MODEL, 2 * LRU_WIDTH), D_MODEL ** -0.5),
        "lru_conv_w": nrm((N_LRU, LRU_CONV, LRU_WIDTH), LRU_CONV ** -0.5),
        "lru_conv_b": nrm((N_LRU, LRU_WIDTH), 0.01),
        "lru_wa": nrm((N_LRU, LRU_BLOCKS, LRU_BLOCK_DIM, LRU_BLOCK_DIM), LRU_BLOCK_DIM ** -0.5),
        "lru_ba": nrm((N_LRU, LRU_WIDTH), 0.01),
        "lru_wx": nrm((N_LRU, LRU_BLOCKS, LRU_BLOCK_DIM, LRU_BLOCK_DIM), LRU_BLOCK_DIM ** -0.5),
        "lru_bx": nrm((N_LRU, LRU_WIDTH), 0.01),
        "lru_a_param": jnp.log(s0) - jnp.log1p(-s0),
        "lru_out_proj": nrm((N_LRU, LRU_WIDTH, D_MODEL), LRU_WIDTH ** -0.5),
        "nsa_in_proj": nrm((N_NSA, D_MODEL, NSA_PROJ), D_MODEL ** -0.5),
        "nsa_cmp_pe": nrm((N_NSA, 2, CMP_LEN, NSA_HEAD_DIM), 0.02),
        "nsa_cmp_w1": nrm((N_NSA, 2, CMP_LEN * NSA_HEAD_DIM, NSA_HEAD_DIM), (CMP_LEN * NSA_HEAD_DIM) ** -0.5),
        "nsa_cmp_w2": nrm((N_NSA, 2, NSA_HEAD_DIM, NSA_HEAD_DIM), NSA_HEAD_DIM ** -0.5),
        "nsa_out_proj": nrm((N_NSA, NSA_QDIM, D_MODEL), NSA_QDIM ** -0.5),
    }


def reference(x, p, positions, norm_mix, norm_ffn, norm_ple, w_ple_up, w_ple_gate, w_ffn_in, w_ffn_out,
              norm_final, ssd_in_proj, ssd_conv_w, ssd_conv_b, ssd_dt_bias, ssd_a_log, ssd_d, ssd_norm,
              ssd_out_proj, lru_in_proj, lru_conv_w, lru_conv_b, lru_wa, lru_ba, lru_wx, lru_bx, lru_a_param,
              lru_out_proj, nsa_in_proj, nsa_cmp_pe, nsa_cmp_w1, nsa_cmp_w2, nsa_out_proj):
    h = x
    for i in range(DEPTH):
        kind, j = i % N_MIXERS, i // N_MIXERS
        u = rms_norm(h, norm_mix[i])
        if kind == 0:
            mix = ssd_mixer(u, ssd_in_proj[j], ssd_conv_w[j], ssd_conv_b[j], ssd_dt_bias[j], ssd_a_log[j],
                            ssd_d[j], ssd_norm[j], ssd_out_proj[j])
        elif kind == 1:
            mix = rglru_mixer(u, lru_in_proj[j], lru_conv_w[j], lru_conv_b[j], lru_wa[j], lru_ba[j], lru_wx[j],
                              lru_bx[j], lru_a_param[j], lru_out_proj[j])
        else:
            mix = nsa_mixer(u, positions, nsa_in_proj[j], nsa_cmp_pe[j], nsa_cmp_w1[j], nsa_cmp_w2[j],
                            nsa_out_proj[j])
        h = h + mix
        h = h + swiglu(rms_norm(h, norm_ffn[i]), w_ffn_in[i], w_ffn_out[i])
        ple_gate = jax.nn.sigmoid(rms_norm(h, norm_ple[i]) @ w_ple_gate[i])
        h = h + ple_gate * (p[i].astype(h.dtype) @ w_ple_up[i])
    return rms_norm(h, norm_final)
```

```python
import functools

import numpy as np
import jax
import jax.numpy as jnp
from jax import lax
from jax.experimental import pallas as pl
from jax.experimental.pallas import tpu as pltpu

F32 = jnp.float32
BF16 = jnp.bfloat16

RMS_EPS = 1e-6
ROPE_THETA = 500000.0
NEG_INF = -1e30
FORCED_SCORE = 1e9

VMEM_LIMIT_BYTES = 52 * 1024 * 1024
LANES = 128

SSD_HEAD_DIM = 64
SSD_GROUPS = 8
SSD_STATE = 128
SSD_CONV = 4
SSD_CHUNK = 128

LRU_BLOCK_DIM = 256
LRU_CONV = 4
LRU_C = 8.0
LRU_ROWS = 256

NSA_HEAD_DIM = 128
NSA_KV_GROUPS = 4
ROT_DIM = NSA_HEAD_DIM // 4
CMP_LEN = 32
CMP_STRIDE = 16
SEL_LEN = 64
SEL_TOPK = 16
WINDOW = 512
Q_BLOCK = 128
SEL_KEY_TILE = 512

CONV_HIST = 8


def _cparams(*sem):
    return pltpu.CompilerParams(dimension_semantics=sem, vmem_limit_bytes=VMEM_LIMIT_BYTES)


def _dot(a, b):
    return jnp.dot(a, b, preferred_element_type=F32)


def _dot_nt(a, b):
    return lax.dot_general(a, b, (((1,), (1,)), ((), ())), preferred_element_type=F32)


def _split3(x):
    hi = x.astype(BF16)
    r1 = x - hi.astype(F32)
    mid = r1.astype(BF16)
    lo = (r1 - mid.astype(F32)).astype(BF16)
    return hi, mid, lo


def _dot_exact_rhs(x, e):
    hi, mid, lo = _split3(x)
    return _dot(hi, e) + _dot(mid, e) + _dot(lo, e)


def _dot_exact_lhs(e, x):
    hi, mid, lo = _split3(x)
    return _dot(e, hi) + _dot(e, mid) + _dot(e, lo)


def _rms_scale(x):
    return lax.rsqrt(jnp.mean(x * x, axis=-1, keepdims=True) + RMS_EPS)


def _softplus(x):
    return jnp.maximum(x, 0.0) + jnp.log(1.0 + jnp.exp(-jnp.abs(x)))


def _sigmoid(x):
    return 1.0 / (1.0 + jnp.exp(-x))


def _norm_mm_kernel(x_ref, nw_ref, w_ref, o_ref, u_ref):
    @pl.when(pl.program_id(1) == 0)
    def _():
        x = x_ref[...]
        u_ref[...] = (x * _rms_scale(x) * nw_ref[...]).astype(BF16)

    o_ref[...] = _dot(u_ref[...], w_ref[...]).astype(o_ref.dtype)


def norm_matmul(x, nw, w, out_dtype, tm=1024, tn=512):
    m, k = x.shape
    n = w.shape[1]
    return pl.pallas_call(
        _norm_mm_kernel,
        grid=(m // tm, n // tn),
        in_specs=[
            pl.BlockSpec((tm, k), lambda i, j: (i, 0)),
            pl.BlockSpec((1, k), lambda i, j: (0, 0)),
            pl.BlockSpec((k, tn), lambda i, j: (0, j)),
        ],
        out_specs=pl.BlockSpec((tm, tn), lambda i, j: (i, j)),
        out_shape=jax.ShapeDtypeStruct((m, n), out_dtype),
        scratch_shapes=[pltpu.VMEM((tm, k), BF16)],
        compiler_params=_cparams("parallel", "arbitrary"),
        name="norm_matmul",
    )(x, nw.reshape(1, k), w)


def _norm_swiglu_kernel(x_ref, nw_ref, wg_ref, wu_ref, o_ref, u_ref):
    @pl.when(pl.program_id(1) == 0)
    def _():
        x = x_ref[...]
        u_ref[...] = (x * _rms_scale(x) * nw_ref[...]).astype(BF16)

    u = u_ref[...]
    gate = _dot(u, wg_ref[...])
    up = _dot(u, wu_ref[...])
    o_ref[...] = (gate * _sigmoid(gate) * up).astype(o_ref.dtype)


def norm_swiglu_in(x, nw, w_in, tm=1024, tn=512):
    m, k = x.shape
    f = w_in.shape[1] // 2
    nj = f // tn
    return pl.pallas_call(
        _norm_swiglu_kernel,
        grid=(m // tm, nj),
        in_specs=[
            pl.BlockSpec((tm, k), lambda i, j: (i, 0)),
            pl.BlockSpec((1, k), lambda i, j: (0, 0)),
            pl.BlockSpec((k, tn), lambda i, j: (0, j)),
            pl.BlockSpec((k, tn), lambda i, j: (0, j + nj)),
        ],
        out_specs=pl.BlockSpec((tm, tn), lambda i, j: (i, j)),
        out_shape=jax.ShapeDtypeStruct((m, f), BF16),
        scratch_shapes=[pltpu.VMEM((tm, k), BF16)],
        compiler_params=_cparams("parallel", "arbitrary"),
        name="norm_swiglu_in",
    )(x, nw.reshape(1, k), w_in, w_in)


def _mm_residual_kernel(x_ref, w_ref, r_ref, o_ref):
    o_ref[...] = r_ref[...] + _dot(x_ref[...], w_ref[...])


def matmul_residual(x, w, res, tm=512, tn=512):
    m, k = x.shape
    n = w.shape[1]
    return pl.pallas_call(
        _mm_residual_kernel,
        grid=(m // tm, n // tn),
        in_specs=[
            pl.BlockSpec((tm, k), lambda i, j: (i, 0)),
            pl.BlockSpec((k, tn), lambda i, j: (0, j)),
            pl.BlockSpec((tm, tn), lambda i, j: (i, j)),
        ],
        out_specs=pl.BlockSpec((tm, tn), lambda i, j: (i, j)),
        out_shape=jax.ShapeDtypeStruct((m, n), F32),
        compiler_params=_cparams("parallel", "arbitrary"),
        name="matmul_residual",
    )(x, w, res)


def _ple_kernel(tn, x_ref, nw_ref, wg_ref, p_ref, wu_ref, o_ref, u_ref):
    j = pl.program_id(1)

    @pl.when(j == 0)
    def _():
        x = x_ref[...]
        u_ref[...] = (x * _rms_scale(x) * nw_ref[...]).astype(BF16)

    gate = _sigmoid(_dot(u_ref[...], wg_ref[...]))
    up = _dot(p_ref[...].astype(BF16), wu_ref[...])
    col = pl.multiple_of(j * tn, tn)
    o_ref[...] = x_ref[:, pl.ds(col, tn)] + gate * up


def ple_layer(x, nw, w_gate, p, w_up, tm=1024, tn=512):
    m, k = x.shape
    n = w_gate.shape[1]
    kp = p.shape[1]
    return pl.pallas_call(
        functools.partial(_ple_kernel, tn),
        grid=(m // tm, n // tn),
        in_specs=[
            pl.BlockSpec((tm, k), lambda i, j: (i, 0)),
            pl.BlockSpec((1, k), lambda i, j: (0, 0)),
            pl.BlockSpec((k, tn), lambda i, j: (0, j)),
            pl.BlockSpec((tm, kp), lambda i, j: (i, 0)),
            pl.BlockSpec((kp, tn), lambda i, j: (0, j)),
        ],
        out_specs=pl.BlockSpec((tm, tn), lambda i, j: (i, j)),
        out_shape=jax.ShapeDtypeStruct((m, n), F32),
        scratch_shapes=[pltpu.VMEM((tm, k), BF16)],
        compiler_params=_cparams("parallel", "arbitrary"),
        name="ple_layer",
    )(x, nw.reshape(1, k), w_gate, p, w_up)


def _final_norm_kernel(x_ref, nw_ref, o_ref):
    x = x_ref[...]
    o_ref[...] = x * _rms_scale(x) * nw_ref[...]


def final_norm(x, nw, tm=512):
    m, k = x.shape
    return pl.pallas_call(
        _final_norm_kernel,
        grid=(m // tm,),
        in_specs=[pl.BlockSpec((tm, k), lambda i: (i, 0)), pl.BlockSpec((1, k), lambda i: (0, 0))],
        out_specs=pl.BlockSpec((tm, k), lambda i: (i, 0)),
        out_shape=jax.ShapeDtypeStruct((m, k), F32),
        compiler_params=_cparams("parallel"),
        name="final_norm",
    )(x, nw.reshape(1, k))


def _causal_conv(pad_ref, w_ref, b_ref, rows, lo, width):
    taps = w_ref.shape[0]
    acc = b_ref[:, lo:lo + width]
    for k in range(taps):
        start = CONV_HIST - (taps - 1) + k
        acc = acc + w_ref[k:k + 1, lo:lo + width] * pad_ref[start:start + rows, lo:lo + width]
    return acc


def _ssd_kernel(z_ref, xs_ref, bc_ref, dt_ref, cwx_ref, cbx_ref, cwbc_ref, cbbc_ref, dtb_ref, alog_ref,
                dskip_ref, nw_ref, tril_ref, ehead_ref, o_ref,
                xpad_ref, bcpad_ref, bcact_ref, state_ref, acs_ref, acst_ref, dt_s_ref, xdt_ref, ydiag_ref):
    q = SSD_CHUNK
    gw = xs_ref.shape[1] // SSD_GROUPS
    heads_per_group = gw // SSD_HEAD_DIM
    n_bc = bc_ref.shape[1] // 2

    @pl.when(pl.program_id(1) == 0)
    def _():
        xpad_ref[0:CONV_HIST, :] = jnp.zeros((CONV_HIST, xpad_ref.shape[1]), F32)
        bcpad_ref[0:CONV_HIST, :] = jnp.zeros((CONV_HIST, bcpad_ref.shape[1]), F32)
        state_ref[...] = jnp.zeros(state_ref.shape, F32)

    xpad_ref[CONV_HIST:CONV_HIST + q, :] = xs_ref[...]
    bcpad_ref[CONV_HIST:CONV_HIST + q, :] = bc_ref[...]

    for lo in range(0, bc_ref.shape[1], 512):
        v = _causal_conv(bcpad_ref, cwbc_ref, cbbc_ref, q, lo, 512)
        bcact_ref[:, lo:lo + 512] = v * _sigmoid(v)

    dt = _softplus(dt_ref[...] + dtb_ref[...])
    adt = dt * (-jnp.exp(alog_ref[...]))
    acs = _dot_exact_lhs(tril_ref[...], adt)
    acs_ref[...] = acs
    acst_ref[...] = acs.T
    dt_s_ref[...] = dt

    li = lax.broadcasted_iota(jnp.int32, (q, q), 0)
    si = lax.broadcasted_iota(jnp.int32, (q, q), 1)
    causal = li >= si

    for g in range(SSD_GROUPS):
        glo = g * gw
        e_g = ehead_ref[:, glo:glo + gw]
        xc = _causal_conv(xpad_ref, cwx_ref, cbx_ref, q, glo, gw)
        xs = xc * _sigmoid(xc)
        dt_g = _dot_exact_rhs(dt_s_ref[...], e_g)
        a_g = _dot_exact_rhs(acs_ref[...], e_g)
        a_last = a_g[q - 1:q, :]
        xdt = xs * dt_g
        xdt_ref[...] = xdt.astype(BF16)

        bm = bcact_ref[:, g * SSD_STATE:(g + 1) * SSD_STATE]
        cm = bcact_ref[:, n_bc + g * SSD_STATE:n_bc + (g + 1) * SSD_STATE]
        bm16 = bm.astype(BF16)
        cm16 = cm.astype(BF16)
        cb = _dot_nt(cm16, bm16)

        prev = state_ref[:, glo:glo + gw]
        y_off = _dot(cm16, prev.astype(BF16)) * jnp.exp(a_g)
        st_new = _dot(bm.T.astype(BF16), (xdt * jnp.exp(a_last - a_g)).astype(BF16))
        state_ref[:, glo:glo + gw] = prev * jnp.exp(a_last) + st_new

        for r in range(heads_per_group):
            h = g * heads_per_group + r
            seg = acs_ref[:, h:h + 1] - acst_ref[h:h + 1, :]
            decay = jnp.exp(jnp.where(causal, seg, NEG_INF))
            m_h = (cb * decay).astype(BF16)
            ydiag_ref[:, r * SSD_HEAD_DIM:(r + 1) * SSD_HEAD_DIM] = _dot(
                m_h, xdt_ref[:, r * SSD_HEAD_DIM:(r + 1) * SSD_HEAD_DIM])

        y = ydiag_ref[...] + y_off + dskip_ref[:, glo:glo + gw] * xs
        zg = z_ref[:, glo:glo + gw]
        y = y * (zg * _sigmoid(zg))
        y = y * _rms_scale(y) * nw_ref[:, glo:glo + gw]
        o_ref[:, glo:glo + gw] = y.astype(o_ref.dtype)

    xpad_ref[0:CONV_HIST, :] = xpad_ref[q:q + CONV_HIST, :]
    bcpad_ref[0:CONV_HIST, :] = bcpad_ref[q:q + CONV_HIST, :]


def ssd_core(zxd, batch, seq, conv_w, conv_b, dt_bias, a_log, d_skip, norm_w):
    m = zxd.shape[0]
    n_heads = dt_bias.shape[0]
    inner = n_heads * SSD_HEAD_DIM
    n_bc = SSD_GROUPS * SSD_STATE
    q = SSD_CHUNK
    nc = seq // q
    assert inner % (2 * n_bc) == 0 and n_heads <= LANES
    pad_h = LANES - n_heads

    row = lambda v: v.reshape(1, -1).astype(F32)
    padh = lambda v: jnp.pad(v.astype(F32), (0, pad_h)).reshape(1, LANES)
    tril = jnp.asarray(np.tril(np.ones((q, q), np.float32)), BF16)
    ehead = jnp.asarray(np.repeat(np.eye(LANES, n_heads, dtype=np.float32), SSD_HEAD_DIM, axis=1), BF16)

    rows = lambda b, c: b * nc + c
    full = lambda a: pl.BlockSpec(a.shape, lambda b, c: (0,) * a.ndim)
    consts = [conv_w[:, :inner].astype(F32), row(conv_b[:inner]), conv_w[:, inner:].astype(F32), row(conv_b[inner:]),
              padh(dt_bias), padh(a_log), row(jnp.repeat(d_skip, SSD_HEAD_DIM)), row(norm_w), tril, ehead]
    return pl.pallas_call(
        _ssd_kernel,
        grid=(batch, nc),
        in_specs=[
            pl.BlockSpec((q, inner), lambda b, c: (rows(b, c), 0)),
            pl.BlockSpec((q, inner), lambda b, c: (rows(b, c), 1)),
            pl.BlockSpec((q, 2 * n_bc), lambda b, c: (rows(b, c), 2 * inner // (2 * n_bc))),
            pl.BlockSpec((q, LANES), lambda b, c: (rows(b, c), (2 * inner + 2 * n_bc) // LANES)),
        ] + [full(a) for a in consts],
        out_specs=pl.BlockSpec((q, inner), lambda b, c: (rows(b, c), 0)),
        out_shape=jax.ShapeDtypeStruct((m, inner), BF16),
        scratch_shapes=[
            pltpu.VMEM((q + CONV_HIST, inner), F32),
            pltpu.VMEM((q + CONV_HIST, 2 * n_bc), F32),
            pltpu.VMEM((q, 2 * n_bc), F32),
            pltpu.VMEM((SSD_STATE, inner), F32),
            pltpu.VMEM((q, LANES), F32),
            pltpu.VMEM((LANES, q), F32),
            pltpu.VMEM((q, LANES), F32),
            pltpu.VMEM((q, inner // SSD_GROUPS), BF16),
            pltpu.VMEM((q, inner // SSD_GROUPS), F32),
        ],
        compiler_params=_cparams("parallel", "arbitrary"),
        name="ssd_core",
    )(zxd, zxd, zxd, zxd, *consts)


def _lru_kernel(gate_ref, xr_ref, cw_ref, cb_ref, wa_ref, ba_ref, wx_ref, bx_ref, ap_ref, o_ref,
                xpad_ref, a_ref, b_ref, h_ref, carry_ref):
    rows = gate_ref.shape[0]
    width = gate_ref.shape[1]

    @pl.when(pl.program_id(1) == 0)
    def _():
        xpad_ref[0:CONV_HIST, :] = jnp.zeros((CONV_HIST, width), F32)
        carry_ref[...] = jnp.zeros(carry_ref.shape, F32)

    xpad_ref[CONV_HIST:CONV_HIST + rows, :] = xr_ref[...]
    for k in range(width // LRU_BLOCK_DIM):
        lo = k * LRU_BLOCK_DIM
        sl = slice(lo, lo + LRU_BLOCK_DIM)
        xc = _causal_conv(xpad_ref, cw_ref, cb_ref, rows, lo, LRU_BLOCK_DIM)
        x16 = xc.astype(BF16)
        r_t = _sigmoid(_dot(x16, wa_ref[k]) + ba_ref[:, sl])
        i_t = _sigmoid(_dot(x16, wx_ref[k]) + bx_ref[:, sl])
        log_a = -LRU_C * r_t * _softplus(-ap_ref[:, sl])
        a_t = jnp.exp(log_a)
        a_ref[:, sl] = a_t
        b_ref[:, sl] = jnp.sqrt(1.0 - a_t * a_t) * (i_t * xc)
    xpad_ref[0:CONV_HIST, :] = xpad_ref[rows:rows + CONV_HIST, :]

    def step(t, h):
        h = a_ref[pl.ds(t, 1), :] * h + b_ref[pl.ds(t, 1), :]
        h_ref[pl.ds(t, 1), :] = h
        return h

    carry_ref[0:1, :] = lax.fori_loop(0, rows, step, carry_ref[0:1, :], unroll=8)
    o_ref[...] = (jax.nn.gelu(gate_ref[...], approximate=True) * h_ref[...]).astype(o_ref.dtype)


def lru_core(gx, batch, seq, conv_w, conv_b, wa, ba, wx, bx, a_param):
    m = gx.shape[0]
    width = gx.shape[1] // 2
    rows = LRU_ROWS
    nt = seq // rows
    row = lambda v: v.reshape(1, -1).astype(F32)
    consts = [conv_w.astype(F32), row(conv_b), wa.astype(BF16), row(ba), wx.astype(BF16), row(bx), row(a_param)]
    full = lambda a: pl.BlockSpec(a.shape, lambda b, t: (0,) * a.ndim)
    return pl.pallas_call(
        _lru_kernel,
        grid=(batch, nt),
        in_specs=[
            pl.BlockSpec((rows, width), lambda b, t: (b * nt + t, 0)),
            pl.BlockSpec((rows, width), lambda b, t: (b * nt + t, 1)),
        ] + [full(a) for a in consts],
        out_specs=pl.BlockSpec((rows, width), lambda b, t: (b * nt + t, 0)),
        out_shape=jax.ShapeDtypeStruct((m, width), BF16),
        scratch_shapes=[
            pltpu.VMEM((rows + CONV_HIST, width), F32),
            pltpu.VMEM((rows, width), F32),
            pltpu.VMEM((rows, width), F32),
            pltpu.VMEM((rows, width), F32),
            pltpu.VMEM((8, width), F32),
        ],
        compiler_params=_cparams("parallel", "arbitrary"),
        name="lru_core",
    )(gx, gx, *consts)


def _rope_tables(pos):
    half = ROT_DIM // 2
    inv = ROPE_THETA ** (-jnp.arange(half, dtype=F32) * 2.0 / ROT_DIM)
    ang = pos.astype(F32)[..., None] * inv
    cos, sin = jnp.cos(ang), jnp.sin(ang)
    shape = pos.shape + (NSA_HEAD_DIM - ROT_DIM,)
    zeros_h = jnp.zeros(pos.shape + (half,), F32)
    cos_t = jnp.concatenate([cos, cos, jnp.ones(shape, F32)], axis=-1)
    sin_lo = jnp.concatenate([-sin, zeros_h, jnp.zeros(shape, F32)], axis=-1)
    sin_hi = jnp.concatenate([zeros_h, sin, jnp.zeros(shape, F32)], axis=-1)
    return cos_t, sin_lo, sin_hi


def _apply_rope(x, cos_t, sin_lo, sin_hi):
    n = x.shape[1] // NSA_HEAD_DIM
    half = ROT_DIM // 2
    tile = lambda t: jnp.concatenate([t] * n, axis=1) if n > 1 else t
    up = pltpu.roll(x, x.shape[1] - half, axis=1)
    down = pltpu.roll(x, half, axis=1)
    return x * tile(cos_t) + up * tile(sin_lo) + down * tile(sin_hi)


def _rope_kernel(x_ref, cos_ref, slo_ref, shi_ref, o_ref):
    o_ref[...] = _apply_rope(x_ref[...], cos_ref[...], slo_ref[...], shi_ref[...]).astype(o_ref.dtype)


def nsa_rope(proj, tables, qdim, kvdim, tm=512):
    m = proj.shape[0]
    nq = qdim // kvdim
    src = lambda j: j + 2 * (j >= nq).astype(jnp.int32) + (j >= nq + 1).astype(jnp.int32)
    tspec = pl.BlockSpec((tm, NSA_HEAD_DIM), lambda i, j: (i, 0))
    return pl.pallas_call(
        _rope_kernel,
        grid=(m // tm, nq + 2),
        in_specs=[pl.BlockSpec((tm, kvdim), lambda i, j: (i, src(j))), tspec, tspec, tspec],
        out_specs=pl.BlockSpec((tm, kvdim), lambda i, j: (i, j)),
        out_shape=jax.ShapeDtypeStruct((m, qdim + 2 * kvdim), BF16),
        compiler_params=_cparams("parallel", "arbitrary"),
        name="nsa_rope",
    )(proj, *tables)


def _compress_kernel(is_key, r_ref, pe_ref, w1_ref, w2_ref, cos_ref, slo_ref, shi_ref, o_ref):
    half_k = w1_ref.shape[0] // 2
    nrows = r_ref.shape[0]
    r = r_ref[...]
    top = _dot((r + pe_ref[:, 0:half_k]).astype(BF16), w1_ref[0:half_k, :])
    bot = _dot((r + pe_ref[:, half_k:]).astype(BF16), w1_ref[half_k:, :])
    pre = top + pltpu.roll(bot, nrows - 1, axis=0)
    out = _dot(jax.nn.gelu(pre, approximate=True).astype(BF16), w2_ref[...])
    if is_key:
        out = _apply_rope(out, cos_ref[...], slo_ref[...], shi_ref[...])
    o_ref[...] = out.astype(o_ref.dtype)


def nsa_compress(tok, pe, w1, w2, tables, is_key):
    b, g, nrows, kdim = tok.shape
    d = w2.shape[0]
    tspec = pl.BlockSpec((None, nrows, d), lambda i, j: (i, 0, 0))
    return pl.pallas_call(
        functools.partial(_compress_kernel, is_key),
        grid=(b, g),
        in_specs=[
            pl.BlockSpec((None, None, nrows, kdim), lambda i, j: (i, j, 0, 0)),
            pl.BlockSpec((1, 2 * kdim), lambda i, j: (0, 0)),
            pl.BlockSpec((2 * kdim, d), lambda i, j: (0, 0)),
            pl.BlockSpec((d, d), lambda i, j: (0, 0)),
            tspec, tspec, tspec,
        ],
        out_specs=pl.BlockSpec((None, None, nrows, d), lambda i, j: (i, j, 0, 0)),
        out_shape=jax.ShapeDtypeStruct((b, g, nrows, d), BF16),
        compiler_params=_cparams("parallel", "arbitrary"),
        name="nsa_compress_k" if is_key else "nsa_compress_v",
    )(tok, pe.reshape(1, 2 * kdim).astype(F32), w1.astype(BF16), w2.astype(BF16), *tables)


def _softmax_rows(s, ok):
    s = jnp.where(ok, s, NEG_INF)
    mx = jnp.max(s, axis=-1, keepdims=True)
    e = jnp.where(ok, jnp.exp(s - mx), 0.0)
    den = jnp.sum(e, axis=-1, keepdims=True)
    return e * jnp.where(den > 0.0, 1.0 / den, 0.0)


def _nsa_attn_kernel(q_ref, kcmp_ref, vcmp_ref, ks_ref, vs_ref, kw_ref, vw_ref, gate_ref, c2s_ref, o_ref,
                     imp_ref, sel_ref):
    tq = Q_BLOCK
    d = NSA_HEAD_DIM
    hpg = q_ref.shape[1] // d
    n_cmp_pad = kcmp_ref.shape[0]
    n_sel_pad = c2s_ref.shape[1]
    scale = d ** -0.5
    qb = pl.program_id(2)
    t0 = qb * tq

    q4 = jnp.concatenate([q_ref[:, r * d:(r + 1) * d] for r in range(hpg)], axis=0)
    t_one = t0 + lax.broadcasted_iota(jnp.int32, (tq, 1), 0)
    t_all = jnp.concatenate([t_one] * hpg, axis=0)

    n_idx = lax.broadcasted_iota(jnp.int32, (1, n_cmp_pad), 1)
    ok_c = (n_idx * CMP_STRIDE + (CMP_LEN - 1) <= t_all) & (n_idx < n_cmp_pad - 1)
    p_c = _softmax_rows(_dot_nt(q4, kcmp_ref[...]) * scale, ok_c)
    o_c = _dot(p_c.astype(BF16), vcmp_ref[...])

    p_sum = p_c[0:tq]
    for r in range(1, hpg):
        p_sum = p_sum + p_c[r * tq:(r + 1) * tq]
    imp = _dot_exact_rhs(p_sum, c2s_ref[...])
    j_idx = lax.broadcasted_iota(jnp.int32, (1, n_sel_pad), 1)
    cur = t_one // SEL_LEN
    forced = (j_idx == 0) | (j_idx == cur) | (j_idx == cur - 1)
    imp = jnp.where(forced, FORCED_SCORE, imp)
    imp = jnp.where(j_idx <= cur, imp, -jnp.inf)
    imp_ref[...] = imp.T
    imp_t = imp_ref[...]
    j_col = lax.broadcasted_iota(jnp.int32, (n_sel_pad, 1), 0)
    rank = jnp.zeros((n_sel_pad, tq), F32)
    for i in range(n_sel_pad):
        other = imp_ref[i:i + 1, :]
        ahead = (other > imp_t) | ((other == imp_t) & (i < j_col))
        rank = rank + jnp.where(ahead, 1.0, 0.0)
    sel_ref[...] = jnp.where(rank < float(SEL_TOPK), 1.0, 0.0).T.astype(BF16)

    kt_n = SEL_KEY_TILE
    blocks_per_tile = kt_n // SEL_LEN
    jj_row = lax.broadcasted_iota(jnp.int32, (n_sel_pad, kt_n), 0)
    kk_blk = lax.broadcasted_iota(jnp.int32, (n_sel_pad, kt_n), 1) // SEL_LEN
    k_lane = lax.broadcasted_iota(jnp.int32, (1, kt_n), 1)

    def sel_step(kt, carry):
        m_run, l_run, acc = carry
        k0 = pl.multiple_of(kt * kt_n, kt_n)
        s = _dot_nt(q4, ks_ref[pl.ds(k0, kt_n), :]) * scale
        expand = jnp.where(jj_row == kk_blk + kt * blocks_per_tile, 1.0, 0.0).astype(BF16)
        picked = _dot(sel_ref[...], expand)
        ok = (picked > 0.5) & (k0 + k_lane <= t_one)
        ok = jnp.concatenate([ok] * hpg, axis=0)
        s = jnp.where(ok, s, NEG_INF)
        m_new = jnp.maximum(m_run, jnp.max(s, axis=-1, keepdims=True))
        alpha = jnp.exp(m_run - m_new)
        p = jnp.where(ok, jnp.exp(s - m_new), 0.0)
        l_new = alpha * l_run + jnp.sum(p, axis=-1, keepdims=True)
        acc = alpha * acc + _dot(p.astype(BF16), vs_ref[pl.ds(k0, kt_n), :])
        return m_new, l_new, acc

    n_tiles = (t0 + tq + kt_n - 1) // kt_n
    init = (jnp.full((hpg * tq, 1), NEG_INF, F32), jnp.zeros((hpg * tq, 1), F32), jnp.zeros((hpg * tq, d), F32))
    _, l_s, acc_s = lax.fori_loop(0, n_tiles, sel_step, init)
    o_s = acc_s * (1.0 / l_s)

    n_win = WINDOW + tq
    w0 = pl.multiple_of(jnp.maximum(t0 - WINDOW, 0), tq)
    kp = w0 + lax.broadcasted_iota(jnp.int32, (1, n_win), 1)
    ok_w = (kp <= t_all) & (kp > t_all - WINDOW)
    p_w = _softmax_rows(_dot_nt(q4, kw_ref[pl.ds(w0, n_win), :]) * scale, ok_w)
    o_w = _dot(p_w.astype(BF16), vw_ref[pl.ds(w0, n_win), :])

    gates = _sigmoid(gate_ref[...])
    for r in range(hpg):
        rs = slice(r * tq, (r + 1) * tq)
        o_r = (gates[:, 3 * r:3 * r + 1] * o_c[rs] + gates[:, 3 * r + 1:3 * r + 2] * o_s[rs]
               + gates[:, 3 * r + 2:3 * r + 3] * o_w[rs])
        o_ref[:, r * d:(r + 1) * d] = o_r.astype(o_ref.dtype)


def nsa_attention(qk, v_sw, k_cmp, v_cmp, proj, c2s, batch, seq, qdim, kvdim, gate_col):
    m = qk.shape[0]
    d = NSA_HEAD_DIM
    groups = kvdim // d
    gq = qdim // groups
    nq = seq // Q_BLOCK
    n_cmp_pad = k_cmp.shape[2]
    kv_blk = lambda off: pl.BlockSpec((seq, d), lambda b, g, t: (b, off + g))
    cmp_blk = pl.BlockSpec((None, None, n_cmp_pad, d), lambda b, g, t: (b, g, 0, 0))
    return pl.pallas_call(
        _nsa_attn_kernel,
        grid=(batch, groups, nq),
        in_specs=[
            pl.BlockSpec((Q_BLOCK, gq), lambda b, g, t: (b * nq + t, g)),
            cmp_blk, cmp_blk,
            kv_blk(qdim // d), kv_blk(0), kv_blk(qdim // d + groups), kv_blk(groups),
            pl.BlockSpec((Q_BLOCK, LANES), lambda b, g, t: (b * nq + t, gate_col // LANES + g)),
            pl.BlockSpec(c2s.shape, lambda b, g, t: (0, 0)),
        ],
        out_specs=pl.BlockSpec((Q_BLOCK, gq), lambda b, g, t: (b * nq + t, g)),
        out_shape=jax.ShapeDtypeStruct((m, qdim), BF16),
        scratch_shapes=[pltpu.VMEM((c2s.shape[1], Q_BLOCK), F32), pltpu.VMEM((Q_BLOCK, c2s.shape[1]), BF16)],
        compiler_params=_cparams("parallel", "parallel", "arbitrary"),
        name="nsa_attention",
    )(qk, k_cmp, v_cmp, qk, v_sw, qk, v_sw, proj, c2s)


def _cmp_to_sel(seq):
    n_cmp = (seq - CMP_LEN) // CMP_STRIDE + 1
    n_sel = seq // SEL_LEN
    c_start = np.arange(n_cmp)[:, None] * CMP_STRIDE
    s_start = np.arange(n_sel)[None, :] * SEL_LEN
    overlap = np.clip(np.minimum(c_start + CMP_LEN, s_start + SEL_LEN) - np.maximum(c_start, s_start), 0, None)
    out = np.zeros((seq // CMP_STRIDE, max(n_sel, LANES)), np.float32)
    out[:n_cmp, :n_sel] = overlap / CMP_STRIDE
    return jnp.asarray(out, BF16)


def _pad_cols(w, n):
    return jnp.pad(w, ((0, 0), (0, n - w.shape[1])))


def ssd_mixer(h, nw, batch, seq, in_proj, conv_w, conv_b, dt_bias, a_log, d_skip, norm_w, out_proj):
    n_heads = dt_bias.shape[0]
    inner = n_heads * SSD_HEAD_DIM
    conv_dim = inner + 2 * SSD_GROUPS * SSD_STATE
    used = inner + conv_dim + n_heads
    w = _pad_cols(in_proj, -(-used // 512) * 512).astype(BF16)
    zxd = norm_matmul(h, nw, w, F32)
    y = ssd_core(zxd, batch, seq, conv_w, conv_b, dt_bias, a_log, d_skip, norm_w)
    return matmul_residual(y, out_proj.astype(BF16), h)


def rglru_mixer(h, nw, batch, seq, in_proj, conv_w, conv_b, wa, ba, wx, bx, a_param, out_proj):
    gx = norm_matmul(h, nw, in_proj.astype(BF16), F32)
    y = lru_core(gx, batch, seq, conv_w, conv_b, wa, ba, wx, bx, a_param)
    return matmul_residual(y, out_proj.astype(BF16), h)


def nsa_mixer(h, nw, positions, batch, seq, in_proj, cmp_pe, cmp_w1, cmp_w2, out_proj):
    d = NSA_HEAD_DIM
    groups = NSA_KV_GROUPS
    kvdim = groups * d
    qdim = out_proj.shape[0]
    n_heads = qdim // d
    hpg = n_heads // groups
    gate_col = qdim + 6 * kvdim

    wg = in_proj[:, gate_col:].reshape(-1, groups, 3 * hpg)
    wg = jnp.pad(wg, ((0, 0), (0, 0), (0, LANES - 3 * hpg))).reshape(-1, groups * LANES)
    w = jnp.concatenate([in_proj[:, :gate_col], wg], axis=1).astype(BF16)
    proj = norm_matmul(h, nw, w, F32)

    tables = tuple(t.reshape(batch * seq, d) for t in _rope_tables(positions))
    qk = nsa_rope(proj, tables, qdim, kvdim)
    v_sw = jnp.concatenate([proj[:, qdim + 3 * kvdim:qdim + 4 * kvdim],
                            proj[:, qdim + 5 * kvdim:qdim + 6 * kvdim]], axis=1).astype(BF16)

    n_rows = seq // CMP_STRIDE
    cmp_end = jnp.minimum(jnp.arange(n_rows) * CMP_STRIDE + CMP_LEN - 1, seq - 1)
    cmp_tables = _rope_tables(positions[:, cmp_end])

    def blocks(col):
        tok = proj[:, col:col + kvdim].reshape(batch, n_rows, CMP_STRIDE, groups, d)
        return tok.transpose(0, 3, 1, 2, 4).reshape(batch, groups, n_rows, CMP_STRIDE * d)

    k_cmp = nsa_compress(blocks(qdim), cmp_pe[0], cmp_w1[0], cmp_w2[0], cmp_tables, True)
    v_cmp = nsa_compress(blocks(qdim + kvdim), cmp_pe[1], cmp_w1[1], cmp_w2[1], cmp_tables, False)

    o = nsa_attention(qk, v_sw, k_cmp, v_cmp, proj, _cmp_to_sel(seq), batch, seq, qdim, kvdim, gate_col)
    return matmul_residual(o, out_proj.astype(BF16), h)


def kernel(x, p, positions, norm_mix, norm_ffn, norm_ple, w_ple_up, w_ple_gate, w_ffn_in, w_ffn_out, norm_final, ssd_in_proj, ssd_conv_w, ssd_conv_b, ssd_dt_bias, ssd_a_log, ssd_d, ssd_norm, ssd_out_proj, lru_in_proj, lru_conv_w, lru_conv_b, lru_wa, lru_ba, lru_wx, lru_bx, lru_a_param, lru_out_proj, nsa_in_proj, nsa_cmp_pe, nsa_cmp_w1, nsa_cmp_w2, nsa_out_proj):
    batch, seq, d_model = x.shape
    depth = norm_mix.shape[0]
    n_mixers = 3
    m = batch * seq
    h = x.reshape(m, d_model)
    for i in range(depth):
        kind, j = i % n_mixers, i // n_mixers
        if kind == 0:
            h = ssd_mixer(h, norm_mix[i], batch, seq, ssd_in_proj[j], ssd_conv_w[j], ssd_conv_b[j], ssd_dt_bias[j],
                          ssd_a_log[j], ssd_d[j], ssd_norm[j], ssd_out_proj[j])
        elif kind == 1:
            h = rglru_mixer(h, norm_mix[i], batch, seq, lru_in_proj[j], lru_conv_w[j], lru_conv_b[j], lru_wa[j],
                            lru_ba[j], lru_wx[j], lru_bx[j], lru_a_param[j], lru_out_proj[j])
        else:
            h = nsa_mixer(h, norm_mix[i], positions, batch, seq, nsa_in_proj[j], nsa_cmp_pe[j], nsa_cmp_w1[j],
                          nsa_cmp_w2[j], nsa_out_proj[j])
        hidden = norm_swiglu_in(h, norm_ffn[i], w_ffn_in[i].astype(BF16))
        h = matmul_residual(hidden, w_ffn_out[i].astype(BF16), h)
        h = ple_layer(h, norm_ple[i], w_ple_gate[i].astype(BF16), p[i].reshape(m, -1), w_ple_up[i].astype(BF16))
    return final_norm(h, norm_final).reshape(batch, seq, d_model)
```

```python
import functools

import numpy as np
import jax
import jax.numpy as jnp
from jax import lax
from jax.experimental import pallas as pl
from jax.experimental.pallas import tpu as pltpu

F32 = jnp.float32
BF16 = jnp.bfloat16

RMS_EPS = 1e-6
ROPE_THETA = 500000.0
NEG_INF = -1e30
FORCED_SCORE = 1e9

VMEM_LIMIT_BYTES = 52 * 1024 * 1024
LANES = 128

SSD_HEAD_DIM = 64
SSD_GROUPS = 8
SSD_STATE = 128
SSD_CONV = 4
SSD_CHUNK = 128

LRU_BLOCK_DIM = 256
LRU_CONV = 4
LRU_C = 8.0
LRU_ROWS = 256

NSA_HEAD_DIM = 128
NSA_KV_GROUPS = 4
ROT_DIM = NSA_HEAD_DIM // 4
CMP_LEN = 32
CMP_STRIDE = 16
SEL_LEN = 64
SEL_TOPK = 16
WINDOW = 512
Q_BLOCK = 128
SEL_KEY_TILE = 512

CONV_HIST = 8


def _cparams(*sem):
    return pltpu.CompilerParams(dimension_semantics=sem, vmem_limit_bytes=VMEM_LIMIT_BYTES)


def _dot(a, b):
    return jnp.dot(a, b, preferred_element_type=F32)


def _dot_nt(a, b):
    return lax.dot_general(a, b, (((1,), (1,)), ((), ())), preferred_element_type=F32)


def _split3(x):
    hi = x.astype(BF16)
    r1 = x - hi.astype(F32)
    mid = r1.astype(BF16)
    lo = (r1 - mid.astype(F32)).astype(BF16)
    return hi, mid, lo


def _dot_exact_rhs(x, e):
    hi, mid, lo = _split3(x)
    return _dot(hi, e) + _dot(mid, e) + _dot(lo, e)


def _dot_exact_lhs(e, x):
    hi, mid, lo = _split3(x)
    return _dot(e, hi) + _dot(e, mid) + _dot(e, lo)


def _rms_scale(x):
    return lax.rsqrt(jnp.mean(x * x, axis=-1, keepdims=True) + RMS_EPS)


def _softplus(x):
    return jnp.maximum(x, 0.0) + jnp.log(1.0 + jnp.exp(-jnp.abs(x)))


def _sigmoid(x):
    return 1.0 / (1.0 + jnp.exp(-x))


def _norm_mm_kernel(x_ref, nw_ref, w_ref, o_ref, u_ref):
    @pl.when(pl.program_id(1) == 0)
    def _():
        x = x_ref[...]
        u_ref[...] = (x * _rms_scale(x) * nw_ref[...]).astype(BF16)

    o_ref[...] = _dot(u_ref[...], w_ref[...]).astype(o_ref.dtype)


def norm_matmul(x, nw, w, out_dtype, tm=1024, tn=512):
    m, k = x.shape
    n = w.shape[1]
    return pl.pallas_call(
        _norm_mm_kernel,
        grid=(m // tm, n // tn),
        in_specs=[
            pl.BlockSpec((tm, k), lambda i, j: (i, 0)),
            pl.BlockSpec((1, k), lambda i, j: (0, 0)),
            pl.BlockSpec((k, tn), lambda i, j: (0, j)),
        ],
        out_specs=pl.BlockSpec((tm, tn), lambda i, j: (i, j)),
        out_shape=jax.ShapeDtypeStruct((m, n), out_dtype),
        scratch_shapes=[pltpu.VMEM((tm, k), BF16)],
        compiler_params=_cparams("parallel", "arbitrary"),
        name="norm_matmul",
    )(x, nw.reshape(1, k), w)


def _norm_swiglu_kernel(x_ref, nw_ref, wg_ref, wu_ref, o_ref, u_ref):
    @pl.when(pl.program_id(1) == 0)
    def _():
        x = x_ref[...]
        u_ref[...] = (x * _rms_scale(x) * nw_ref[...]).astype(BF16)

    u = u_ref[...]
    gate = _dot(u, wg_ref[...])
    up = _dot(u, wu_ref[...])
    o_ref[...] = (gate * _sigmoid(gate) * up).astype(o_ref.dtype)


def norm_swiglu_in(x, nw, w_in, tm=1024, tn=512):
    m, k = x.shape
    f = w_in.shape[1] // 2
    nj = f // tn
    return pl.pallas_call(
        _norm_swiglu_kernel,
        grid=(m // tm, nj),
        in_specs=[
            pl.BlockSpec((tm, k), lambda i, j: (i, 0)),
            pl.BlockSpec((1, k), lambda i, j: (0, 0)),
            pl.BlockSpec((k, tn), lambda i, j: (0, j)),
            pl.BlockSpec((k, tn), lambda i, j: (0, j + nj)),
        ],
        out_specs=pl.BlockSpec((tm, tn), lambda i, j: (i, j)),
        out_shape=jax.ShapeDtypeStruct((m, f), BF16),
        scratch_shapes=[pltpu.VMEM((tm, k), BF16)],
        compiler_params=_cparams("parallel", "arbitrary"),
        name="norm_swiglu_in",
    )(x, nw.reshape(1, k), w_in, w_in)


def _mm_residual_kernel(x_ref, w_ref, r_ref, o_ref):
    o_ref[...] = r_ref[...] + _dot(x_ref[...], w_ref[...])


def matmul_residual(x, w, res, tm=1024, tn=512):
    m, k = x.shape
    n = w.shape[1]
    return pl.pallas_call(
        _mm_residual_kernel,
        grid=(m // tm, n // tn),
        in_specs=[
            pl.BlockSpec((tm, k), lambda i, j: (i, 0)),
            pl.BlockSpec((k, tn), lambda i, j: (0, j)),
            pl.BlockSpec((tm, tn), lambda i, j: (i, j)),
        ],
        out_specs=pl.BlockSpec((tm, tn), lambda i, j: (i, j)),
        out_shape=jax.ShapeDtypeStruct((m, n), F32),
        compiler_params=_cparams("parallel", "arbitrary"),
        name="matmul_residual",
    )(x, w, res)


def _ple_kernel(tn, x_ref, nw_ref, wg_ref, p_ref, wu_ref, o_ref, u_ref):
    j = pl.program_id(1)

    @pl.when(j == 0)
    def _():
        x = x_ref[...]
        u_ref[...] = (x * _rms_scale(x) * nw_ref[...]).astype(BF16)

    gate = _sigmoid(_dot(u_ref[...], wg_ref[...]))
    up = _dot(p_ref[...].astype(BF16), wu_ref[...])
    col = pl.multiple_of(j * tn, tn)
    o_ref[...] = x_ref[:, pl.ds(col, tn)] + gate * up


def ple_layer(x, nw, w_gate, p, w_up, tm=1024, tn=1024):
    m, k = x.shape
    n = w_gate.shape[1]
    kp = p.shape[1]
    return pl.pallas_call(
        functools.partial(_ple_kernel, tn),
        grid=(m // tm, n // tn),
        in_specs=[
            pl.BlockSpec((tm, k), lambda i, j: (i, 0)),
            pl.BlockSpec((1, k), lambda i, j: (0, 0)),
            pl.BlockSpec((k, tn), lambda i, j: (0, j)),
            pl.BlockSpec((tm, kp), lambda i, j: (i, 0)),
            pl.BlockSpec((kp, tn), lambda i, j: (0, j)),
        ],
        out_specs=pl.BlockSpec((tm, tn), lambda i, j: (i, j)),
        out_shape=jax.ShapeDtypeStruct((m, n), F32),
        scratch_shapes=[pltpu.VMEM((tm, k), BF16)],
        compiler_params=_cparams("parallel", "arbitrary"),
        name="ple_layer",
    )(x, nw.reshape(1, k), w_gate, p, w_up)


def _final_norm_kernel(x_ref, nw_ref, o_ref):
    x = x_ref[...]
    o_ref[...] = x * _rms_scale(x) * nw_ref[...]


def final_norm(x, nw, tm=512):
    m, k = x.shape
    return pl.pallas_call(
        _final_norm_kernel,
        grid=(m // tm,),
        in_specs=[pl.BlockSpec((tm, k), lambda i: (i, 0)), pl.BlockSpec((1, k), lambda i: (0, 0))],
        out_specs=pl.BlockSpec((tm, k), lambda i: (i, 0)),
        out_shape=jax.ShapeDtypeStruct((m, k), F32),
        compiler_params=_cparams("parallel"),
        name="final_norm",
    )(x, nw.reshape(1, k))


def _causal_conv(pad_ref, w_ref, b_ref, rows, lo, width):
    taps = w_ref.shape[0]
    x = pad_ref[0:CONV_HIST + rows, lo:lo + width]
    acc = w_ref[0:1, lo:lo + width] * x
    for k in range(1, taps):
        acc = pltpu.roll(acc, 1, axis=0) + w_ref[k:k + 1, lo:lo + width] * x
    return acc[CONV_HIST:CONV_HIST + rows, :] + b_ref[:, lo:lo + width]


def _ssd_kernel(z_ref, xs_ref, bc_ref, dt_ref, cwx_ref, cbx_ref, cwbc_ref, cbbc_ref, dtb_ref, alog_ref,
                dskip_ref, nw_ref, tril_ref, ehead_ref, o_ref,
                xpad_ref, bcpad_ref, bcact_ref, state_ref, acs_ref, acst_ref, dt_s_ref, xdt_ref, ydiag_ref):
    q = SSD_CHUNK
    gw = xs_ref.shape[1] // SSD_GROUPS
    heads_per_group = gw // SSD_HEAD_DIM
    n_bc = bc_ref.shape[1] // 2

    @pl.when(pl.program_id(1) == 0)
    def _():
        xpad_ref[0:CONV_HIST, :] = jnp.zeros((CONV_HIST, xpad_ref.shape[1]), F32)
        bcpad_ref[0:CONV_HIST, :] = jnp.zeros((CONV_HIST, bcpad_ref.shape[1]), F32)
        state_ref[...] = jnp.zeros(state_ref.shape, F32)

    xpad_ref[CONV_HIST:CONV_HIST + q, :] = xs_ref[...]
    bcpad_ref[CONV_HIST:CONV_HIST + q, :] = bc_ref[...]

    for lo in range(0, bc_ref.shape[1], 512):
        v = _causal_conv(bcpad_ref, cwbc_ref, cbbc_ref, q, lo, 512)
        bcact_ref[:, lo:lo + 512] = v * _sigmoid(v)

    dt = _softplus(dt_ref[...] + dtb_ref[...])
    adt = dt * (-jnp.exp(alog_ref[...]))
    acs = _dot_exact_lhs(tril_ref[...], adt)
    acs_ref[...] = acs
    acst_ref[...] = acs.T
    dt_s_ref[...] = dt

    li = lax.broadcasted_iota(jnp.int32, (q, q), 0)
    si = lax.broadcasted_iota(jnp.int32, (q, q), 1)
    causal = li >= si

    for g in range(SSD_GROUPS):
        glo = g * gw
        e_g = ehead_ref[:, glo:glo + gw]
        xc = _causal_conv(xpad_ref, cwx_ref, cbx_ref, q, glo, gw)
        xs = xc * _sigmoid(xc)
        dt_g = _dot_exact_rhs(dt_s_ref[...], e_g)
        a_g = _dot_exact_rhs(acs_ref[...], e_g)
        a_last = a_g[q - 1:q, :]
        xdt = xs * dt_g
        xdt_ref[...] = xdt.astype(BF16)

        bm = bcact_ref[:, g * SSD_STATE:(g + 1) * SSD_STATE]
        cm = bcact_ref[:, n_bc + g * SSD_STATE:n_bc + (g + 1) * SSD_STATE]
        bm16 = bm.astype(BF16)
        cm16 = cm.astype(BF16)
        cb = _dot_nt(cm16, bm16)

        prev = state_ref[:, glo:glo + gw]
        y_off = _dot(cm16, prev.astype(BF16)) * jnp.exp(a_g)
        st_new = _dot(bm.T.astype(BF16), (xdt * jnp.exp(a_last - a_g)).astype(BF16))
        state_ref[:, glo:glo + gw] = prev * jnp.exp(a_last) + st_new

        for r in range(heads_per_group):
            h = g * heads_per_group + r
            seg = acs_ref[:, h:h + 1] - acst_ref[h:h + 1, :]
            decay = jnp.exp(jnp.where(causal, seg, NEG_INF))
            m_h = (cb * decay).astype(BF16)
            ydiag_ref[:, r * SSD_HEAD_DIM:(r + 1) * SSD_HEAD_DIM] = _dot(
                m_h, xdt_ref[:, r * SSD_HEAD_DIM:(r + 1) * SSD_HEAD_DIM])

        y = ydiag_ref[...] + y_off + dskip_ref[:, glo:glo + gw] * xs
        zg = z_ref[:, glo:glo + gw]
        y = y * (zg * _sigmoid(zg))
        y = y * _rms_scale(y) * nw_ref[:, glo:glo + gw]
        o_ref[:, glo:glo + gw] = y.astype(o_ref.dtype)

    xpad_ref[0:CONV_HIST, :] = xpad_ref[q:q + CONV_HIST, :]
    bcpad_ref[0:CONV_HIST, :] = bcpad_ref[q:q + CONV_HIST, :]


def ssd_core(zxd, batch, seq, conv_w, conv_b, dt_bias, a_log, d_skip, norm_w):
    m = zxd.shape[0]
    n_heads = dt_bias.shape[0]
    inner = n_heads * SSD_HEAD_DIM
    n_bc = SSD_GROUPS * SSD_STATE
    q = SSD_CHUNK
    nc = seq // q
    assert inner % (2 * n_bc) == 0 and n_heads <= LANES
    pad_h = LANES - n_heads

    row = lambda v: v.reshape(1, -1).astype(F32)
    padh = lambda v: jnp.pad(v.astype(F32), (0, pad_h)).reshape(1, LANES)
    tril = jnp.asarray(np.tril(np.ones((q, q), np.float32)), BF16)
    ehead = jnp.asarray(np.repeat(np.eye(LANES, n_heads, dtype=np.float32), SSD_HEAD_DIM, axis=1), BF16)

    rows = lambda b, c: b * nc + c
    full = lambda a: pl.BlockSpec(a.shape, lambda b, c: (0,) * a.ndim)
    consts = [conv_w[:, :inner].astype(F32), row(conv_b[:inner]), conv_w[:, inner:].astype(F32), row(conv_b[inner:]),
              padh(dt_bias), padh(a_log), row(jnp.repeat(d_skip, SSD_HEAD_DIM)), row(norm_w), tril, ehead]
    return pl.pallas_call(
        _ssd_kernel,
        grid=(batch, nc),
        in_specs=[
            pl.BlockSpec((q, inner), lambda b, c: (rows(b, c), 0)),
            pl.BlockSpec((q, inner), lambda b, c: (rows(b, c), 1)),
            pl.BlockSpec((q, 2 * n_bc), lambda b, c: (rows(b, c), 2 * inner // (2 * n_bc))),
            pl.BlockSpec((q, LANES), lambda b, c: (rows(b, c), (2 * inner + 2 * n_bc) // LANES)),
        ] + [full(a) for a in consts],
        out_specs=pl.BlockSpec((q, inner), lambda b, c: (rows(b, c), 0)),
        out_shape=jax.ShapeDtypeStruct((m, inner), BF16),
        scratch_shapes=[
            pltpu.VMEM((q + CONV_HIST, inner), F32),
            pltpu.VMEM((q + CONV_HIST, 2 * n_bc), F32),
            pltpu.VMEM((q, 2 * n_bc), F32),
            pltpu.VMEM((SSD_STATE, inner), F32),
            pltpu.VMEM((q, LANES), F32),
            pltpu.VMEM((LANES, q), F32),
            pltpu.VMEM((q, LANES), F32),
            pltpu.VMEM((q, inner // SSD_GROUPS), BF16),
            pltpu.VMEM((q, inner // SSD_GROUPS), F32),
        ],
        compiler_params=_cparams("parallel", "arbitrary"),
        name="ssd_core",
    )(zxd, zxd, zxd, zxd, *consts)


def _lru_kernel(gate_ref, xr_ref, cw_ref, cb_ref, wa_ref, ba_ref, wx_ref, bx_ref, ap_ref, o_ref,
                xpad_ref, a_ref, b_ref, h_ref, carry_ref):
    rows = gate_ref.shape[0]
    width = gate_ref.shape[1]

    @pl.when(pl.program_id(1) == 0)
    def _():
        xpad_ref[0:CONV_HIST, :] = jnp.zeros((CONV_HIST, width), F32)
        carry_ref[...] = jnp.zeros(carry_ref.shape, F32)

    xpad_ref[CONV_HIST:CONV_HIST + rows, :] = xr_ref[...]
    for k in range(width // LRU_BLOCK_DIM):
        lo = k * LRU_BLOCK_DIM
        sl = slice(lo, lo + LRU_BLOCK_DIM)
        xc = _causal_conv(xpad_ref, cw_ref, cb_ref, rows, lo, LRU_BLOCK_DIM)
        x16 = xc.astype(BF16)
        r_t = _sigmoid(_dot(x16, wa_ref[k]) + ba_ref[:, sl])
        i_t = _sigmoid(_dot(x16, wx_ref[k]) + bx_ref[:, sl])
        log_a = -LRU_C * r_t * _softplus(-ap_ref[:, sl])
        a_t = jnp.exp(log_a)
        a_ref[:, sl] = a_t
        b_ref[:, sl] = jnp.sqrt(1.0 - a_t * a_t) * (i_t * xc)
    xpad_ref[0:CONV_HIST, :] = xpad_ref[rows:rows + CONV_HIST, :]

    def step(t, h):
        h = a_ref[pl.ds(t, 1), :] * h + b_ref[pl.ds(t, 1), :]
        h_ref[pl.ds(t, 1), :] = h
        return h

    carry_ref[0:1, :] = lax.fori_loop(0, rows, step, carry_ref[0:1, :], unroll=8)
    o_ref[...] = (jax.nn.gelu(gate_ref[...], approximate=True) * h_ref[...]).astype(o_ref.dtype)


def lru_core(gx, batch, seq, conv_w, conv_b, wa, ba, wx, bx, a_param):
    m = gx.shape[0]
    width = gx.shape[1] // 2
    rows = LRU_ROWS
    nt = seq // rows
    row = lambda v: v.reshape(1, -1).astype(F32)
    consts = [conv_w.astype(F32), row(conv_b), wa.astype(BF16), row(ba), wx.astype(BF16), row(bx), row(a_param)]
    full = lambda a: pl.BlockSpec(a.shape, lambda b, t: (0,) * a.ndim)
    return pl.pallas_call(
        _lru_kernel,
        grid=(batch, nt),
        in_specs=[
            pl.BlockSpec((rows, width), lambda b, t: (b * nt + t, 0)),
            pl.BlockSpec((rows, width), lambda b, t: (b * nt + t, 1)),
        ] + [full(a) for a in consts],
        out_specs=pl.BlockSpec((rows, width), lambda b, t: (b * nt + t, 0)),
        out_shape=jax.ShapeDtypeStruct((m, width), BF16),
        scratch_shapes=[
            pltpu.VMEM((rows + CONV_HIST, width), F32),
            pltpu.VMEM((rows, width), F32),
            pltpu.VMEM((rows, width), F32),
            pltpu.VMEM((rows, width), F32),
            pltpu.VMEM((8, width), F32),
        ],
        compiler_params=_cparams("parallel", "arbitrary"),
        name="lru_core",
    )(gx, gx, *consts)


def _rope_tables(pos):
    half = ROT_DIM // 2
    inv = ROPE_THETA ** (-jnp.arange(half, dtype=F32) * 2.0 / ROT_DIM)
    ang = pos.astype(F32)[..., None] * inv
    cos, sin = jnp.cos(ang), jnp.sin(ang)
    shape = pos.shape + (NSA_HEAD_DIM - ROT_DIM,)
    zeros_h = jnp.zeros(pos.shape + (half,), F32)
    cos_t = jnp.concatenate([cos, cos, jnp.ones(shape, F32)], axis=-1)
    sin_lo = jnp.concatenate([-sin, zeros_h, jnp.zeros(shape, F32)], axis=-1)
    sin_hi = jnp.concatenate([zeros_h, sin, jnp.zeros(shape, F32)], axis=-1)
    return cos_t, sin_lo, sin_hi


def _apply_rope(x, cos_t, sin_lo, sin_hi):
    n = x.shape[1] // NSA_HEAD_DIM
    half = ROT_DIM // 2
    tile = lambda t: jnp.concatenate([t] * n, axis=1) if n > 1 else t
    up = pltpu.roll(x, x.shape[1] - half, axis=1)
    down = pltpu.roll(x, half, axis=1)
    return x * tile(cos_t) + up * tile(sin_lo) + down * tile(sin_hi)


def _rope_kernel(x_ref, cos_ref, slo_ref, shi_ref, o_ref):
    o_ref[...] = _apply_rope(x_ref[...], cos_ref[...], slo_ref[...], shi_ref[...]).astype(o_ref.dtype)


def nsa_rope(proj, tables, qdim, kvdim, tm=512):
    m = proj.shape[0]
    nq = qdim // kvdim
    src = lambda j: j + 2 * (j >= nq).astype(jnp.int32) + (j >= nq + 1).astype(jnp.int32)
    tspec = pl.BlockSpec((tm, NSA_HEAD_DIM), lambda i, j: (i, 0))
    return pl.pallas_call(
        _rope_kernel,
        grid=(m // tm, nq + 2),
        in_specs=[pl.BlockSpec((tm, kvdim), lambda i, j: (i, src(j))), tspec, tspec, tspec],
        out_specs=pl.BlockSpec((tm, kvdim), lambda i, j: (i, j)),
        out_shape=jax.ShapeDtypeStruct((m, qdim + 2 * kvdim), BF16),
        compiler_params=_cparams("parallel", "arbitrary"),
        name="nsa_rope",
    )(proj, *tables)


def _compress_kernel(is_key, r_ref, pe_ref, w1_ref, w2_ref, cos_ref, slo_ref, shi_ref, o_ref):
    half_k = w1_ref.shape[0] // 2
    nrows = r_ref.shape[0]
    r = r_ref[...]
    top = _dot((r + pe_ref[:, 0:half_k]).astype(BF16), w1_ref[0:half_k, :])
    bot = _dot((r + pe_ref[:, half_k:]).astype(BF16), w1_ref[half_k:, :])
    pre = top + pltpu.roll(bot, nrows - 1, axis=0)
    out = _dot(jax.nn.gelu(pre, approximate=True).astype(BF16), w2_ref[...])
    if is_key:
        o_ref[...] = _apply_rope(out, cos_ref[...], slo_ref[...], shi_ref[...]).astype(o_ref.dtype)
    else:
        o_ref[...] = out.T.astype(o_ref.dtype)


def nsa_compress(tok, pe, w1, w2, tables, is_key):
    b, g, nrows, kdim = tok.shape
    d = w2.shape[0]
    tspec = pl.BlockSpec((None, nrows, d), lambda i, j: (i, 0, 0))
    out_block = (None, None, nrows, d) if is_key else (None, None, d, nrows)
    out_dims = (b, g, nrows, d) if is_key else (b, g, d, nrows)
    return pl.pallas_call(
        functools.partial(_compress_kernel, is_key),
        grid=(b, g),
        in_specs=[
            pl.BlockSpec((None, None, nrows, kdim), lambda i, j: (i, j, 0, 0)),
            pl.BlockSpec((1, 2 * kdim), lambda i, j: (0, 0)),
            pl.BlockSpec((2 * kdim, d), lambda i, j: (0, 0)),
            pl.BlockSpec((d, d), lambda i, j: (0, 0)),
            tspec, tspec, tspec,
        ],
        out_specs=pl.BlockSpec(out_block, lambda i, j: (i, j, 0, 0)),
        out_shape=jax.ShapeDtypeStruct(out_dims, BF16),
        compiler_params=_cparams("parallel", "arbitrary"),
        name="nsa_compress_k" if is_key else "nsa_compress_v",
    )(tok, pe.reshape(1, 2 * kdim).astype(F32), w1.astype(BF16), w2.astype(BF16), *tables)


def _nsa_attn_kernel(n_sel, q_ref, kcmp_ref, vcmp_ref, ks_ref, vs_ref, kw_ref, vw_ref, gate_ref, c2s_ref, o_ref,
                     impt_ref, sel_ref, m_ref, l_ref, acc_ref, out_ref, q4_ref):
    tq = Q_BLOCK
    d = NSA_HEAD_DIM
    hpg = q_ref.shape[1] // d
    n_cmp_pad = kcmp_ref.shape[0]
    n_sel_pad = c2s_ref.shape[0]
    scale = d ** -0.5
    t0 = pl.program_id(2) * tq
    t_row = t0 + lax.broadcasted_iota(jnp.int32, (1, tq), 1)
    gates = _sigmoid(gate_ref[...]).T
    head = lambda r: slice(r * d, (r + 1) * d)
    cols = lambda r: slice(r * tq, (r + 1) * tq)
    per_head = lambda a: jnp.concatenate([a] * hpg, axis=1)
    for r in range(hpg):
        q4_ref[cols(r), :] = q_ref[:, head(r)]

    n_col = lax.broadcasted_iota(jnp.int32, (n_cmp_pad, 1), 0)
    ok_c = (n_col * CMP_STRIDE + (CMP_LEN - 1) <= t_row) & (n_col < n_cmp_pad - 1)
    bias_c = jnp.where(ok_c, 0.0, NEG_INF)
    keep_c = jnp.where(ok_c, 1.0, 0.0)
    s = _dot_nt(kcmp_ref[...], q4_ref[...]) * scale + per_head(bias_c)
    e = jnp.exp(s - jnp.max(s, axis=0, keepdims=True)) * per_head(keep_c)
    den = jnp.sum(e, axis=0, keepdims=True)
    p = e * jnp.where(den > 0.0, 1.0 / den, 0.0)
    o_c = _dot(vcmp_ref[...], p.astype(BF16))
    p_sum = p[:, cols(0)]
    for r in range(1, hpg):
        p_sum = p_sum + p[:, cols(r)]
    for r in range(hpg):
        out_ref[:, cols(r)] = gates[3 * r:3 * r + 1, :] * o_c[:, cols(r)]

    imp = _dot_exact_lhs(c2s_ref[...], p_sum)
    j_col = lax.broadcasted_iota(jnp.int32, (n_sel_pad, 1), 0)
    cur = t_row // SEL_LEN
    forced = (j_col == 0) | (j_col == cur) | (j_col == cur - 1)
    imp = jnp.where(forced, FORCED_SCORE, imp)
    impt_ref[...] = jnp.where(j_col <= cur, imp, -jnp.inf)
    n_blk = n_sel // 8
    blks = [impt_ref[jb * 8:(jb + 1) * 8, :] for jb in range(n_blk)]
    cnt = [jnp.zeros((8, tq), F32) for _ in range(n_blk)]
    sub = lax.broadcasted_iota(jnp.int32, (8, 1), 0)
    for i in range(n_sel):
        other = impt_ref[i:i + 1, :]
        for jb in range(n_blk):
            if jb * 8 > i:
                ahead = other >= blks[jb]
            elif jb * 8 + 7 < i:
                ahead = other > blks[jb]
            else:
                ahead = (other > blks[jb]) | ((other == blks[jb]) & (sub > i - jb * 8))
            cnt[jb] = cnt[jb] + jnp.where(ahead, 1.0, 0.0)
    sel_t = [jnp.where(c < float(SEL_TOPK), 1.0, 0.0) for c in cnt]
    if n_sel_pad > n_sel:
        sel_t.append(jnp.zeros((n_sel_pad - n_sel, tq), F32))
    sel_ref[...] = jnp.concatenate(sel_t, axis=0).astype(BF16)

    kt_n = SEL_KEY_TILE
    blocks_per_tile = kt_n // SEL_LEN
    key_blk = lax.broadcasted_iota(jnp.int32, (kt_n, n_sel_pad), 0) // SEL_LEN
    blk_lane = lax.broadcasted_iota(jnp.int32, (kt_n, n_sel_pad), 1)
    key_row = lax.broadcasted_iota(jnp.int32, (kt_n, 1), 0)
    m_ref[...] = jnp.full(m_ref.shape, NEG_INF, F32)
    l_ref[...] = jnp.zeros(l_ref.shape, F32)
    acc_ref[...] = jnp.zeros(acc_ref.shape, F32)

    def sel_scores(kt):
        k0 = pl.multiple_of(kt * kt_n, kt_n)
        expand = jnp.where(blk_lane == key_blk + kt * blocks_per_tile, 1.0, 0.0).astype(BF16)
        picked = _dot(expand, sel_ref[...])
        bias = jnp.where((picked > 0.5) & (k0 + key_row <= t_row), 0.0, NEG_INF)
        return _dot_nt(ks_ref[pl.ds(k0, kt_n), :], q4_ref[...]) * scale + per_head(bias)

    def sel_update(kt, s):
        k0 = pl.multiple_of(kt * kt_n, kt_n)
        m_old = m_ref[...]
        m_new = jnp.maximum(m_old, jnp.max(s, axis=0, keepdims=True))
        alpha = jnp.exp(m_old - m_new)
        p = jnp.exp(s - m_new)
        l_ref[...] = alpha * l_ref[...] + jnp.sum(p, axis=0, keepdims=True)
        acc_ref[...] = alpha * acc_ref[...] + _dot(vs_ref[:, pl.ds(k0, kt_n)], p.astype(BF16))
        m_ref[...] = m_new

    def sel_step(pair, carry):
        s_a = sel_scores(2 * pair)
        s_b = sel_scores(2 * pair + 1)
        sel_update(2 * pair, s_a)
        sel_update(2 * pair + 1, s_b)
        return carry

    lax.fori_loop(0, (t0 + tq + 2 * kt_n - 1) // (2 * kt_n), sel_step, 0)

    n_win = WINDOW + tq
    w0 = pl.multiple_of(jnp.maximum(t0 - WINDOW, 0), tq)
    kp = w0 + lax.broadcasted_iota(jnp.int32, (n_win, 1), 0)
    bias_w = jnp.where((kp <= t_row) & (kp > t_row - WINDOW), 0.0, NEG_INF)
    s = _dot_nt(kw_ref[pl.ds(w0, n_win), :], q4_ref[...]) * scale + per_head(bias_w)
    e = jnp.exp(s - jnp.max(s, axis=0, keepdims=True))
    p = e * (1.0 / jnp.sum(e, axis=0, keepdims=True))
    o_w = _dot(vw_ref[:, pl.ds(w0, n_win)], p.astype(BF16))
    o_s = acc_ref[...] * (1.0 / l_ref[...])
    for r in range(hpg):
        o_r = (out_ref[:, cols(r)] + gates[3 * r + 1:3 * r + 2, :] * o_s[:, cols(r)]
               + gates[3 * r + 2:3 * r + 3, :] * o_w[:, cols(r)])
        o_ref[:, head(r)] = o_r.T.astype(o_ref.dtype)


def nsa_attention(qk, v_sw_t, k_cmp, v_cmp_t, proj, c2s_t, batch, seq, qdim, kvdim, gate_col):
    m = qk.shape[0]
    d = NSA_HEAD_DIM
    groups = kvdim // d
    gq = qdim // groups
    hpg = gq // d
    nq = seq // Q_BLOCK
    n_sel = seq // SEL_LEN
    assert n_sel % 8 == 0 and seq % (2 * SEL_KEY_TILE) == 0 and seq >= WINDOW + Q_BLOCK
    n_cmp_pad = k_cmp.shape[2]
    n_sel_pad = c2s_t.shape[0]
    k_blk = lambda off: pl.BlockSpec((seq, d), lambda b, g, t: (b, off + g))
    v_blk = lambda off: pl.BlockSpec((d, seq), lambda b, g, t: (off + g, b))
    return pl.pallas_call(
        functools.partial(_nsa_attn_kernel, n_sel),
        grid=(batch, groups, nq),
        in_specs=[
            pl.BlockSpec((Q_BLOCK, gq), lambda b, g, t: (b * nq + t, g)),
            pl.BlockSpec((None, None, n_cmp_pad, d), lambda b, g, t: (b, g, 0, 0)),
            pl.BlockSpec((None, None, d, n_cmp_pad), lambda b, g, t: (b, g, 0, 0)),
            k_blk(qdim // d), v_blk(0), k_blk(qdim // d + groups), v_blk(groups),
            pl.BlockSpec((Q_BLOCK, LANES), lambda b, g, t: (b * nq + t, gate_col // LANES + g)),
            pl.BlockSpec(c2s_t.shape, lambda b, g, t: (0, 0)),
        ],
        out_specs=pl.BlockSpec((Q_BLOCK, gq), lambda b, g, t: (b * nq + t, g)),
        out_shape=jax.ShapeDtypeStruct((m, qdim), BF16),
        scratch_shapes=[
            pltpu.VMEM((n_sel_pad, Q_BLOCK), F32),
            pltpu.VMEM((n_sel_pad, Q_BLOCK), BF16),
            pltpu.VMEM((1, hpg * Q_BLOCK), F32),
            pltpu.VMEM((1, hpg * Q_BLOCK), F32),
            pltpu.VMEM((d, hpg * Q_BLOCK), F32),
            pltpu.VMEM((d, hpg * Q_BLOCK), F32),
            pltpu.VMEM((hpg * Q_BLOCK, d), BF16),
        ],
        compiler_params=_cparams("parallel", "parallel", "arbitrary"),
        name="nsa_attention",
    )(qk, k_cmp, v_cmp_t, qk, v_sw_t, qk, v_sw_t, proj, c2s_t)


def _cmp_to_sel(seq):
    n_cmp = (seq - CMP_LEN) // CMP_STRIDE + 1
    n_sel = seq // SEL_LEN
    c_start = np.arange(n_cmp)[:, None] * CMP_STRIDE
    s_start = np.arange(n_sel)[None, :] * SEL_LEN
    overlap = np.clip(np.minimum(c_start + CMP_LEN, s_start + SEL_LEN) - np.maximum(c_start, s_start), 0, None)
    out = np.zeros((max(n_sel, LANES), seq // CMP_STRIDE), np.float32)
    out[:n_sel, :n_cmp] = (overlap / CMP_STRIDE).T
    return jnp.asarray(out, BF16)


def _pad_cols(w, n):
    return jnp.pad(w, ((0, 0), (0, n - w.shape[1])))


def ssd_mixer(h, nw, batch, seq, in_proj, conv_w, conv_b, dt_bias, a_log, d_skip, norm_w, out_proj):
    n_heads = dt_bias.shape[0]
    inner = n_heads * SSD_HEAD_DIM
    conv_dim = inner + 2 * SSD_GROUPS * SSD_STATE
    used = inner + conv_dim + n_heads
    w = _pad_cols(in_proj, -(-used // 512) * 512).astype(BF16)
    zxd = norm_matmul(h, nw, w, F32, tn=1536)
    y = ssd_core(zxd, batch, seq, conv_w, conv_b, dt_bias, a_log, d_skip, norm_w)
    return matmul_residual(y, out_proj.astype(BF16), h)


def rglru_mixer(h, nw, batch, seq, in_proj, conv_w, conv_b, wa, ba, wx, bx, a_param, out_proj):
    gx = norm_matmul(h, nw, in_proj.astype(BF16), F32, tn=1024)
    y = lru_core(gx, batch, seq, conv_w, conv_b, wa, ba, wx, bx, a_param)
    return matmul_residual(y, out_proj.astype(BF16), h)


def nsa_mixer(h, nw, positions, batch, seq, in_proj, cmp_pe, cmp_w1, cmp_w2, out_proj):
    d = NSA_HEAD_DIM
    groups = NSA_KV_GROUPS
    kvdim = groups * d
    qdim = out_proj.shape[0]
    n_heads = qdim // d
    hpg = n_heads // groups
    gate_col = qdim + 6 * kvdim

    wg = in_proj[:, gate_col:].reshape(-1, groups, 3 * hpg)
    wg = jnp.pad(wg, ((0, 0), (0, 0), (0, LANES - 3 * hpg))).reshape(-1, groups * LANES)
    w = jnp.concatenate([in_proj[:, :gate_col], wg], axis=1).astype(BF16)
    proj = norm_matmul(h, nw, w, F32, tn=1408)

    tables = tuple(t.reshape(batch * seq, d) for t in _rope_tables(positions))
    qk = nsa_rope(proj, tables, qdim, kvdim)
    v_sw_t = jnp.concatenate([proj[:, qdim + 3 * kvdim:qdim + 4 * kvdim],
                              proj[:, qdim + 5 * kvdim:qdim + 6 * kvdim]], axis=1).astype(BF16).T

    n_rows = seq // CMP_STRIDE
    cmp_end = jnp.minimum(jnp.arange(n_rows) * CMP_STRIDE + CMP_LEN - 1, seq - 1)
    cmp_tables = _rope_tables(positions[:, cmp_end])

    def blocks(col):
        tok = proj[:, col:col + kvdim].reshape(batch, n_rows, CMP_STRIDE, groups, d)
        return tok.transpose(0, 3, 1, 2, 4).reshape(batch, groups, n_rows, CMP_STRIDE * d)

    k_cmp = nsa_compress(blocks(qdim), cmp_pe[0], cmp_w1[0], cmp_w2[0], cmp_tables, True)
    v_cmp_t = nsa_compress(blocks(qdim + kvdim), cmp_pe[1], cmp_w1[1], cmp_w2[1], cmp_tables, False)

    o = nsa_attention(qk, v_sw_t, k_cmp, v_cmp_t, proj, _cmp_to_sel(seq), batch, seq, qdim, kvdim, gate_col)
    return matmul_residual(o, out_proj.astype(BF16), h)


def kernel(x, p, positions, norm_mix, norm_ffn, norm_ple, w_ple_up, w_ple_gate, w_ffn_in, w_ffn_out, norm_final, ssd_in_proj, ssd_conv_w, ssd_conv_b, ssd_dt_bias, ssd_a_log, ssd_d, ssd_norm, ssd_out_proj, lru_in_proj, lru_conv_w, lru_conv_b, lru_wa, lru_ba, lru_wx, lru_bx, lru_a_param, lru_out_proj, nsa_in_proj, nsa_cmp_pe, nsa_cmp_w1, nsa_cmp_w2, nsa_out_proj):
    batch, seq, d_model = x.shape
    depth = norm_mix.shape[0]
    n_mixers = 3
    m = batch * seq
    h = x.reshape(m, d_model)
    for i in range(depth):
        kind, j = i % n_mixers, i // n_mixers
        if kind == 0:
            h = ssd_mixer(h, norm_mix[i], batch, seq, ssd_in_proj[j], ssd_conv_w[j], ssd_conv_b[j], ssd_dt_bias[j],
                          ssd_a_log[j], ssd_d[j], ssd_norm[j], ssd_out_proj[j])
        elif kind == 1:
            h = rglru_mixer(h, norm_mix[i], batch, seq, lru_in_proj[j], lru_conv_w[j], lru_conv_b[j], lru_wa[j],
                            lru_ba[j], lru_wx[j], lru_bx[j], lru_a_param[j], lru_out_proj[j])
        else:
            h = nsa_mixer(h, norm_mix[i], positions, batch, seq, nsa_in_proj[j], nsa_cmp_pe[j], nsa_cmp_w1[j],
                          nsa_cmp_w2[j], nsa_out_proj[j])
        hidden = norm_swiglu_in(h, norm_ffn[i], w_ffn_in[i].astype(BF16))
        h = matmul_residual(hidden, w_ffn_out[i].astype(BF16), h)
        h = ple_layer(h, norm_ple[i], w_ple_gate[i].astype(BF16), p[i].reshape(m, -1), w_ple_up[i].astype(BF16))
    return final_norm(h, norm_final).reshape(batch, seq, d_model)
```

```python
import functools

import numpy as np
import jax
import jax.numpy as jnp
from jax import lax
from jax.experimental import pallas as pl
from jax.experimental.pallas import tpu as pltpu

F32 = jnp.float32
BF16 = jnp.bfloat16

RMS_EPS = 1e-6
ROPE_THETA = 500000.0
NEG_INF = -1e30
FORCED_SCORE = 1e9
LOG2_E = 1.4426950408889634

VMEM_LIMIT_BYTES = 52 * 1024 * 1024
LANES = 128

SSD_HEAD_DIM = 64
SSD_GROUPS = 8
SSD_STATE = 128
SSD_CONV = 4
SSD_CHUNK = 128

LRU_BLOCK_DIM = 256
LRU_CONV = 4
LRU_C = 8.0
LRU_ROWS = 256

NSA_HEAD_DIM = 128
NSA_KV_GROUPS = 4
ROT_DIM = NSA_HEAD_DIM // 4
CMP_LEN = 32
CMP_STRIDE = 16
SEL_LEN = 64
SEL_TOPK = 16
WINDOW = 512
Q_BLOCK = 128
SEL_KEY_TILE = 512

CONV_HIST = 8


def _cparams(*sem):
    return pltpu.CompilerParams(dimension_semantics=sem, vmem_limit_bytes=VMEM_LIMIT_BYTES)


def _dot(a, b):
    return jnp.dot(a, b, preferred_element_type=F32)


def _dot_nt(a, b):
    return lax.dot_general(a, b, (((1,), (1,)), ((), ())), preferred_element_type=F32)


def _split3(x):
    hi = x.astype(BF16)
    r1 = x - hi.astype(F32)
    mid = r1.astype(BF16)
    lo = (r1 - mid.astype(F32)).astype(BF16)
    return hi, mid, lo


def _dot_exact_rhs(x, e):
    hi, mid, lo = _split3(x)
    return _dot(hi, e) + _dot(mid, e) + _dot(lo, e)


def _dot_exact_lhs(e, x):
    hi, mid, lo = _split3(x)
    return _dot(e, hi) + _dot(e, mid) + _dot(e, lo)


def _rms_scale(x):
    return lax.rsqrt(jnp.mean(x * x, axis=-1, keepdims=True) + RMS_EPS)


def _softplus(x):
    return jnp.maximum(x, 0.0) + jnp.log(1.0 + jnp.exp(-jnp.abs(x)))


def _sigmoid(x):
    return 1.0 / (1.0 + jnp.exp(-x))


def _norm_mm_kernel(x_ref, nw_ref, w_ref, o_ref, u_ref):
    @pl.when(pl.program_id(1) == 0)
    def _():
        x = x_ref[...]
        u_ref[...] = (x * _rms_scale(x) * nw_ref[...]).astype(BF16)

    o_ref[...] = _dot(u_ref[...], w_ref[...]).astype(o_ref.dtype)


def norm_matmul(x, nw, w, out_dtype, tm=1024, tn=512):
    m, k = x.shape
    n = w.shape[1]
    return pl.pallas_call(
        _norm_mm_kernel,
        grid=(m // tm, n // tn),
        in_specs=[
            pl.BlockSpec((tm, k), lambda i, j: (i, 0)),
            pl.BlockSpec((1, k), lambda i, j: (0, 0)),
            pl.BlockSpec((k, tn), lambda i, j: (0, j)),
        ],
        out_specs=pl.BlockSpec((tm, tn), lambda i, j: (i, j)),
        out_shape=jax.ShapeDtypeStruct((m, n), out_dtype),
        scratch_shapes=[pltpu.VMEM((tm, k), BF16)],
        compiler_params=_cparams("parallel", "arbitrary"),
        name="norm_matmul",
    )(x, nw.reshape(1, k), w)


def _norm_mm_t_kernel(x_ref, nw_ref, wt_ref, o_ref):
    x = x_ref[...]
    u = (x * _rms_scale(x) * nw_ref[...]).astype(BF16)
    o_ref[...] = _dot_nt(wt_ref[...], u).astype(o_ref.dtype)


def norm_matmul_t(x, nw, w_t, out_dtype, tm=1024):
    m, k = x.shape
    n = w_t.shape[0]
    return pl.pallas_call(
        _norm_mm_t_kernel,
        grid=(m // tm,),
        in_specs=[
            pl.BlockSpec((tm, k), lambda i: (i, 0)),
            pl.BlockSpec((1, k), lambda i: (0, 0)),
            pl.BlockSpec((n, k), lambda i: (0, 0)),
        ],
        out_specs=pl.BlockSpec((n, tm), lambda i: (0, i)),
        out_shape=jax.ShapeDtypeStruct((n, m), out_dtype),
        compiler_params=_cparams("parallel"),
        name="norm_matmul_t",
    )(x, nw.reshape(1, k), w_t)


def _norm_swiglu_kernel(x_ref, nw_ref, wg_ref, wu_ref, o_ref, u_ref):
    @pl.when(pl.program_id(1) == 0)
    def _():
        x = x_ref[...]
        u_ref[...] = (x * _rms_scale(x) * nw_ref[...]).astype(BF16)

    u = u_ref[...]
    gate = _dot(u, wg_ref[...])
    up = _dot(u, wu_ref[...])
    o_ref[...] = (gate * _sigmoid(gate) * up).astype(o_ref.dtype)


def norm_swiglu_in(x, nw, w_in, tm=1024, tn=512):
    m, k = x.shape
    f = w_in.shape[1] // 2
    nj = f // tn
    return pl.pallas_call(
        _norm_swiglu_kernel,
        grid=(m // tm, nj),
        in_specs=[
            pl.BlockSpec((tm, k), lambda i, j: (i, 0)),
            pl.BlockSpec((1, k), lambda i, j: (0, 0)),
            pl.BlockSpec((k, tn), lambda i, j: (0, j)),
            pl.BlockSpec((k, tn), lambda i, j: (0, j + nj)),
        ],
        out_specs=pl.BlockSpec((tm, tn), lambda i, j: (i, j)),
        out_shape=jax.ShapeDtypeStruct((m, f), BF16),
        scratch_shapes=[pltpu.VMEM((tm, k), BF16)],
        compiler_params=_cparams("parallel", "arbitrary"),
        name="norm_swiglu_in",
    )(x, nw.reshape(1, k), w_in, w_in)


def _mm_residual_kernel(x_ref, w_ref, r_ref, o_ref):
    o_ref[...] = r_ref[...] + _dot(x_ref[...], w_ref[...])


def matmul_residual(x, w, res, tm=1024, tn=512):
    m, k = x.shape
    n = w.shape[1]
    return pl.pallas_call(
        _mm_residual_kernel,
        grid=(m // tm, n // tn),
        in_specs=[
            pl.BlockSpec((tm, k), lambda i, j: (i, 0)),
            pl.BlockSpec((k, tn), lambda i, j: (0, j)),
            pl.BlockSpec((tm, tn), lambda i, j: (i, j)),
        ],
        out_specs=pl.BlockSpec((tm, tn), lambda i, j: (i, j)),
        out_shape=jax.ShapeDtypeStruct((m, n), F32),
        compiler_params=_cparams("parallel", "arbitrary"),
        name="matmul_residual",
    )(x, w, res)


def _ple_kernel(tn, x_ref, nw_ref, wg_ref, p_ref, wu_ref, o_ref, u_ref):
    j = pl.program_id(1)

    @pl.when(j == 0)
    def _():
        x = x_ref[...]
        u_ref[...] = (x * _rms_scale(x) * nw_ref[...]).astype(BF16)

    gate = _sigmoid(_dot(u_ref[...], wg_ref[...]))
    up = _dot(p_ref[...].astype(BF16), wu_ref[...])
    col = pl.multiple_of(j * tn, tn)
    o_ref[...] = x_ref[:, pl.ds(col, tn)] + gate * up


def ple_layer(x, nw, w_gate, p, w_up, tm=1024, tn=1024):
    m, k = x.shape
    n = w_gate.shape[1]
    kp = p.shape[1]
    return pl.pallas_call(
        functools.partial(_ple_kernel, tn),
        grid=(m // tm, n // tn),
        in_specs=[
            pl.BlockSpec((tm, k), lambda i, j: (i, 0)),
            pl.BlockSpec((1, k), lambda i, j: (0, 0)),
            pl.BlockSpec((k, tn), lambda i, j: (0, j)),
            pl.BlockSpec((tm, kp), lambda i, j: (i, 0)),
            pl.BlockSpec((kp, tn), lambda i, j: (0, j)),
        ],
        out_specs=pl.BlockSpec((tm, tn), lambda i, j: (i, j)),
        out_shape=jax.ShapeDtypeStruct((m, n), F32),
        scratch_shapes=[pltpu.VMEM((tm, k), BF16)],
        compiler_params=_cparams("parallel", "arbitrary"),
        name="ple_layer",
    )(x, nw.reshape(1, k), w_gate, p, w_up)


def _ple_final_kernel(x_ref, nw_ref, wg_ref, p_ref, wu_ref, fw_ref, o_ref):
    x = x_ref[...]
    u = (x * _rms_scale(x) * nw_ref[...]).astype(BF16)
    y = x + _sigmoid(_dot(u, wg_ref[...])) * _dot(p_ref[...].astype(BF16), wu_ref[...])
    o_ref[...] = y * _rms_scale(y) * fw_ref[...]


def ple_final_layer(x, nw, w_gate, p, w_up, final_nw, tm=512):
    m, k = x.shape
    n = w_gate.shape[1]
    kp = p.shape[1]
    const = lambda shape: pl.BlockSpec(shape, lambda i: (0, 0))
    return pl.pallas_call(
        _ple_final_kernel,
        grid=(m // tm,),
        in_specs=[pl.BlockSpec((tm, k), lambda i: (i, 0)), const((1, k)), const((k, n)),
                  pl.BlockSpec((tm, kp), lambda i: (i, 0)), const((kp, n)), const((1, n))],
        out_specs=pl.BlockSpec((tm, n), lambda i: (i, 0)),
        out_shape=jax.ShapeDtypeStruct((m, n), F32),
        compiler_params=_cparams("parallel"),
        name="ple_final_layer",
    )(x, nw.reshape(1, k), w_gate, p, w_up, final_nw.reshape(1, n))


def _causal_conv(pad_ref, w_ref, b_ref, rows, lo, width):
    taps = w_ref.shape[0]
    x = pad_ref[0:CONV_HIST + rows, lo:lo + width]
    acc = w_ref[0:1, lo:lo + width] * x
    for k in range(1, taps):
        acc = pltpu.roll(acc, 1, axis=0) + w_ref[k:k + 1, lo:lo + width] * x
    return acc[CONV_HIST:CONV_HIST + rows, :] + b_ref[:, lo:lo + width]


def _ssd_kernel(z_ref, xs_ref, bc_ref, dt_ref, cwx_ref, cbx_ref, cwbc_ref, cbbc_ref, dtb_ref, alog_ref,
                dskip_ref, nw_ref, tril_ref, ehead_ref, o_ref,
                xpad_ref, bcpad_ref, bcact_ref, state_ref, acs_ref, acst_ref, dt_s_ref, xdt_ref, ydiag_ref):
    q = SSD_CHUNK
    gw = xs_ref.shape[1] // SSD_GROUPS
    heads_per_group = gw // SSD_HEAD_DIM
    n_bc = bc_ref.shape[1] // 2

    @pl.when(pl.program_id(1) == 0)
    def _():
        xpad_ref[0:CONV_HIST, :] = jnp.zeros((CONV_HIST, xpad_ref.shape[1]), F32)
        bcpad_ref[0:CONV_HIST, :] = jnp.zeros((CONV_HIST, bcpad_ref.shape[1]), F32)
        state_ref[...] = jnp.zeros(state_ref.shape, F32)

    xpad_ref[CONV_HIST:CONV_HIST + q, :] = xs_ref[...]
    bcpad_ref[CONV_HIST:CONV_HIST + q, :] = bc_ref[...]

    for lo in range(0, bc_ref.shape[1], 512):
        v = _causal_conv(bcpad_ref, cwbc_ref, cbbc_ref, q, lo, 512)
        bcact_ref[:, lo:lo + 512] = v * _sigmoid(v)

    dt = _softplus(dt_ref[...] + dtb_ref[...])
    adt = dt * (-jnp.exp(alog_ref[...]))
    acs = _dot_exact_lhs(tril_ref[...], adt)
    acs_ref[...] = acs
    acst_ref[...] = acs.T
    dt_s_ref[...] = dt

    li = lax.broadcasted_iota(jnp.int32, (q, q), 0)
    si = lax.broadcasted_iota(jnp.int32, (q, q), 1)
    causal = li >= si

    for g in range(SSD_GROUPS):
        glo = g * gw
        e_g = ehead_ref[:, glo:glo + gw]
        xc = _causal_conv(xpad_ref, cwx_ref, cbx_ref, q, glo, gw)
        xs = xc * _sigmoid(xc)
        dt_g = _dot_exact_rhs(dt_s_ref[...], e_g)
        a_g = _dot_exact_rhs(acs_ref[...], e_g)
        a_last = a_g[q - 1:q, :]
        xdt = xs * dt_g
        xdt_ref[...] = xdt.astype(BF16)

        bm = bcact_ref[:, g * SSD_STATE:(g + 1) * SSD_STATE]
        cm = bcact_ref[:, n_bc + g * SSD_STATE:n_bc + (g + 1) * SSD_STATE]
        bm16 = bm.astype(BF16)
        cm16 = cm.astype(BF16)
        cb = _dot_nt(cm16, bm16)

        prev = state_ref[:, glo:glo + gw]
        y_off = _dot(cm16, prev.astype(BF16)) * jnp.exp(a_g)
        st_new = _dot(bm.T.astype(BF16), (xdt * jnp.exp(a_last - a_g)).astype(BF16))
        state_ref[:, glo:glo + gw] = prev * jnp.exp(a_last) + st_new

        for r in range(heads_per_group):
            h = g * heads_per_group + r
            seg = acs_ref[:, h:h + 1] - acst_ref[h:h + 1, :]
            decay = jnp.exp(jnp.where(causal, seg, NEG_INF))
            m_h = (cb * decay).astype(BF16)
            ydiag_ref[:, r * SSD_HEAD_DIM:(r + 1) * SSD_HEAD_DIM] = _dot(
                m_h, xdt_ref[:, r * SSD_HEAD_DIM:(r + 1) * SSD_HEAD_DIM])

        y = ydiag_ref[...] + y_off + dskip_ref[:, glo:glo + gw] * xs
        zg = z_ref[:, glo:glo + gw]
        y = y * (zg * _sigmoid(zg))
        y = y * _rms_scale(y) * nw_ref[:, glo:glo + gw]
        o_ref[:, glo:glo + gw] = y.astype(o_ref.dtype)

    xpad_ref[0:CONV_HIST, :] = xpad_ref[q:q + CONV_HIST, :]
    bcpad_ref[0:CONV_HIST, :] = bcpad_ref[q:q + CONV_HIST, :]


def ssd_core(zxd, batch, seq, conv_w, conv_b, dt_bias, a_log, d_skip, norm_w):
    m = zxd.shape[0]
    n_heads = dt_bias.shape[0]
    inner = n_heads * SSD_HEAD_DIM
    n_bc = SSD_GROUPS * SSD_STATE
    q = SSD_CHUNK
    nc = seq // q
    assert inner % (2 * n_bc) == 0 and n_heads <= LANES
    pad_h = LANES - n_heads

    row = lambda v: v.reshape(1, -1).astype(F32)
    padh = lambda v: jnp.pad(v.astype(F32), (0, pad_h)).reshape(1, LANES)
    tril = jnp.asarray(np.tril(np.ones((q, q), np.float32)), BF16)
    ehead = jnp.asarray(np.repeat(np.eye(LANES, n_heads, dtype=np.float32), SSD_HEAD_DIM, axis=1), BF16)

    rows = lambda b, c: b * nc + c
    full = lambda a: pl.BlockSpec(a.shape, lambda b, c: (0,) * a.ndim)
    consts = [conv_w[:, :inner].astype(F32), row(conv_b[:inner]), conv_w[:, inner:].astype(F32), row(conv_b[inner:]),
              padh(dt_bias), padh(a_log), row(jnp.repeat(d_skip, SSD_HEAD_DIM)), row(norm_w), tril, ehead]
    return pl.pallas_call(
        _ssd_kernel,
        grid=(batch, nc),
        in_specs=[
            pl.BlockSpec((q, inner), lambda b, c: (rows(b, c), 0)),
            pl.BlockSpec((q, inner), lambda b, c: (rows(b, c), 1)),
            pl.BlockSpec((q, 2 * n_bc), lambda b, c: (rows(b, c), 2 * inner // (2 * n_bc))),
            pl.BlockSpec((q, LANES), lambda b, c: (rows(b, c), (2 * inner + 2 * n_bc) // LANES)),
        ] + [full(a) for a in consts],
        out_specs=pl.BlockSpec((q, inner), lambda b, c: (rows(b, c), 0)),
        out_shape=jax.ShapeDtypeStruct((m, inner), BF16),
        scratch_shapes=[
            pltpu.VMEM((q + CONV_HIST, inner), F32),
            pltpu.VMEM((q + CONV_HIST, 2 * n_bc), F32),
            pltpu.VMEM((q, 2 * n_bc), F32),
            pltpu.VMEM((SSD_STATE, inner), F32),
            pltpu.VMEM((q, LANES), F32),
            pltpu.VMEM((LANES, q), F32),
            pltpu.VMEM((q, LANES), F32),
            pltpu.VMEM((q, inner // SSD_GROUPS), BF16),
            pltpu.VMEM((q, inner // SSD_GROUPS), F32),
        ],
        compiler_params=_cparams("parallel", "arbitrary"),
        name="ssd_core",
    )(zxd, zxd, zxd, zxd, *consts)


def _lru_kernel(gate_ref, xr_ref, cw_ref, cb_ref, wa_ref, ba_ref, wx_ref, bx_ref, ap_ref, o_ref,
                xpad_ref, a_ref, b_ref, h_ref, carry_ref):
    rows = gate_ref.shape[0]
    width = gate_ref.shape[1]

    @pl.when(pl.program_id(1) == 0)
    def _():
        xpad_ref[0:CONV_HIST, :] = jnp.zeros((CONV_HIST, width), F32)
        carry_ref[...] = jnp.zeros(carry_ref.shape, F32)

    xpad_ref[CONV_HIST:CONV_HIST + rows, :] = xr_ref[...]
    for k in range(width // LRU_BLOCK_DIM):
        lo = k * LRU_BLOCK_DIM
        sl = slice(lo, lo + LRU_BLOCK_DIM)
        xc = _causal_conv(xpad_ref, cw_ref, cb_ref, rows, lo, LRU_BLOCK_DIM)
        x16 = xc.astype(BF16)
        r_t = _sigmoid(_dot(x16, wa_ref[k]) + ba_ref[:, sl])
        i_t = _sigmoid(_dot(x16, wx_ref[k]) + bx_ref[:, sl])
        log_a = -LRU_C * r_t * _softplus(-ap_ref[:, sl])
        a_t = jnp.exp(log_a)
        a_ref[:, sl] = a_t
        b_ref[:, sl] = jnp.sqrt(1.0 - a_t * a_t) * (i_t * xc)
    xpad_ref[0:CONV_HIST, :] = xpad_ref[rows:rows + CONV_HIST, :]

    def step(t, h):
        h = a_ref[pl.ds(t, 1), :] * h + b_ref[pl.ds(t, 1), :]
        h_ref[pl.ds(t, 1), :] = h
        return h

    carry_ref[0:1, :] = lax.fori_loop(0, rows, step, carry_ref[0:1, :], unroll=8)
    o_ref[...] = (jax.nn.gelu(gate_ref[...], approximate=True) * h_ref[...]).astype(o_ref.dtype)


def lru_core(gx, batch, seq, conv_w, conv_b, wa, ba, wx, bx, a_param):
    m = gx.shape[0]
    width = gx.shape[1] // 2
    rows = LRU_ROWS
    nt = seq // rows
    row = lambda v: v.reshape(1, -1).astype(F32)
    consts = [conv_w.astype(F32), row(conv_b), wa.astype(BF16), row(ba), wx.astype(BF16), row(bx), row(a_param)]
    full = lambda a: pl.BlockSpec(a.shape, lambda b, t: (0,) * a.ndim)
    return pl.pallas_call(
        _lru_kernel,
        grid=(batch, nt),
        in_specs=[
            pl.BlockSpec((rows, width), lambda b, t: (b * nt + t, 0)),
            pl.BlockSpec((rows, width), lambda b, t: (b * nt + t, 1)),
        ] + [full(a) for a in consts],
        out_specs=pl.BlockSpec((rows, width), lambda b, t: (b * nt + t, 0)),
        out_shape=jax.ShapeDtypeStruct((m, width), BF16),
        scratch_shapes=[
            pltpu.VMEM((rows + CONV_HIST, width), F32),
            pltpu.VMEM((rows, width), F32),
            pltpu.VMEM((rows, width), F32),
            pltpu.VMEM((rows, width), F32),
            pltpu.VMEM((8, width), F32),
        ],
        compiler_params=_cparams("parallel", "arbitrary"),
        name="lru_core",
    )(gx, gx, *consts)


def _rope_tables(pos):
    half = ROT_DIM // 2
    inv = ROPE_THETA ** (-jnp.arange(half, dtype=F32) * 2.0 / ROT_DIM)
    ang = pos.astype(F32)[..., None] * inv
    cos, sin = jnp.cos(ang), jnp.sin(ang)
    shape = pos.shape + (NSA_HEAD_DIM - ROT_DIM,)
    zeros_h = jnp.zeros(pos.shape + (half,), F32)
    cos_t = jnp.concatenate([cos, cos, jnp.ones(shape, F32)], axis=-1)
    sin_lo = jnp.concatenate([-sin, zeros_h, jnp.zeros(shape, F32)], axis=-1)
    sin_hi = jnp.concatenate([zeros_h, sin, jnp.zeros(shape, F32)], axis=-1)
    return cos_t, sin_lo, sin_hi


def _apply_rope(x, cos_t, sin_lo, sin_hi):
    n = x.shape[1] // NSA_HEAD_DIM
    half = ROT_DIM // 2
    tile = lambda t: jnp.concatenate([t] * n, axis=1) if n > 1 else t
    up = pltpu.roll(x, x.shape[1] - half, axis=1)
    down = pltpu.roll(x, half, axis=1)
    return x * tile(cos_t) + up * tile(sin_lo) + down * tile(sin_hi)


def _rope_kernel(nq, x_ref, cos_ref, slo_ref, shi_ref, o_ref):
    factor = jnp.where(pl.program_id(1) < nq, NSA_HEAD_DIM ** -0.5 * LOG2_E, 1.0)
    o_ref[...] = (_apply_rope(x_ref[...], cos_ref[...], slo_ref[...], shi_ref[...]) * factor).astype(o_ref.dtype)


def nsa_rope(proj, tables, qdim, kvdim, tm=512):
    m = proj.shape[0]
    nq = qdim // kvdim
    src = lambda j: j + 2 * (j >= nq).astype(jnp.int32)
    tspec = pl.BlockSpec((tm, NSA_HEAD_DIM), lambda i, j: (i, 0))
    return pl.pallas_call(
        functools.partial(_rope_kernel, nq),
        grid=(m // tm, nq + 2),
        in_specs=[pl.BlockSpec((tm, kvdim), lambda i, j: (i, src(j))), tspec, tspec, tspec],
        out_specs=pl.BlockSpec((tm, kvdim), lambda i, j: (i, j)),
        out_shape=jax.ShapeDtypeStruct((m, qdim + 2 * kvdim), BF16),
        compiler_params=_cparams("parallel", "arbitrary"),
        name="nsa_rope",
    )(proj, *tables)


def _compress_kernel(is_key, r_ref, pe_ref, w1_ref, w2_ref, cos_ref, slo_ref, shi_ref, o_ref):
    half_k = w1_ref.shape[0] // 2
    nrows = r_ref.shape[0]
    r = r_ref[...]
    top = _dot((r + pe_ref[:, 0:half_k]).astype(BF16), w1_ref[0:half_k, :])
    bot = _dot((r + pe_ref[:, half_k:]).astype(BF16), w1_ref[half_k:, :])
    pre = top + pltpu.roll(bot, nrows - 1, axis=0)
    out = _dot(jax.nn.gelu(pre, approximate=True).astype(BF16), w2_ref[...])
    if is_key:
        o_ref[...] = _apply_rope(out, cos_ref[...], slo_ref[...], shi_ref[...]).astype(o_ref.dtype)
    else:
        o_ref[...] = out.T.astype(o_ref.dtype)


def nsa_compress(tok, pe, w1, w2, tables, is_key):
    b, g, nrows, kdim = tok.shape
    d = w2.shape[0]
    tspec = pl.BlockSpec((None, nrows, d), lambda i, j: (i, 0, 0))
    out_block = (None, None, nrows, d) if is_key else (None, None, d, nrows)
    out_dims = (b, g, nrows, d) if is_key else (b, g, d, nrows)
    return pl.pallas_call(
        functools.partial(_compress_kernel, is_key),
        grid=(b, g),
        in_specs=[
            pl.BlockSpec((None, None, nrows, kdim), lambda i, j: (i, j, 0, 0)),
            pl.BlockSpec((1, 2 * kdim), lambda i, j: (0, 0)),
            pl.BlockSpec((2 * kdim, d), lambda i, j: (0, 0)),
            pl.BlockSpec((d, d), lambda i, j: (0, 0)),
            tspec, tspec, tspec,
        ],
        out_specs=pl.BlockSpec(out_block, lambda i, j: (i, j, 0, 0)),
        out_shape=jax.ShapeDtypeStruct(out_dims, BF16),
        compiler_params=_cparams("parallel", "arbitrary"),
        name="nsa_compress_k" if is_key else "nsa_compress_v",
    )(tok, pe.reshape(1, 2 * kdim).astype(F32), w1.astype(BF16), w2.astype(BF16), *tables)


def _nsa_attn_kernel(n_sel, q_ref, kcmp_ref, vcmp_ref, ks_ref, vs_ref, kw_ref, vw_ref, gate_ref, c2s_ref, e_ref,
                     o_ref, impt_ref, m_ref, l_ref, acc_ref, out_ref, q4_ref, sa_ref, sb_ref):
    tq = Q_BLOCK
    d = NSA_HEAD_DIM
    hpg = q_ref.shape[1] // d
    n_cmp_pad = kcmp_ref.shape[0]
    n_sel_pad = c2s_ref.shape[0]
    t0 = pl.multiple_of(pl.program_id(2) * tq, tq)
    t_row = t0 + lax.broadcasted_iota(jnp.int32, (1, tq), 1)
    gates = _sigmoid(gate_ref[...]).T
    head = lambda r: slice(r * d, (r + 1) * d)
    cols = lambda r: slice(r * tq, (r + 1) * tq)
    per_head = lambda a: jnp.concatenate([a] * hpg, axis=1)
    for r in range(hpg):
        q4_ref[cols(r), 0:d] = q_ref[:, head(r)]
    q4 = q4_ref[:, 0:d]

    n_col = lax.broadcasted_iota(jnp.int32, (n_cmp_pad, 1), 0)
    ok_c = (n_col * CMP_STRIDE + (CMP_LEN - 1) <= t_row) & (n_col < n_cmp_pad - 1)
    bias_c = jnp.where(ok_c, 0.0, NEG_INF)
    keep_c = jnp.where(ok_c, 1.0, 0.0)
    s = _dot_nt(kcmp_ref[...], q4) + per_head(bias_c)
    e = jnp.exp2(s - jnp.max(s, axis=0, keepdims=True)) * per_head(keep_c)
    den = jnp.sum(e, axis=0, keepdims=True)
    p = e * jnp.where(den > 0.0, 1.0 / den, 0.0)
    o_c = _dot(vcmp_ref[...], p.astype(BF16))
    p_sum = p[:, cols(0)]
    for r in range(1, hpg):
        p_sum = p_sum + p[:, cols(r)]
    for r in range(hpg):
        out_ref[:, cols(r)] = gates[3 * r:3 * r + 1, :] * o_c[:, cols(r)]

    imp = _dot_exact_lhs(c2s_ref[...], p_sum)
    j_col = lax.broadcasted_iota(jnp.int32, (n_sel_pad, 1), 0)
    cur = t_row // SEL_LEN
    forced = (j_col == 0) | (j_col == cur) | (j_col == cur - 1)
    imp = jnp.where(forced, FORCED_SCORE, imp)
    impt_ref[...] = jnp.where(j_col <= cur, imp, -jnp.inf)
    n_blk = n_sel // 8
    blks = [impt_ref[jb * 8:(jb + 1) * 8, :] for jb in range(n_blk)]
    cnt = [jnp.zeros((8, tq), F32) for _ in range(n_blk)]
    sub = lax.broadcasted_iota(jnp.int32, (8, 1), 0)
    for i in range(n_sel):
        other = impt_ref[i:i + 1, :]
        for jb in range(n_blk):
            if jb * 8 > i:
                ahead = other >= blks[jb]
            elif jb * 8 + 7 < i:
                ahead = other > blks[jb]
            else:
                ahead = (other > blks[jb]) | ((other == blks[jb]) & (sub > i - jb * 8))
            cnt[jb] = cnt[jb] + jnp.where(ahead, 1.0, 0.0)
    sel_t = [jnp.where(c < float(SEL_TOPK), 1.0, 0.0) for c in cnt]
    if n_sel_pad > n_sel:
        sel_t.append(jnp.zeros((n_sel_pad - n_sel, tq), F32))
    sel = jnp.concatenate(sel_t, axis=0)
    sel_bias = jnp.where((sel > 0.5) & (j_col * SEL_LEN < t0), 0.0, NEG_INF).T.astype(BF16)
    for r in range(hpg):
        q4_ref[cols(r), d:2 * d] = sel_bias

    causal = (lax.broadcasted_iota(jnp.int32, (tq, 1), 0) <= lax.broadcasted_iota(jnp.int32, (1, tq), 1))
    s = _dot_nt(ks_ref[pl.ds(t0, tq), :], q4) + per_head(jnp.where(causal, 0.0, NEG_INF))
    m_0 = jnp.max(s, axis=0, keepdims=True)
    p = jnp.exp2(s - m_0)
    m_ref[...] = m_0
    l_ref[...] = jnp.sum(p, axis=0, keepdims=True)
    acc_ref[...] = _dot(vs_ref[:, pl.ds(t0, tq)], p.astype(BF16))

    kt_n = SEL_KEY_TILE

    last_tile = ks_ref.shape[0] // kt_n - 1

    def sel_scores(kt):
        k0 = pl.multiple_of(jnp.minimum(kt, last_tile) * kt_n, kt_n)
        lhs = jnp.concatenate([ks_ref[pl.ds(k0, kt_n), :], e_ref[pl.ds(k0, kt_n), :]], axis=1)
        return _dot_nt(lhs, q4_ref[...])

    def sel_update(kt, s):
        k0 = pl.multiple_of(kt * kt_n, kt_n)
        m_old = m_ref[...]
        m_new = jnp.maximum(m_old, jnp.max(s, axis=0, keepdims=True))
        alpha = jnp.exp2(m_old - m_new)
        p = jnp.exp2(s - m_new)
        l_ref[...] = alpha * l_ref[...] + jnp.sum(p, axis=0, keepdims=True)
        acc_ref[...] = alpha * acc_ref[...] + _dot(vs_ref[:, pl.ds(k0, kt_n)], p.astype(BF16))
        m_ref[...] = m_new

    sa_ref[...] = sel_scores(0)

    def sel_step(pair, carry):
        sb_ref[...] = sel_scores(2 * pair + 1)
        sel_update(2 * pair, sa_ref[...])
        sa_ref[...] = sel_scores(2 * pair + 2)
        sel_update(2 * pair + 1, sb_ref[...])
        return carry

    lax.fori_loop(0, (t0 + 2 * kt_n - 1) // (2 * kt_n), sel_step, 0)

    n_win = WINDOW + tq
    w0 = pl.multiple_of(jnp.maximum(t0 - WINDOW, 0), tq)
    kp = w0 + lax.broadcasted_iota(jnp.int32, (n_win, 1), 0)
    bias_w = jnp.where((kp <= t_row) & (kp > t_row - WINDOW), 0.0, NEG_INF)
    s = _dot_nt(kw_ref[pl.ds(w0, n_win), :], q4) + per_head(bias_w)
    e = jnp.exp2(s - jnp.max(s, axis=0, keepdims=True))
    p = e * (1.0 / jnp.sum(e, axis=0, keepdims=True))
    o_w = _dot(vw_ref[:, pl.ds(w0, n_win)], p.astype(BF16))
    o_s = acc_ref[...] * (1.0 / l_ref[...])
    for r in range(hpg):
        o_r = (out_ref[:, cols(r)] + gates[3 * r + 1:3 * r + 2, :] * o_s[:, cols(r)]
               + gates[3 * r + 2:3 * r + 3, :] * o_w[:, cols(r)])
        o_ref[:, head(r)] = o_r.T.astype(o_ref.dtype)


def nsa_attention(qk, v_sw_t, k_cmp, v_cmp_t, proj, c2s_t, batch, seq, qdim, kvdim, gate_col):
    m = qk.shape[0]
    d = NSA_HEAD_DIM
    groups = kvdim // d
    gq = qdim // groups
    hpg = gq // d
    nq = seq // Q_BLOCK
    n_sel = seq // SEL_LEN
    assert n_sel % 8 == 0 and seq % (2 * SEL_KEY_TILE) == 0 and seq >= WINDOW + Q_BLOCK
    n_cmp_pad = k_cmp.shape[2]
    n_sel_pad = c2s_t.shape[0]
    assert n_sel_pad == d
    block_onehot = jnp.asarray(np.arange(seq)[:, None] // SEL_LEN == np.arange(n_sel_pad)[None, :], BF16)
    k_blk = lambda off: pl.BlockSpec((seq, d), lambda b, g, t: (b, off + g))
    v_blk = lambda off: pl.BlockSpec((d, seq), lambda b, g, t: (off + g, b))
    return pl.pallas_call(
        functools.partial(_nsa_attn_kernel, n_sel),
        grid=(batch, groups, nq),
        in_specs=[
            pl.BlockSpec((Q_BLOCK, gq), lambda b, g, t: (b * nq + t, g)),
            pl.BlockSpec((None, None, n_cmp_pad, d), lambda b, g, t: (b, g, 0, 0)),
            pl.BlockSpec((None, None, d, n_cmp_pad), lambda b, g, t: (b, g, 0, 0)),
            k_blk(qdim // d), v_blk(0), k_blk(qdim // d + groups), v_blk(groups),
            pl.BlockSpec((Q_BLOCK, LANES), lambda b, g, t: (b * nq + t, gate_col // LANES + g)),
            pl.BlockSpec(c2s_t.shape, lambda b, g, t: (0, 0)),
            pl.BlockSpec(block_onehot.shape, lambda b, g, t: (0, 0)),
        ],
        out_specs=pl.BlockSpec((Q_BLOCK, gq), lambda b, g, t: (b * nq + t, g)),
        out_shape=jax.ShapeDtypeStruct((m, qdim), BF16),
        scratch_shapes=[
            pltpu.VMEM((n_sel_pad, Q_BLOCK), F32),
            pltpu.VMEM((1, hpg * Q_BLOCK), F32),
            pltpu.VMEM((1, hpg * Q_BLOCK), F32),
            pltpu.VMEM((d, hpg * Q_BLOCK), F32),
            pltpu.VMEM((d, hpg * Q_BLOCK), F32),
            pltpu.VMEM((hpg * Q_BLOCK, 2 * d), BF16),
            pltpu.VMEM((SEL_KEY_TILE, hpg * Q_BLOCK), F32),
            pltpu.VMEM((SEL_KEY_TILE, hpg * Q_BLOCK), F32),
        ],
        compiler_params=_cparams("parallel", "parallel", "arbitrary"),
        name="nsa_attention",
    )(qk, k_cmp, v_cmp_t, qk, v_sw_t, qk, v_sw_t, proj, c2s_t, block_onehot)


def _cmp_to_sel(seq):
    n_cmp = (seq - CMP_LEN) // CMP_STRIDE + 1
    n_sel = seq // SEL_LEN
    c_start = np.arange(n_cmp)[:, None] * CMP_STRIDE
    s_start = np.arange(n_sel)[None, :] * SEL_LEN
    overlap = np.clip(np.minimum(c_start + CMP_LEN, s_start + SEL_LEN) - np.maximum(c_start, s_start), 0, None)
    out = np.zeros((max(n_sel, LANES), seq // CMP_STRIDE), np.float32)
    out[:n_sel, :n_cmp] = (overlap / CMP_STRIDE).T
    return jnp.asarray(out, BF16)


def _pad_cols(w, n):
    return jnp.pad(w, ((0, 0), (0, n - w.shape[1])))


def ssd_mixer(h, nw, batch, seq, in_proj, conv_w, conv_b, dt_bias, a_log, d_skip, norm_w, out_proj):
    n_heads = dt_bias.shape[0]
    inner = n_heads * SSD_HEAD_DIM
    conv_dim = inner + 2 * SSD_GROUPS * SSD_STATE
    used = inner + conv_dim + n_heads
    n_pad = -(-used // LANES) * LANES
    w = _pad_cols(in_proj, n_pad).astype(BF16)
    zxd = norm_matmul(h, nw, w, F32, tn=n_pad // 9)
    y = ssd_core(zxd, batch, seq, conv_w, conv_b, dt_bias, a_log, d_skip, norm_w)
    return matmul_residual(y, out_proj.astype(BF16), h)


def rglru_mixer(h, nw, batch, seq, in_proj, conv_w, conv_b, wa, ba, wx, bx, a_param, out_proj):
    gx = norm_matmul(h, nw, in_proj.astype(BF16), F32, tn=1024)
    y = lru_core(gx, batch, seq, conv_w, conv_b, wa, ba, wx, bx, a_param)
    return matmul_residual(y, out_proj.astype(BF16), h)


def nsa_mixer(h, nw, positions, batch, seq, in_proj, cmp_pe, cmp_w1, cmp_w2, out_proj):
    d = NSA_HEAD_DIM
    groups = NSA_KV_GROUPS
    kvdim = groups * d
    qdim = out_proj.shape[0]
    n_heads = qdim // d
    hpg = n_heads // groups
    col = lambda k: qdim + k * kvdim
    gate_col = qdim + 4 * kvdim

    wg = in_proj[:, col(6):].reshape(-1, groups, 3 * hpg)
    wg = jnp.pad(wg, ((0, 0), (0, 0), (0, LANES - 3 * hpg))).reshape(-1, groups * LANES)
    w = jnp.concatenate([in_proj[:, :col(2)], in_proj[:, col(2):col(3)], in_proj[:, col(4):col(5)], wg], axis=1)
    proj = norm_matmul(h, nw, w.astype(BF16), F32, tn=1536)
    w_v_t = jnp.concatenate([in_proj[:, col(3):col(4)], in_proj[:, col(5):col(6)]], axis=1).T.astype(BF16)
    v_sw_t = norm_matmul_t(h, nw, w_v_t, BF16)

    tables = tuple(t.reshape(batch * seq, d) for t in _rope_tables(positions))
    qk = nsa_rope(proj, tables, qdim, kvdim)

    n_rows = seq // CMP_STRIDE
    cmp_end = jnp.minimum(jnp.arange(n_rows) * CMP_STRIDE + CMP_LEN - 1, seq - 1)
    cmp_tables = _rope_tables(positions[:, cmp_end])

    def blocks(col):
        tok = proj[:, col:col + kvdim].reshape(batch, n_rows, CMP_STRIDE, groups, d)
        return tok.transpose(0, 3, 1, 2, 4).reshape(batch, groups, n_rows, CMP_STRIDE * d)

    k_cmp = nsa_compress(blocks(qdim), cmp_pe[0], cmp_w1[0], cmp_w2[0], cmp_tables, True)
    v_cmp_t = nsa_compress(blocks(qdim + kvdim), cmp_pe[1], cmp_w1[1], cmp_w2[1], cmp_tables, False)

    o = nsa_attention(qk, v_sw_t, k_cmp, v_cmp_t, proj, _cmp_to_sel(seq), batch, seq, qdim, kvdim, gate_col)
    return matmul_residual(o, out_proj.astype(BF16), h)


def kernel(x, p, positions, norm_mix, norm_ffn, norm_ple, w_ple_up, w_ple_gate, w_ffn_in, w_ffn_out, norm_final, ssd_in_proj, ssd_conv_w, ssd_conv_b, ssd_dt_bias, ssd_a_log, ssd_d, ssd_norm, ssd_out_proj, lru_in_proj, lru_conv_w, lru_conv_b, lru_wa, lru_ba, lru_wx, lru_bx, lru_a_param, lru_out_proj, nsa_in_proj, nsa_cmp_pe, nsa_cmp_w1, nsa_cmp_w2, nsa_out_proj):
    batch, seq, d_model = x.shape
    depth = norm_mix.shape[0]
    n_mixers = 3
    m = batch * seq
    h = x.reshape(m, d_model)
    for i in range(depth):
        kind, j = i % n_mixers, i // n_mixers
        if kind == 0:
            h = ssd_mixer(h, norm_mix[i], batch, seq, ssd_in_proj[j], ssd_conv_w[j], ssd_conv_b[j], ssd_dt_bias[j],
                          ssd_a_log[j], ssd_d[j], ssd_norm[j], ssd_out_proj[j])
        elif kind == 1:
            h = rglru_mixer(h, norm_mix[i], batch, seq, lru_in_proj[j], lru_conv_w[j], lru_conv_b[j], lru_wa[j],
                            lru_ba[j], lru_wx[j], lru_bx[j], lru_a_param[j], lru_out_proj[j])
        else:
            h = nsa_mixer(h, norm_mix[i], positions, batch, seq, nsa_in_proj[j], nsa_cmp_pe[j], nsa_cmp_w1[j],
                          nsa_cmp_w2[j], nsa_out_proj[j])
        hidden = norm_swiglu_in(h, norm_ffn[i], w_ffn_in[i].astype(BF16))
        h = matmul_residual(hidden, w_ffn_out[i].astype(BF16), h)
        ple_args = (h, norm_ple[i], w_ple_gate[i].astype(BF16), p[i].reshape(m, -1), w_ple_up[i].astype(BF16))
        h = ple_layer(*ple_args) if i < depth - 1 else ple_final_layer(*ple_args, norm_final)
    return h.reshape(batch, seq, d_model)
```

```python
import functools

import numpy as np
import jax
import jax.numpy as jnp
from jax import lax
from jax.experimental import pallas as pl
from jax.experimental.pallas import tpu as pltpu

F32 = jnp.float32
BF16 = jnp.bfloat16

RMS_EPS = 1e-6
ROPE_THETA = 500000.0
NEG_INF = -1e30
FORCED_SCORE = 1e9
LOG2_E = 1.4426950408889634

VMEM_LIMIT_BYTES = 52 * 1024 * 1024
LANES = 128

SSD_HEAD_DIM = 64
SSD_GROUPS = 8
SSD_STATE = 128
SSD_CONV = 4
SSD_CHUNK = 128

LRU_BLOCK_DIM = 256
LRU_CONV = 4
LRU_C = 8.0
LRU_ROWS = 256

NSA_HEAD_DIM = 128
NSA_KV_GROUPS = 4
ROT_DIM = NSA_HEAD_DIM // 4
CMP_LEN = 32
CMP_STRIDE = 16
SEL_LEN = 64
SEL_TOPK = 16
WINDOW = 512
Q_BLOCK = 128
SEL_KEY_TILE = 512

CONV_HIST = 8


def _cparams(*sem):
    return pltpu.CompilerParams(dimension_semantics=sem, vmem_limit_bytes=VMEM_LIMIT_BYTES)


def _dot(a, b):
    return jnp.dot(a, b, preferred_element_type=F32)


def _dot_nt(a, b):
    return lax.dot_general(a, b, (((1,), (1,)), ((), ())), preferred_element_type=F32)


def _split3(x):
    hi = x.astype(BF16)
    r1 = x - hi.astype(F32)
    mid = r1.astype(BF16)
    lo = (r1 - mid.astype(F32)).astype(BF16)
    return hi, mid, lo


def _dot_exact_rhs(x, e):
    hi, mid, lo = _split3(x)
    return _dot(hi, e) + _dot(mid, e) + _dot(lo, e)


def _dot_exact_lhs(e, x):
    hi, mid, lo = _split3(x)
    return _dot(e, hi) + _dot(e, mid) + _dot(e, lo)


def _rms_scale(x):
    return lax.rsqrt(jnp.mean(x * x, axis=-1, keepdims=True) + RMS_EPS)


def _softplus(x):
    return jnp.maximum(x, 0.0) + jnp.log(1.0 + jnp.exp(-jnp.abs(x)))


def _sigmoid(x):
    return jax.nn.sigmoid(x)


def _norm_mm_kernel(x_ref, nw_ref, w_ref, o_ref, u_ref):
    @pl.when(pl.program_id(1) == 0)
    def _():
        x = x_ref[...]
        u_ref[...] = (x * _rms_scale(x) * nw_ref[...]).astype(BF16)

    o_ref[...] = _dot(u_ref[...], w_ref[...]).astype(o_ref.dtype)


def norm_matmul(x, nw, w, out_dtype, tm=1024, tn=512):
    m, k = x.shape
    n = w.shape[1]
    return pl.pallas_call(
        _norm_mm_kernel,
        grid=(m // tm, n // tn),
        in_specs=[
            pl.BlockSpec((tm, k), lambda i, j: (i, 0)),
            pl.BlockSpec((1, k), lambda i, j: (0, 0)),
            pl.BlockSpec((k, tn), lambda i, j: (0, j)),
        ],
        out_specs=pl.BlockSpec((tm, tn), lambda i, j: (i, j)),
        out_shape=jax.ShapeDtypeStruct((m, n), out_dtype),
        scratch_shapes=[pltpu.VMEM((tm, k), BF16)],
        compiler_params=_cparams("parallel", "arbitrary"),
        name="norm_matmul",
    )(x, nw.reshape(1, k), w)


def _norm_mm_t_kernel(x_ref, nw_ref, wt_ref, o_ref):
    x = x_ref[...]
    u = (x * _rms_scale(x) * nw_ref[...]).astype(BF16)
    o_ref[...] = _dot_nt(wt_ref[...], u).astype(o_ref.dtype)


def norm_matmul_t(x, nw, w_t, out_dtype, tm=1024):
    m, k = x.shape
    n = w_t.shape[0]
    return pl.pallas_call(
        _norm_mm_t_kernel,
        grid=(m // tm,),
        in_specs=[
            pl.BlockSpec((tm, k), lambda i: (i, 0)),
            pl.BlockSpec((1, k), lambda i: (0, 0)),
            pl.BlockSpec((n, k), lambda i: (0, 0)),
        ],
        out_specs=pl.BlockSpec((n, tm), lambda i: (0, i)),
        out_shape=jax.ShapeDtypeStruct((n, m), out_dtype),
        compiler_params=_cparams("parallel"),
        name="norm_matmul_t",
    )(x, nw.reshape(1, k), w_t)


def _norm_swiglu_kernel(x_ref, nw_ref, wg_ref, wu_ref, o_ref, u_ref):
    @pl.when(pl.program_id(1) == 0)
    def _():
        x = x_ref[...]
        u_ref[...] = (x * _rms_scale(x) * nw_ref[...]).astype(BF16)

    u = u_ref[...]
    gate = _dot(u, wg_ref[...])
    up = _dot(u, wu_ref[...])
    o_ref[...] = (gate * _sigmoid(gate) * up).astype(o_ref.dtype)


def norm_swiglu_in(x, nw, w_in, tm=1024, tn=512):
    m, k = x.shape
    f = w_in.shape[1] // 2
    nj = f // tn
    return pl.pallas_call(
        _norm_swiglu_kernel,
        grid=(m // tm, nj),
        in_specs=[
            pl.BlockSpec((tm, k), lambda i, j: (i, 0)),
            pl.BlockSpec((1, k), lambda i, j: (0, 0)),
            pl.BlockSpec((k, tn), lambda i, j: (0, j)),
            pl.BlockSpec((k, tn), lambda i, j: (0, j + nj)),
        ],
        out_specs=pl.BlockSpec((tm, tn), lambda i, j: (i, j)),
        out_shape=jax.ShapeDtypeStruct((m, f), BF16),
        scratch_shapes=[pltpu.VMEM((tm, k), BF16)],
        compiler_params=_cparams("parallel", "arbitrary"),
        name="norm_swiglu_in",
    )(x, nw.reshape(1, k), w_in, w_in)


def _mm_residual_kernel(x_ref, w_ref, r_ref, o_ref):
    o_ref[...] = r_ref[...] + _dot(x_ref[...], w_ref[...])


def matmul_residual(x, w, res, tm=1024, tn=512):
    m, k = x.shape
    n = w.shape[1]
    return pl.pallas_call(
        _mm_residual_kernel,
        grid=(m // tm, n // tn),
        in_specs=[
            pl.BlockSpec((tm, k), lambda i, j: (i, 0)),
            pl.BlockSpec((k, tn), lambda i, j: (0, j)),
            pl.BlockSpec((tm, tn), lambda i, j: (i, j)),
        ],
        out_specs=pl.BlockSpec((tm, tn), lambda i, j: (i, j)),
        out_shape=jax.ShapeDtypeStruct((m, n), F32),
        compiler_params=_cparams("parallel", "arbitrary"),
        name="matmul_residual",
    )(x, w, res)


def _ple_kernel(tn, x_ref, nw_ref, wg_ref, p_ref, wu_ref, o_ref, u_ref):
    j = pl.program_id(1)

    @pl.when(j == 0)
    def _():
        x = x_ref[...]
        u_ref[...] = (x * _rms_scale(x) * nw_ref[...]).astype(BF16)

    gate = _sigmoid(_dot(u_ref[...], wg_ref[...]))
    up = _dot(p_ref[...].astype(BF16), wu_ref[...])
    col = pl.multiple_of(j * tn, tn)
    o_ref[...] = x_ref[:, pl.ds(col, tn)] + gate * up


def ple_layer(x, nw, w_gate, p, w_up, tm=1024, tn=1024):
    m, k = x.shape
    n = w_gate.shape[1]
    kp = p.shape[1]
    return pl.pallas_call(
        functools.partial(_ple_kernel, tn),
        grid=(m // tm, n // tn),
        in_specs=[
            pl.BlockSpec((tm, k), lambda i, j: (i, 0)),
            pl.BlockSpec((1, k), lambda i, j: (0, 0)),
            pl.BlockSpec((k, tn), lambda i, j: (0, j)),
            pl.BlockSpec((tm, kp), lambda i, j: (i, 0)),
            pl.BlockSpec((kp, tn), lambda i, j: (0, j)),
        ],
        out_specs=pl.BlockSpec((tm, tn), lambda i, j: (i, j)),
        out_shape=jax.ShapeDtypeStruct((m, n), F32),
        scratch_shapes=[pltpu.VMEM((tm, k), BF16)],
        compiler_params=_cparams("parallel", "arbitrary"),
        name="ple_layer",
    )(x, nw.reshape(1, k), w_gate, p, w_up)


def _ple_final_kernel(x_ref, nw_ref, wg_ref, p_ref, wu_ref, fw_ref, o_ref):
    x = x_ref[...]
    u = (x * _rms_scale(x) * nw_ref[...]).astype(BF16)
    y = x + _sigmoid(_dot(u, wg_ref[...])) * _dot(p_ref[...].astype(BF16), wu_ref[...])
    o_ref[...] = y * _rms_scale(y) * fw_ref[...]


def ple_final_layer(x, nw, w_gate, p, w_up, final_nw, tm=512):
    m, k = x.shape
    n = w_gate.shape[1]
    kp = p.shape[1]
    const = lambda shape: pl.BlockSpec(shape, lambda i: (0, 0))
    return pl.pallas_call(
        _ple_final_kernel,
        grid=(m // tm,),
        in_specs=[pl.BlockSpec((tm, k), lambda i: (i, 0)), const((1, k)), const((k, n)),
                  pl.BlockSpec((tm, kp), lambda i: (i, 0)), const((kp, n)), const((1, n))],
        out_specs=pl.BlockSpec((tm, n), lambda i: (i, 0)),
        out_shape=jax.ShapeDtypeStruct((m, n), F32),
        compiler_params=_cparams("parallel"),
        name="ple_final_layer",
    )(x, nw.reshape(1, k), w_gate, p, w_up, final_nw.reshape(1, n))


def _causal_conv(pad_ref, w_ref, b_ref, rows, lo, width):
    taps = w_ref.shape[0]
    x = pad_ref[0:CONV_HIST + rows, lo:lo + width]
    acc = w_ref[0:1, lo:lo + width] * x
    for k in range(1, taps):
        acc = pltpu.roll(acc, 1, axis=0) + w_ref[k:k + 1, lo:lo + width] * x
    return acc[CONV_HIST:CONV_HIST + rows, :] + b_ref[:, lo:lo + width]


def _ssd_kernel(z_ref, xs_ref, bc_ref, dt_ref, cwx_ref, cbx_ref, cwbc_ref, cbbc_ref, dtb_ref, alog_ref,
                dskip_ref, nw_ref, tril_ref, ehead_ref, o_ref,
                xpad_ref, bcpad_ref, bcact_ref, state_ref, acs_ref, acst_ref, dt_s_ref, xdt_ref, ydiag_ref):
    q = SSD_CHUNK
    gw = xs_ref.shape[1] // SSD_GROUPS
    heads_per_group = gw // SSD_HEAD_DIM
    n_bc = bc_ref.shape[1] // 2

    @pl.when(pl.program_id(1) == 0)
    def _():
        xpad_ref[0:CONV_HIST, :] = jnp.zeros((CONV_HIST, xpad_ref.shape[1]), F32)
        bcpad_ref[0:CONV_HIST, :] = jnp.zeros((CONV_HIST, bcpad_ref.shape[1]), F32)
        state_ref[...] = jnp.zeros(state_ref.shape, F32)

    xpad_ref[CONV_HIST:CONV_HIST + q, :] = xs_ref[...]
    bcpad_ref[CONV_HIST:CONV_HIST + q, :] = bc_ref[...]

    for lo in range(0, bc_ref.shape[1], 512):
        v = _causal_conv(bcpad_ref, cwbc_ref, cbbc_ref, q, lo, 512)
        bcact_ref[:, lo:lo + 512] = v * _sigmoid(v)

    dt = _softplus(dt_ref[...] + dtb_ref[...])
    adt = dt * (-jnp.exp(alog_ref[...]))
    acs = _dot_exact_lhs(tril_ref[...], adt)
    acs_ref[...] = acs
    acst_ref[...] = acs.T
    dt_s_ref[...] = dt

    li = lax.broadcasted_iota(jnp.int32, (q, q), 0)
    si = lax.broadcasted_iota(jnp.int32, (q, q), 1)
    causal = li >= si

    for g in range(SSD_GROUPS):
        glo = g * gw
        e_g = ehead_ref[:, glo:glo + gw]
        xc = _causal_conv(xpad_ref, cwx_ref, cbx_ref, q, glo, gw)
        xs = xc * _sigmoid(xc)
        dt_g = _dot_exact_rhs(dt_s_ref[...], e_g)
        a_g = _dot_exact_rhs(acs_ref[...], e_g)
        a_last = a_g[q - 1:q, :]
        xdt = xs * dt_g
        xdt_ref[...] = xdt.astype(BF16)

        bm = bcact_ref[:, g * SSD_STATE:(g + 1) * SSD_STATE]
        cm = bcact_ref[:, n_bc + g * SSD_STATE:n_bc + (g + 1) * SSD_STATE]
        bm16 = bm.astype(BF16)
        cm16 = cm.astype(BF16)
        cb = _dot_nt(cm16, bm16)

        prev = state_ref[:, glo:glo + gw]
        y_off = _dot(cm16, prev.astype(BF16)) * jnp.exp(a_g)
        st_new = _dot(bm.T.astype(BF16), (xdt * jnp.exp(a_last - a_g)).astype(BF16))
        state_ref[:, glo:glo + gw] = prev * jnp.exp(a_last) + st_new

        for r in range(heads_per_group):
            h = g * heads_per_group + r
            seg = acs_ref[:, h:h + 1] - acst_ref[h:h + 1, :]
            decay = jnp.exp(jnp.where(causal, seg, NEG_INF))
            m_h = (cb * decay).astype(BF16)
            ydiag_ref[:, r * SSD_HEAD_DIM:(r + 1) * SSD_HEAD_DIM] = _dot(
                m_h, xdt_ref[:, r * SSD_HEAD_DIM:(r + 1) * SSD_HEAD_DIM])

        y = ydiag_ref[...] + y_off + dskip_ref[:, glo:glo + gw] * xs
        zg = z_ref[:, glo:glo + gw]
        y = y * (zg * _sigmoid(zg))
        y = y * _rms_scale(y) * nw_ref[:, glo:glo + gw]
        o_ref[:, glo:glo + gw] = y.astype(o_ref.dtype)

    xpad_ref[0:CONV_HIST, :] = xpad_ref[q:q + CONV_HIST, :]
    bcpad_ref[0:CONV_HIST, :] = bcpad_ref[q:q + CONV_HIST, :]


def ssd_core(zxd, batch, seq, conv_w, conv_b, dt_bias, a_log, d_skip, norm_w):
    m = zxd.shape[0]
    n_heads = dt_bias.shape[0]
    inner = n_heads * SSD_HEAD_DIM
    n_bc = SSD_GROUPS * SSD_STATE
    q = SSD_CHUNK
    nc = seq // q
    assert inner % (2 * n_bc) == 0 and n_heads <= LANES
    pad_h = LANES - n_heads

    row = lambda v: v.reshape(1, -1).astype(F32)
    padh = lambda v: jnp.pad(v.astype(F32), (0, pad_h)).reshape(1, LANES)
    tril = jnp.asarray(np.tril(np.ones((q, q), np.float32)), BF16)
    ehead = jnp.asarray(np.repeat(np.eye(LANES, n_heads, dtype=np.float32), SSD_HEAD_DIM, axis=1), BF16)

    rows = lambda b, c: b * nc + c
    full = lambda a: pl.BlockSpec(a.shape, lambda b, c: (0,) * a.ndim)
    consts = [conv_w[:, :inner].astype(F32), row(conv_b[:inner]), conv_w[:, inner:].astype(F32), row(conv_b[inner:]),
              padh(dt_bias), padh(a_log), row(jnp.repeat(d_skip, SSD_HEAD_DIM)), row(norm_w), tril, ehead]
    return pl.pallas_call(
        _ssd_kernel,
        grid=(batch, nc),
        in_specs=[
            pl.BlockSpec((q, inner), lambda b, c: (rows(b, c), 0)),
            pl.BlockSpec((q, inner), lambda b, c: (rows(b, c), 1)),
            pl.BlockSpec((q, 2 * n_bc), lambda b, c: (rows(b, c), 2 * inner // (2 * n_bc))),
            pl.BlockSpec((q, LANES), lambda b, c: (rows(b, c), (2 * inner + 2 * n_bc) // LANES)),
        ] + [full(a) for a in consts],
        out_specs=pl.BlockSpec((q, inner), lambda b, c: (rows(b, c), 0)),
        out_shape=jax.ShapeDtypeStruct((m, inner), BF16),
        scratch_shapes=[
            pltpu.VMEM((q + CONV_HIST, inner), F32),
            pltpu.VMEM((q + CONV_HIST, 2 * n_bc), F32),
            pltpu.VMEM((q, 2 * n_bc), F32),
            pltpu.VMEM((SSD_STATE, inner), F32),
            pltpu.VMEM((q, LANES), F32),
            pltpu.VMEM((LANES, q), F32),
            pltpu.VMEM((q, LANES), F32),
            pltpu.VMEM((q, inner // SSD_GROUPS), BF16),
            pltpu.VMEM((q, inner // SSD_GROUPS), F32),
        ],
        compiler_params=_cparams("parallel", "arbitrary"),
        name="ssd_core",
    )(zxd, zxd, zxd, zxd, *consts)


def _lru_kernel(gate_ref, xr_ref, cw_ref, cb_ref, wa_ref, ba_ref, wx_ref, bx_ref, ap_ref, o_ref,
                xpad_ref, a_ref, b_ref, h_ref, carry_ref):
    rows = gate_ref.shape[0]
    width = gate_ref.shape[1]

    @pl.when(pl.program_id(1) == 0)
    def _():
        xpad_ref[0:CONV_HIST, :] = jnp.zeros((CONV_HIST, width), F32)
        carry_ref[...] = jnp.zeros(carry_ref.shape, F32)

    xpad_ref[CONV_HIST:CONV_HIST + rows, :] = xr_ref[...]
    for k in range(width // LRU_BLOCK_DIM):
        lo = k * LRU_BLOCK_DIM
        sl = slice(lo, lo + LRU_BLOCK_DIM)
        xc = _causal_conv(xpad_ref, cw_ref, cb_ref, rows, lo, LRU_BLOCK_DIM)
        x16 = xc.astype(BF16)
        r_t = _sigmoid(_dot(x16, wa_ref[k]) + ba_ref[:, sl])
        i_t = _sigmoid(_dot(x16, wx_ref[k]) + bx_ref[:, sl])
        log_a = -LRU_C * r_t * _softplus(-ap_ref[:, sl])
        a_t = jnp.exp(log_a)
        a_ref[:, sl] = a_t
        b_ref[:, sl] = jnp.sqrt(1.0 - a_t * a_t) * (i_t * xc)
    xpad_ref[0:CONV_HIST, :] = xpad_ref[rows:rows + CONV_HIST, :]

    def step(t, h):
        h = a_ref[pl.ds(t, 1), :] * h + b_ref[pl.ds(t, 1), :]
        h_ref[pl.ds(t, 1), :] = h
        return h

    carry_ref[0:1, :] = lax.fori_loop(0, rows, step, carry_ref[0:1, :], unroll=8)
    o_ref[...] = (jax.nn.gelu(gate_ref[...], approximate=True) * h_ref[...]).astype(o_ref.dtype)


def lru_core(gx, batch, seq, conv_w, conv_b, wa, ba, wx, bx, a_param):
    m = gx.shape[0]
    width = gx.shape[1] // 2
    rows = LRU_ROWS
    nt = seq // rows
    row = lambda v: v.reshape(1, -1).astype(F32)
    consts = [conv_w.astype(F32), row(conv_b), wa.astype(BF16), row(ba), wx.astype(BF16), row(bx), row(a_param)]
    full = lambda a: pl.BlockSpec(a.shape, lambda b, t: (0,) * a.ndim)
    return pl.pallas_call(
        _lru_kernel,
        grid=(batch, nt),
        in_specs=[
            pl.BlockSpec((rows, width), lambda b, t: (b * nt + t, 0)),
            pl.BlockSpec((rows, width), lambda b, t: (b * nt + t, 1)),
        ] + [full(a) for a in consts],
        out_specs=pl.BlockSpec((rows, width), lambda b, t: (b * nt + t, 0)),
        out_shape=jax.ShapeDtypeStruct((m, width), BF16),
        scratch_shapes=[
            pltpu.VMEM((rows + CONV_HIST, width), F32),
            pltpu.VMEM((rows, width), F32),
            pltpu.VMEM((rows, width), F32),
            pltpu.VMEM((rows, width), F32),
            pltpu.VMEM((8, width), F32),
        ],
        compiler_params=_cparams("parallel", "arbitrary"),
        name="lru_core",
    )(gx, gx, *consts)


def _rope_tables(pos):
    half = ROT_DIM // 2
    inv = ROPE_THETA ** (-jnp.arange(half, dtype=F32) * 2.0 / ROT_DIM)
    ang = pos.astype(F32)[..., None] * inv
    cos, sin = jnp.cos(ang), jnp.sin(ang)
    shape = pos.shape + (NSA_HEAD_DIM - ROT_DIM,)
    zeros_h = jnp.zeros(pos.shape + (half,), F32)
    cos_t = jnp.concatenate([cos, cos, jnp.ones(shape, F32)], axis=-1)
    sin_lo = jnp.concatenate([-sin, zeros_h, jnp.zeros(shape, F32)], axis=-1)
    sin_hi = jnp.concatenate([zeros_h, sin, jnp.zeros(shape, F32)], axis=-1)
    return cos_t, sin_lo, sin_hi


def _apply_rope(x, cos_t, sin_lo, sin_hi):
    n = x.shape[1] // NSA_HEAD_DIM
    half = ROT_DIM // 2
    tile = lambda t: jnp.concatenate([t] * n, axis=1) if n > 1 else t
    up = pltpu.roll(x, x.shape[1] - half, axis=1)
    down = pltpu.roll(x, half, axis=1)
    return x * tile(cos_t) + up * tile(sin_lo) + down * tile(sin_hi)


def _norm_mm_rope_kernel(x_ref, nw_ref, w_ref, cos_ref, slo_ref, shi_ref, cs_ref, o_ref, u_ref):
    @pl.when(pl.program_id(1) == 0)
    def _():
        x = x_ref[...]
        u_ref[...] = (x * _rms_scale(x) * nw_ref[...]).astype(BF16)

    y = _apply_rope(_dot(u_ref[...], w_ref[...]), cos_ref[...], slo_ref[...], shi_ref[...])
    o_ref[...] = (y * cs_ref[...]).astype(o_ref.dtype)


def norm_matmul_rope(x, nw, w, tables, col_scale, tm=1024, tn=768):
    m, k = x.shape
    n = w.shape[1]
    tspec = pl.BlockSpec((tm, NSA_HEAD_DIM), lambda i, j: (i, 0))
    return pl.pallas_call(
        _norm_mm_rope_kernel,
        grid=(m // tm, n // tn),
        in_specs=[
            pl.BlockSpec((tm, k), lambda i, j: (i, 0)),
            pl.BlockSpec((1, k), lambda i, j: (0, 0)),
            pl.BlockSpec((k, tn), lambda i, j: (0, j)),
            tspec, tspec, tspec,
            pl.BlockSpec((1, tn), lambda i, j: (0, j)),
        ],
        out_specs=pl.BlockSpec((tm, tn), lambda i, j: (i, j)),
        out_shape=jax.ShapeDtypeStruct((m, n), BF16),
        scratch_shapes=[pltpu.VMEM((tm, k), BF16)],
        compiler_params=_cparams("parallel", "arbitrary"),
        name="norm_matmul_rope",
    )(x, nw.reshape(1, k), w, *tables, col_scale.reshape(1, n))


def _compress_kernel(is_key, tok_ref, pe_ref, w1_ref, w2_ref, cos_ref, slo_ref, shi_ref, o_ref):
    d = w2_ref.shape[0]
    nrows = tok_ref.shape[0] // CMP_STRIDE
    top = jnp.zeros((nrows, d), F32)
    bot = jnp.zeros((nrows, d), F32)
    for pos in range(CMP_STRIDE):
        tok = tok_ref[pl.ds(pos, nrows, stride=CMP_STRIDE), :]
        lo, hi = pos * d, (CMP_STRIDE + pos) * d
        top = top + _dot((tok + pe_ref[pos:pos + 1, :]).astype(BF16), w1_ref[lo:lo + d, :])
        bot = bot + _dot((tok + pe_ref[CMP_STRIDE + pos:CMP_STRIDE + pos + 1, :]).astype(BF16), w1_ref[hi:hi + d, :])
    pre = top + pltpu.roll(bot, nrows - 1, axis=0)
    out = _dot(jax.nn.gelu(pre, approximate=True).astype(BF16), w2_ref[...])
    if is_key:
        o_ref[...] = _apply_rope(out, cos_ref[...], slo_ref[...], shi_ref[...]).astype(o_ref.dtype)
    else:
        o_ref[...] = out.T.astype(o_ref.dtype)


def nsa_compress(proj, col, batch, seq, groups, pe, w1, w2, tables, is_key):
    d = w2.shape[0]
    nrows = seq // CMP_STRIDE
    assert CMP_LEN == 2 * CMP_STRIDE and col % d == 0
    tspec = pl.BlockSpec((None, nrows, d), lambda i, j: (i, 0, 0))
    const = lambda a: pl.BlockSpec(a.shape, lambda i, j: (0,) * a.ndim)
    out_block = (None, None, nrows, d) if is_key else (None, None, d, nrows)
    out_dims = (batch, groups, nrows, d) if is_key else (batch, groups, d, nrows)
    consts = [pe.astype(F32), w1.astype(BF16), w2.astype(BF16)]
    return pl.pallas_call(
        functools.partial(_compress_kernel, is_key),
        grid=(batch, groups),
        in_specs=[pl.BlockSpec((seq, d), lambda i, j: (i, col // d + j))] + [const(a) for a in consts]
        + [tspec, tspec, tspec],
        out_specs=pl.BlockSpec(out_block, lambda i, j: (i, j, 0, 0)),
        out_shape=jax.ShapeDtypeStruct(out_dims, BF16),
        compiler_params=_cparams("parallel", "arbitrary"),
        name="nsa_compress_k" if is_key else "nsa_compress_v",
    )(proj, *consts, *tables)


def _nsa_attn_kernel(n_sel, q_ref, kcmp_ref, vcmp_ref, ks_ref, vs_ref, kw_ref, vw_ref, gate_ref, c2s_ref, e_ref,
                     o_ref, impt_ref, m_ref, l_ref, acc_ref, out_ref, q4_ref, sa_ref, sb_ref):
    tq = Q_BLOCK
    d = NSA_HEAD_DIM
    hpg = q_ref.shape[1] // d
    n_cmp_pad = kcmp_ref.shape[0]
    n_sel_pad = c2s_ref.shape[0]
    t0 = pl.multiple_of(pl.program_id(2) * tq, tq)
    t_row = t0 + lax.broadcasted_iota(jnp.int32, (1, tq), 1)
    gates = _sigmoid(gate_ref[...]).T
    head = lambda r: slice(r * d, (r + 1) * d)
    cols = lambda r: slice(r * tq, (r + 1) * tq)
    per_head = lambda a: jnp.concatenate([a] * hpg, axis=1)
    for r in range(hpg):
        q4_ref[cols(r), 0:d] = q_ref[:, head(r)]
    q4 = q4_ref[:, 0:d]

    n_col = lax.broadcasted_iota(jnp.int32, (n_cmp_pad, 1), 0)
    ok_c = (n_col * CMP_STRIDE + (CMP_LEN - 1) <= t_row) & (n_col < n_cmp_pad - 1)
    bias_c = jnp.where(ok_c, 0.0, NEG_INF)
    keep_c = jnp.where(ok_c, 1.0, 0.0)
    s = _dot_nt(kcmp_ref[...], q4) + per_head(bias_c)

    n_win = WINDOW + tq
    w0 = pl.multiple_of(jnp.maximum(t0 - WINDOW, 0), tq)
    kp = w0 + lax.broadcasted_iota(jnp.int32, (n_win, 1), 0)
    bias_w = jnp.where((kp <= t_row) & (kp > t_row - WINDOW), 0.0, NEG_INF)
    s_w = _dot_nt(kw_ref[pl.ds(w0, n_win), :], q4) + per_head(bias_w)
    causal = (lax.broadcasted_iota(jnp.int32, (tq, 1), 0) <= lax.broadcasted_iota(jnp.int32, (1, tq), 1))
    s_d = _dot_nt(ks_ref[pl.ds(t0, tq), :], q4) + per_head(jnp.where(causal, 0.0, NEG_INF))

    e = jnp.exp2(s - jnp.max(s, axis=0, keepdims=True)) * per_head(keep_c)
    den = jnp.sum(e, axis=0, keepdims=True)
    p = e * jnp.where(den > 0.0, 1.0 / den, 0.0)
    o_c = _dot(vcmp_ref[...], p.astype(BF16))
    p_sum = p[:, cols(0)]
    for r in range(1, hpg):
        p_sum = p_sum + p[:, cols(r)]
    for r in range(hpg):
        out_ref[:, cols(r)] = gates[3 * r:3 * r + 1, :] * o_c[:, cols(r)]

    imp = _dot_exact_lhs(c2s_ref[...], p_sum)
    j_col = lax.broadcasted_iota(jnp.int32, (n_sel_pad, 1), 0)
    cur = t_row // SEL_LEN
    forced = (j_col == 0) | (j_col == cur) | (j_col == cur - 1)
    imp = jnp.where(forced, FORCED_SCORE, imp)
    impt_ref[...] = jnp.where(j_col <= cur, imp, -jnp.inf)
    n_blk = n_sel // 8
    blks = [impt_ref[jb * 8:(jb + 1) * 8, :] for jb in range(n_blk)]
    cnt = [jnp.zeros((8, tq), F32) for _ in range(n_blk)]
    sub = lax.broadcasted_iota(jnp.int32, (8, 1), 0)
    for i in range(n_sel):
        other = impt_ref[i:i + 1, :]
        for jb in range(n_blk):
            if jb * 8 > i:
                ahead = other >= blks[jb]
            elif jb * 8 + 7 < i:
                ahead = other > blks[jb]
            else:
                ahead = (other > blks[jb]) | ((other == blks[jb]) & (sub > i - jb * 8))
            cnt[jb] = cnt[jb] + jnp.where(ahead, 1.0, 0.0)
    sel_t = [jnp.where(c < float(SEL_TOPK), 1.0, 0.0) for c in cnt]
    if n_sel_pad > n_sel:
        sel_t.append(jnp.zeros((n_sel_pad - n_sel, tq), F32))
    sel = jnp.concatenate(sel_t, axis=0)
    sel_bias = jnp.where((sel > 0.5) & (j_col * SEL_LEN < t0), 0.0, NEG_INF).T.astype(BF16)
    for r in range(hpg):
        q4_ref[cols(r), d:2 * d] = sel_bias

    m_0 = jnp.max(s_d, axis=0, keepdims=True)
    p = jnp.exp2(s_d - m_0)
    m_ref[...] = m_0
    l_ref[...] = jnp.sum(p, axis=0, keepdims=True)
    acc_ref[...] = _dot(vs_ref[:, pl.ds(t0, tq)], p.astype(BF16))

    kt_n = SEL_KEY_TILE

    last_tile = ks_ref.shape[0] // kt_n - 1

    def sel_scores(kt):
        k0 = pl.multiple_of(jnp.minimum(kt, last_tile) * kt_n, kt_n)
        lhs = jnp.concatenate([ks_ref[pl.ds(k0, kt_n), :], e_ref[pl.ds(k0, kt_n), :]], axis=1)
        return _dot_nt(lhs, q4_ref[...])

    def sel_update(kt, s):
        k0 = pl.multiple_of(kt * kt_n, kt_n)
        m_old = m_ref[...]
        m_new = jnp.maximum(m_old, jnp.max(s, axis=0, keepdims=True))
        alpha = jnp.exp2(m_old - m_new)
        p = jnp.exp2(s - m_new)
        l_ref[...] = alpha * l_ref[...] + jnp.sum(p, axis=0, keepdims=True)
        acc_ref[...] = alpha * acc_ref[...] + _dot(vs_ref[:, pl.ds(k0, kt_n)], p.astype(BF16))
        m_ref[...] = m_new

    sa_ref[...] = sel_scores(0)

    e = jnp.exp2(s_w - jnp.max(s_w, axis=0, keepdims=True))
    p = e * (1.0 / jnp.sum(e, axis=0, keepdims=True))
    o_w = _dot(vw_ref[:, pl.ds(w0, n_win)], p.astype(BF16))
    for r in range(hpg):
        out_ref[:, cols(r)] = out_ref[:, cols(r)] + gates[3 * r + 2:3 * r + 3, :] * o_w[:, cols(r)]

    def sel_step(pair, carry):
        sb_ref[...] = sel_scores(2 * pair + 1)
        sel_update(2 * pair, sa_ref[...])
        sa_ref[...] = sel_scores(2 * pair + 2)
        sel_update(2 * pair + 1, sb_ref[...])
        return carry

    lax.fori_loop(0, (t0 + 2 * kt_n - 1) // (2 * kt_n), sel_step, 0)

    o_s = acc_ref[...] * (1.0 / l_ref[...])
    for r in range(hpg):
        o_r = out_ref[:, cols(r)] + gates[3 * r + 1:3 * r + 2, :] * o_s[:, cols(r)]
        o_ref[:, head(r)] = o_r.T.astype(o_ref.dtype)


def nsa_attention(qk, v_sw_t, k_cmp, v_cmp_t, proj, c2s_t, batch, seq, qdim, kvdim, gate_col):
    m = qk.shape[0]
    d = NSA_HEAD_DIM
    groups = kvdim // d
    gq = qdim // groups
    hpg = gq // d
    nq = seq // Q_BLOCK
    n_sel = seq // SEL_LEN
    assert n_sel % 8 == 0 and seq % (2 * SEL_KEY_TILE) == 0 and seq >= WINDOW + Q_BLOCK
    n_cmp_pad = k_cmp.shape[2]
    n_sel_pad = c2s_t.shape[0]
    assert n_sel_pad == d
    block_onehot = jnp.asarray(np.arange(seq)[:, None] // SEL_LEN == np.arange(n_sel_pad)[None, :], BF16)
    k_blk = lambda off: pl.BlockSpec((seq, d), lambda b, g, t: (b, off + g))
    v_blk = lambda off: pl.BlockSpec((d, seq), lambda b, g, t: (off + g, b))
    return pl.pallas_call(
        functools.partial(_nsa_attn_kernel, n_sel),
        grid=(batch, groups, nq),
        in_specs=[
            pl.BlockSpec((Q_BLOCK, gq), lambda b, g, t: (b * nq + t, g)),
            pl.BlockSpec((None, None, n_cmp_pad, d), lambda b, g, t: (b, g, 0, 0)),
            pl.BlockSpec((None, None, d, n_cmp_pad), lambda b, g, t: (b, g, 0, 0)),
            k_blk(qdim // d), v_blk(0), k_blk(qdim // d + groups), v_blk(groups),
            pl.BlockSpec((Q_BLOCK, LANES), lambda b, g, t: (b * nq + t, gate_col // LANES + g)),
            pl.BlockSpec(c2s_t.shape, lambda b, g, t: (0, 0)),
            pl.BlockSpec(block_onehot.shape, lambda b, g, t: (0, 0)),
        ],
        out_specs=pl.BlockSpec((Q_BLOCK, gq), lambda b, g, t: (b * nq + t, g)),
        out_shape=jax.ShapeDtypeStruct((m, qdim), BF16),
        scratch_shapes=[
            pltpu.VMEM((n_sel_pad, Q_BLOCK), F32),
            pltpu.VMEM((1, hpg * Q_BLOCK), F32),
            pltpu.VMEM((1, hpg * Q_BLOCK), F32),
            pltpu.VMEM((d, hpg * Q_BLOCK), F32),
            pltpu.VMEM((d, hpg * Q_BLOCK), F32),
            pltpu.VMEM((hpg * Q_BLOCK, 2 * d), BF16),
            pltpu.VMEM((SEL_KEY_TILE, hpg * Q_BLOCK), F32),
            pltpu.VMEM((SEL_KEY_TILE, hpg * Q_BLOCK), F32),
        ],
        compiler_params=_cparams("parallel", "parallel", "arbitrary"),
        name="nsa_attention",
    )(qk, k_cmp, v_cmp_t, qk, v_sw_t, qk, v_sw_t, proj, c2s_t, block_onehot)


def _cmp_to_sel(seq):
    n_cmp = (seq - CMP_LEN) // CMP_STRIDE + 1
    n_sel = seq // SEL_LEN
    c_start = np.arange(n_cmp)[:, None] * CMP_STRIDE
    s_start = np.arange(n_sel)[None, :] * SEL_LEN
    overlap = np.clip(np.minimum(c_start + CMP_LEN, s_start + SEL_LEN) - np.maximum(c_start, s_start), 0, None)
    out = np.zeros((max(n_sel, LANES), seq // CMP_STRIDE), np.float32)
    out[:n_sel, :n_cmp] = (overlap / CMP_STRIDE).T
    return jnp.asarray(out, BF16)


def _pad_cols(w, n):
    return jnp.pad(w, ((0, 0), (0, n - w.shape[1])))


def ssd_mixer(h, nw, batch, seq, in_proj, conv_w, conv_b, dt_bias, a_log, d_skip, norm_w, out_proj):
    n_heads = dt_bias.shape[0]
    inner = n_heads * SSD_HEAD_DIM
    conv_dim = inner + 2 * SSD_GROUPS * SSD_STATE
    used = inner + conv_dim + n_heads
    w = _pad_cols(in_proj, -(-used // 1536) * 1536).astype(BF16)
    zxd = norm_matmul(h, nw, w, F32, tn=1536)
    y = ssd_core(zxd, batch, seq, conv_w, conv_b, dt_bias, a_log, d_skip, norm_w)
    return matmul_residual(y, out_proj.astype(BF16), h)


def rglru_mixer(h, nw, batch, seq, in_proj, conv_w, conv_b, wa, ba, wx, bx, a_param, out_proj):
    gx = norm_matmul(h, nw, in_proj.astype(BF16), F32, tn=1024)
    y = lru_core(gx, batch, seq, conv_w, conv_b, wa, ba, wx, bx, a_param)
    return matmul_residual(y, out_proj.astype(BF16), h, tm=512, tn=out_proj.shape[1])


def nsa_mixer(h, nw, positions, batch, seq, in_proj, cmp_pe, cmp_w1, cmp_w2, out_proj):
    d = NSA_HEAD_DIM
    groups = NSA_KV_GROUPS
    kvdim = groups * d
    qdim = out_proj.shape[0]
    n_heads = qdim // d
    hpg = n_heads // groups
    col = lambda k: qdim + k * kvdim
    gate_col = 2 * kvdim

    tables = tuple(t.reshape(batch * seq, d) for t in _rope_tables(positions))
    w_rot = jnp.concatenate([in_proj[:, :qdim], in_proj[:, col(2):col(3)], in_proj[:, col(4):col(5)]], axis=1)
    col_scale = jnp.concatenate([jnp.full((qdim,), d ** -0.5 * LOG2_E, F32), jnp.ones((2 * kvdim,), F32)])
    qk = norm_matmul_rope(h, nw, w_rot.astype(BF16), tables, col_scale)

    wg = in_proj[:, col(6):].reshape(-1, groups, 3 * hpg)
    wg = jnp.pad(wg, ((0, 0), (0, 0), (0, LANES - 3 * hpg))).reshape(-1, groups * LANES)
    w = jnp.concatenate([in_proj[:, col(0):col(2)], wg], axis=1)
    proj = norm_matmul(h, nw, w.astype(BF16), F32, tn=w.shape[1])
    w_v_t = jnp.concatenate([in_proj[:, col(3):col(4)], in_proj[:, col(5):col(6)]], axis=1).T.astype(BF16)
    v_sw_t = norm_matmul_t(h, nw, w_v_t, BF16)

    n_rows = seq // CMP_STRIDE
    cmp_end = jnp.minimum(jnp.arange(n_rows) * CMP_STRIDE + CMP_LEN - 1, seq - 1)
    cmp_tables = _rope_tables(positions[:, cmp_end])

    k_cmp = nsa_compress(proj, 0, batch, seq, groups, cmp_pe[0], cmp_w1[0], cmp_w2[0], cmp_tables, True)
    v_cmp_t = nsa_compress(proj, kvdim, batch, seq, groups, cmp_pe[1], cmp_w1[1], cmp_w2[1], cmp_tables, False)

    o = nsa_attention(qk, v_sw_t, k_cmp, v_cmp_t, proj, _cmp_to_sel(seq), batch, seq, qdim, kvdim, gate_col)
    return matmul_residual(o, out_proj.astype(BF16), h, tm=512, tn=out_proj.shape[1])


def kernel(x, p, positions, norm_mix, norm_ffn, norm_ple, w_ple_up, w_ple_gate, w_ffn_in, w_ffn_out, norm_final, ssd_in_proj, ssd_conv_w, ssd_conv_b, ssd_dt_bias, ssd_a_log, ssd_d, ssd_norm, ssd_out_proj, lru_in_proj, lru_conv_w, lru_conv_b, lru_wa, lru_ba, lru_wx, lru_bx, lru_a_param, lru_out_proj, nsa_in_proj, nsa_cmp_pe, nsa_cmp_w1, nsa_cmp_w2, nsa_out_proj):
    batch, seq, d_model = x.shape
    depth = norm_mix.shape[0]
    n_mixers = 3
    m = batch * seq
    h = x.reshape(m, d_model)
    for i in range(depth):
        kind, j = i % n_mixers, i // n_mixers
        if kind == 0:
            h = ssd_mixer(h, norm_mix[i], batch, seq, ssd_in_proj[j], ssd_conv_w[j], ssd_conv_b[j], ssd_dt_bias[j],
                          ssd_a_log[j], ssd_d[j], ssd_norm[j], ssd_out_proj[j])
        elif kind == 1:
            h = rglru_mixer(h, norm_mix[i], batch, seq, lru_in_proj[j], lru_conv_w[j], lru_conv_b[j], lru_wa[j],
                            lru_ba[j], lru_wx[j], lru_bx[j], lru_a_param[j], lru_out_proj[j])
        else:
            h = nsa_mixer(h, norm_mix[i], positions, batch, seq, nsa_in_proj[j], nsa_cmp_pe[j], nsa_cmp_w1[j],
                          nsa_cmp_w2[j], nsa_out_proj[j])
        hidden = norm_swiglu_in(h, norm_ffn[i], w_ffn_in[i].astype(BF16))
        h = matmul_residual(hidden, w_ffn_out[i].astype(BF16), h)
        ple_args = (h, norm_ple[i], w_ple_gate[i].astype(BF16), p[i].reshape(m, -1), w_ple_up[i].astype(BF16))
        h = ple_layer(*ple_args) if i < depth - 1 else ple_final_layer(*ple_args, norm_final)
    return h.reshape(batch, seq, d_model)
```

```python
import functools

import numpy as np
import jax
import jax.numpy as jnp
from jax import lax
from jax.experimental import pallas as pl
from jax.experimental.pallas import tpu as pltpu

F32 = jnp.float32
BF16 = jnp.bfloat16

RMS_EPS = 1e-6
ROPE_THETA = 500000.0
NEG_INF = -1e30
FORCED_SCORE = 1e9
LOG2_E = 1.4426950408889634

VMEM_LIMIT_BYTES = 52 * 1024 * 1024
LANES = 128

SSD_HEAD_DIM = 64
SSD_GROUPS = 8
SSD_STATE = 128
SSD_CONV = 4
SSD_CHUNK = 128

LRU_BLOCK_DIM = 256
LRU_CONV = 4
LRU_C = 8.0
LRU_ROWS = 256

NSA_HEAD_DIM = 128
NSA_KV_GROUPS = 4
ROT_DIM = NSA_HEAD_DIM // 4
CMP_LEN = 32
CMP_STRIDE = 16
SEL_LEN = 64
SEL_TOPK = 16
WINDOW = 512
Q_BLOCK = 128
SEL_KEY_TILE = 512

CONV_HIST = 8


def _cparams(*sem):
    return pltpu.CompilerParams(dimension_semantics=sem, vmem_limit_bytes=VMEM_LIMIT_BYTES)


def _layer_spec(block, index_map, layer):
    if layer is None:
        return pl.BlockSpec(block, index_map)
    return pl.BlockSpec((None,) + block, lambda *g: (layer,) + index_map(*g))


def _dot(a, b):
    return jnp.dot(a, b, preferred_element_type=F32)


def _dot_nt(a, b):
    return lax.dot_general(a, b, (((1,), (1,)), ((), ())), preferred_element_type=F32)


def _split3(x):
    hi = x.astype(BF16)
    r1 = x - hi.astype(F32)
    mid = r1.astype(BF16)
    lo = (r1 - mid.astype(F32)).astype(BF16)
    return hi, mid, lo


def _dot_exact_rhs(x, e):
    hi, mid, lo = _split3(x)
    return _dot(hi, e) + _dot(mid, e) + _dot(lo, e)


def _dot_exact_lhs(e, x):
    hi, mid, lo = _split3(x)
    return _dot(e, hi) + _dot(e, mid) + _dot(e, lo)


def _rms_scale(x):
    return lax.rsqrt(jnp.mean(x * x, axis=-1, keepdims=True) + RMS_EPS)


def _softplus(x):
    return jnp.maximum(x, 0.0) + jnp.log(1.0 + jnp.exp(-jnp.abs(x)))


def _sigmoid(x):
    return jax.nn.sigmoid(x)


def _norm_mm_kernel(x_ref, nw_ref, w_ref, o_ref, u_ref):
    @pl.when(pl.program_id(1) == 0)
    def _():
        x = x_ref[...]
        u_ref[...] = (x * _rms_scale(x) * nw_ref[...]).astype(BF16)

    o_ref[...] = _dot(u_ref[...], w_ref[...]).astype(o_ref.dtype)


def norm_matmul(x, nw, w, out_dtype, tm=1024, tn=512, layer=None):
    m, k = x.shape
    n = w.shape[-1]
    return pl.pallas_call(
        _norm_mm_kernel,
        grid=(m // tm, n // tn),
        in_specs=[
            pl.BlockSpec((tm, k), lambda i, j: (i, 0)),
            pl.BlockSpec((1, k), lambda i, j: (0, 0)),
            _layer_spec((k, tn), lambda i, j: (0, j), layer),
        ],
        out_specs=pl.BlockSpec((tm, tn), lambda i, j: (i, j)),
        out_shape=jax.ShapeDtypeStruct((m, n), out_dtype),
        scratch_shapes=[pltpu.VMEM((tm, k), BF16)],
        compiler_params=_cparams("parallel", "arbitrary"),
        name="norm_matmul",
    )(x, nw.reshape(1, k), w)


def _norm_mm_t_kernel(x_ref, nw_ref, wt_ref, o_ref):
    x = x_ref[...]
    u = (x * _rms_scale(x) * nw_ref[...]).astype(BF16)
    o_ref[...] = _dot_nt(wt_ref[...], u).astype(o_ref.dtype)


def norm_matmul_t(x, nw, w_t, out_dtype, tm=1024):
    m, k = x.shape
    n = w_t.shape[0]
    return pl.pallas_call(
        _norm_mm_t_kernel,
        grid=(m // tm,),
        in_specs=[
            pl.BlockSpec((tm, k), lambda i: (i, 0)),
            pl.BlockSpec((1, k), lambda i: (0, 0)),
            pl.BlockSpec((n, k), lambda i: (0, 0)),
        ],
        out_specs=pl.BlockSpec((n, tm), lambda i: (0, i)),
        out_shape=jax.ShapeDtypeStruct((n, m), out_dtype),
        compiler_params=_cparams("parallel"),
        name="norm_matmul_t",
    )(x, nw.reshape(1, k), w_t)


def _norm_swiglu_kernel(x_ref, nw_ref, wg_ref, wu_ref, o_ref, u_ref):
    @pl.when(pl.program_id(1) == 0)
    def _():
        x = x_ref[...]
        u_ref[...] = (x * _rms_scale(x) * nw_ref[...]).astype(BF16)

    u = u_ref[...]
    gate = _dot(u, wg_ref[...])
    up = _dot(u, wu_ref[...])
    o_ref[...] = (gate * _sigmoid(gate) * up).astype(o_ref.dtype)


def norm_swiglu_in(x, nw, w_in, tm=1024, tn=512, layer=None):
    m, k = x.shape
    f = w_in.shape[-1] // 2
    nj = f // tn
    return pl.pallas_call(
        _norm_swiglu_kernel,
        grid=(m // tm, nj),
        in_specs=[
            pl.BlockSpec((tm, k), lambda i, j: (i, 0)),
            pl.BlockSpec((1, k), lambda i, j: (0, 0)),
            _layer_spec((k, tn), lambda i, j: (0, j), layer),
            _layer_spec((k, tn), lambda i, j: (0, j + nj), layer),
        ],
        out_specs=pl.BlockSpec((tm, tn), lambda i, j: (i, j)),
        out_shape=jax.ShapeDtypeStruct((m, f), BF16),
        scratch_shapes=[pltpu.VMEM((tm, k), BF16)],
        compiler_params=_cparams("parallel", "arbitrary"),
        name="norm_swiglu_in",
    )(x, nw.reshape(1, k), w_in, w_in)


def _mm_residual_kernel(x_ref, w_ref, r_ref, o_ref):
    o_ref[...] = r_ref[...] + _dot(x_ref[...], w_ref[...])


def matmul_residual(x, w, res, tm=1024, tn=512, layer=None):
    m, k = x.shape
    n = w.shape[-1]
    return pl.pallas_call(
        _mm_residual_kernel,
        grid=(m // tm, n // tn),
        in_specs=[
            pl.BlockSpec((tm, k), lambda i, j: (i, 0)),
            _layer_spec((k, tn), lambda i, j: (0, j), layer),
            pl.BlockSpec((tm, tn), lambda i, j: (i, j)),
        ],
        out_specs=pl.BlockSpec((tm, tn), lambda i, j: (i, j)),
        out_shape=jax.ShapeDtypeStruct((m, n), F32),
        compiler_params=_cparams("parallel", "arbitrary"),
        name="matmul_residual",
    )(x, w, res)


def _ple_kernel(final, x_ref, nw_ref, wg_ref, p_ref, wu_ref, fw_ref, o_ref):
    x = x_ref[...]
    u = (x * _rms_scale(x) * nw_ref[...]).astype(BF16)
    y = x + _sigmoid(_dot(u, wg_ref[...])) * _dot(p_ref[...].astype(BF16), wu_ref[...])
    o_ref[...] = y * _rms_scale(y) * fw_ref[...] if final else y


def ple_layer(x, nw, w_gate, p, w_up, layer=None, final_nw=None, tm=512):
    m, k = x.shape
    n = w_gate.shape[-1]
    kp = p.shape[-1]
    final = final_nw is not None
    fw = final_nw.reshape(1, n) if final else jnp.ones((1, n), F32)
    return pl.pallas_call(
        functools.partial(_ple_kernel, final),
        grid=(m // tm,),
        in_specs=[
            pl.BlockSpec((tm, k), lambda i: (i, 0)),
            pl.BlockSpec((1, k), lambda i: (0, 0)),
            _layer_spec((k, n), lambda i: (0, 0), layer),
            _layer_spec((tm, kp), lambda i: (i, 0), layer),
            _layer_spec((kp, n), lambda i: (0, 0), layer),
            pl.BlockSpec((1, n), lambda i: (0, 0)),
        ],
        out_specs=pl.BlockSpec((tm, n), lambda i: (i, 0)),
        out_shape=jax.ShapeDtypeStruct((m, n), F32),
        compiler_params=_cparams("parallel"),
        name="ple_layer",
    )(x, nw.reshape(1, k), w_gate, p, w_up, fw)


def _causal_conv(pad_ref, w_ref, b_ref, rows, lo, width):
    taps = w_ref.shape[0]
    x = pad_ref[0:CONV_HIST + rows, lo:lo + width]
    acc = w_ref[0:1, lo:lo + width] * x
    for k in range(1, taps):
        acc = pltpu.roll(acc, 1, axis=0) + w_ref[k:k + 1, lo:lo + width] * x
    return acc[CONV_HIST:CONV_HIST + rows, :] + b_ref[:, lo:lo + width]


def _ssd_kernel(z_ref, xs_ref, bc_ref, dt_ref, cwx_ref, cbx_ref, cwbc_ref, cbbc_ref, dtb_ref, alog_ref,
                dskip_ref, nw_ref, tril_ref, ehead_ref, o_ref,
                xpad_ref, bcpad_ref, bcact_ref, state_ref, acs_ref, acst_ref, dt_s_ref, xdt_ref, ydiag_ref):
    q = SSD_CHUNK
    gw = xs_ref.shape[1] // SSD_GROUPS
    heads_per_group = gw // SSD_HEAD_DIM
    n_bc = bc_ref.shape[1] // 2

    @pl.when(pl.program_id(1) == 0)
    def _():
        xpad_ref[0:CONV_HIST, :] = jnp.zeros((CONV_HIST, xpad_ref.shape[1]), F32)
        bcpad_ref[0:CONV_HIST, :] = jnp.zeros((CONV_HIST, bcpad_ref.shape[1]), F32)
        state_ref[...] = jnp.zeros(state_ref.shape, F32)

    xpad_ref[CONV_HIST:CONV_HIST + q, :] = xs_ref[...]
    bcpad_ref[CONV_HIST:CONV_HIST + q, :] = bc_ref[...]

    for lo in range(0, bc_ref.shape[1], 512):
        v = _causal_conv(bcpad_ref, cwbc_ref, cbbc_ref, q, lo, 512)
        bcact_ref[:, lo:lo + 512] = v * _sigmoid(v)

    dt = _softplus(dt_ref[...] + dtb_ref[...])
    adt = dt * (-jnp.exp(alog_ref[...]))
    acs = _dot_exact_lhs(tril_ref[...], adt)
    acs_ref[...] = acs
    acst_ref[...] = acs.T
    dt_s_ref[...] = dt

    li = lax.broadcasted_iota(jnp.int32, (q, q), 0)
    si = lax.broadcasted_iota(jnp.int32, (q, q), 1)
    causal = li >= si

    for g in range(SSD_GROUPS):
        glo = g * gw
        e_g = ehead_ref[:, glo:glo + gw]
        xc = _causal_conv(xpad_ref, cwx_ref, cbx_ref, q, glo, gw)
        xs = xc * _sigmoid(xc)
        dt_g = _dot_exact_rhs(dt_s_ref[...], e_g)
        a_g = _dot_exact_rhs(acs_ref[...], e_g)
        a_last = a_g[q - 1:q, :]
        xdt = xs * dt_g
        xdt_ref[...] = xdt.astype(BF16)

        bm = bcact_ref[:, g * SSD_STATE:(g + 1) * SSD_STATE]
        cm = bcact_ref[:, n_bc + g * SSD_STATE:n_bc + (g + 1) * SSD_STATE]
        bm16 = bm.astype(BF16)
        cm16 = cm.astype(BF16)
        cb = _dot_nt(cm16, bm16)

        prev = state_ref[:, glo:glo + gw]
        y_off = _dot(cm16, prev.astype(BF16)) * jnp.exp(a_g)
        st_new = _dot(bm.T.astype(BF16), (xdt * jnp.exp(a_last - a_g)).astype(BF16))
        state_ref[:, glo:glo + gw] = prev * jnp.exp(a_last) + st_new

        for r in range(heads_per_group):
            h = g * heads_per_group + r
            seg = acs_ref[:, h:h + 1] - acst_ref[h:h + 1, :]
            decay = jnp.exp(jnp.where(causal, seg, NEG_INF))
            m_h = (cb * decay).astype(BF16)
            ydiag_ref[:, r * SSD_HEAD_DIM:(r + 1) * SSD_HEAD_DIM] = _dot(
                m_h, xdt_ref[:, r * SSD_HEAD_DIM:(r + 1) * SSD_HEAD_DIM])

        y = ydiag_ref[...] + y_off + dskip_ref[:, glo:glo + gw] * xs
        zg = z_ref[:, glo:glo + gw]
        y = y * (zg * _sigmoid(zg))
        y = y * _rms_scale(y) * nw_ref[:, glo:glo + gw]
        o_ref[:, glo:glo + gw] = y.astype(o_ref.dtype)

    xpad_ref[0:CONV_HIST, :] = xpad_ref[q:q + CONV_HIST, :]
    bcpad_ref[0:CONV_HIST, :] = bcpad_ref[q:q + CONV_HIST, :]


def ssd_core(zxd, batch, seq, conv_w, conv_b, dt_bias, a_log, d_skip, norm_w):
    m = zxd.shape[0]
    n_heads = dt_bias.shape[0]
    inner = n_heads * SSD_HEAD_DIM
    n_bc = SSD_GROUPS * SSD_STATE
    q = SSD_CHUNK
    nc = seq // q
    assert inner % (2 * n_bc) == 0 and n_heads <= LANES
    pad_h = LANES - n_heads

    row = lambda v: v.reshape(1, -1).astype(F32)
    padh = lambda v: jnp.pad(v.astype(F32), (0, pad_h)).reshape(1, LANES)
    tril = jnp.asarray(np.tril(np.ones((q, q), np.float32)), BF16)
    ehead = jnp.asarray(np.repeat(np.eye(LANES, n_heads, dtype=np.float32), SSD_HEAD_DIM, axis=1), BF16)

    rows = lambda b, c: b * nc + c
    full = lambda a: pl.BlockSpec(a.shape, lambda b, c: (0,) * a.ndim)
    consts = [conv_w[:, :inner].astype(F32), row(conv_b[:inner]), conv_w[:, inner:].astype(F32), row(conv_b[inner:]),
              padh(dt_bias), padh(a_log), row(jnp.repeat(d_skip, SSD_HEAD_DIM)), row(norm_w), tril, ehead]
    return pl.pallas_call(
        _ssd_kernel,
        grid=(batch, nc),
        in_specs=[
            pl.BlockSpec((q, inner), lambda b, c: (rows(b, c), 0)),
            pl.BlockSpec((q, inner), lambda b, c: (rows(b, c), 1)),
            pl.BlockSpec((q, 2 * n_bc), lambda b, c: (rows(b, c), 2 * inner // (2 * n_bc))),
            pl.BlockSpec((q, LANES), lambda b, c: (rows(b, c), (2 * inner + 2 * n_bc) // LANES)),
        ] + [full(a) for a in consts],
        out_specs=pl.BlockSpec((q, inner), lambda b, c: (rows(b, c), 0)),
        out_shape=jax.ShapeDtypeStruct((m, inner), BF16),
        scratch_shapes=[
            pltpu.VMEM((q + CONV_HIST, inner), F32),
            pltpu.VMEM((q + CONV_HIST, 2 * n_bc), F32),
            pltpu.VMEM((q, 2 * n_bc), F32),
            pltpu.VMEM((SSD_STATE, inner), F32),
            pltpu.VMEM((q, LANES), F32),
            pltpu.VMEM((LANES, q), F32),
            pltpu.VMEM((q, LANES), F32),
            pltpu.VMEM((q, inner // SSD_GROUPS), BF16),
            pltpu.VMEM((q, inner // SSD_GROUPS), F32),
        ],
        compiler_params=_cparams("parallel", "arbitrary"),
        name="ssd_core",
    )(zxd, zxd, zxd, zxd, *consts)


def _lru_kernel(gate_ref, xr_ref, cw_ref, cb_ref, wa_ref, ba_ref, wx_ref, bx_ref, ap_ref, o_ref,
                xpad_ref, a_ref, b_ref, h_ref, carry_ref):
    rows = gate_ref.shape[0]
    width = gate_ref.shape[1]

    @pl.when(pl.program_id(1) == 0)
    def _():
        xpad_ref[0:CONV_HIST, :] = jnp.zeros((CONV_HIST, width), F32)
        carry_ref[...] = jnp.zeros(carry_ref.shape, F32)

    xpad_ref[CONV_HIST:CONV_HIST + rows, :] = xr_ref[...]
    for k in range(width // LRU_BLOCK_DIM):
        lo = k * LRU_BLOCK_DIM
        sl = slice(lo, lo + LRU_BLOCK_DIM)
        xc = _causal_conv(xpad_ref, cw_ref, cb_ref, rows, lo, LRU_BLOCK_DIM)
        x16 = xc.astype(BF16)
        r_t = _sigmoid(_dot(x16, wa_ref[k]) + ba_ref[:, sl])
        i_t = _sigmoid(_dot(x16, wx_ref[k]) + bx_ref[:, sl])
        log_a = -LRU_C * r_t * _softplus(-ap_ref[:, sl])
        a_t = jnp.exp(log_a)
        a_ref[:, sl] = a_t
        b_ref[:, sl] = jnp.sqrt(1.0 - a_t * a_t) * (i_t * xc)
    xpad_ref[0:CONV_HIST, :] = xpad_ref[rows:rows + CONV_HIST, :]

    def step(t, h):
        h = a_ref[pl.ds(t, 1), :] * h + b_ref[pl.ds(t, 1), :]
        h_ref[pl.ds(t, 1), :] = h
        return h

    carry_ref[0:1, :] = lax.fori_loop(0, rows, step, carry_ref[0:1, :], unroll=8)
    o_ref[...] = (jax.nn.gelu(gate_ref[...], approximate=True) * h_ref[...]).astype(o_ref.dtype)


def lru_core(gx, batch, seq, conv_w, conv_b, wa, ba, wx, bx, a_param):
    m = gx.shape[0]
    width = gx.shape[1] // 2
    rows = LRU_ROWS
    nt = seq // rows
    row = lambda v: v.reshape(1, -1).astype(F32)
    consts = [conv_w.astype(F32), row(conv_b), wa.astype(BF16), row(ba), wx.astype(BF16), row(bx), row(a_param)]
    full = lambda a: pl.BlockSpec(a.shape, lambda b, t: (0,) * a.ndim)
    return pl.pallas_call(
        _lru_kernel,
        grid=(batch, nt),
        in_specs=[
            pl.BlockSpec((rows, width), lambda b, t: (b * nt + t, 0)),
            pl.BlockSpec((rows, width), lambda b, t: (b * nt + t, 1)),
        ] + [full(a) for a in consts],
        out_specs=pl.BlockSpec((rows, width), lambda b, t: (b * nt + t, 0)),
        out_shape=jax.ShapeDtypeStruct((m, width), BF16),
        scratch_shapes=[
            pltpu.VMEM((rows + CONV_HIST, width), F32),
            pltpu.VMEM((rows, width), F32),
            pltpu.VMEM((rows, width), F32),
            pltpu.VMEM((rows, width), F32),
            pltpu.VMEM((8, width), F32),
        ],
        compiler_params=_cparams("parallel", "arbitrary"),
        name="lru_core",
    )(gx, gx, *consts)


def _rope_tables(pos):
    half = ROT_DIM // 2
    inv = ROPE_THETA ** (-jnp.arange(half, dtype=F32) * 2.0 / ROT_DIM)
    ang = pos.astype(F32)[..., None] * inv
    cos, sin = jnp.cos(ang), jnp.sin(ang)
    shape = pos.shape + (NSA_HEAD_DIM - ROT_DIM,)
    zeros_h = jnp.zeros(pos.shape + (half,), F32)
    cos_t = jnp.concatenate([cos, cos, jnp.ones(shape, F32)], axis=-1)
    sin_lo = jnp.concatenate([-sin, zeros_h, jnp.zeros(shape, F32)], axis=-1)
    sin_hi = jnp.concatenate([zeros_h, sin, jnp.zeros(shape, F32)], axis=-1)
    return cos_t, sin_lo, sin_hi


def _apply_rope(x, cos_t, sin_lo, sin_hi):
    n = x.shape[1] // NSA_HEAD_DIM
    half = ROT_DIM // 2
    tile = lambda t: jnp.concatenate([t] * n, axis=1) if n > 1 else t
    up = pltpu.roll(x, x.shape[1] - half, axis=1)
    down = pltpu.roll(x, half, axis=1)
    return x * tile(cos_t) + up * tile(sin_lo) + down * tile(sin_hi)


def _norm_mm_rope_kernel(x_ref, nw_ref, w_ref, cos_ref, slo_ref, shi_ref, cs_ref, o_ref, u_ref):
    @pl.when(pl.program_id(1) == 0)
    def _():
        x = x_ref[...]
        u_ref[...] = (x * _rms_scale(x) * nw_ref[...]).astype(BF16)

    y = _apply_rope(_dot(u_ref[...], w_ref[...]), cos_ref[...], slo_ref[...], shi_ref[...])
    o_ref[...] = (y * cs_ref[...]).astype(o_ref.dtype)


def norm_matmul_rope(x, nw, w, tables, col_scale, tm=1024, tn=768):
    m, k = x.shape
    n = w.shape[1]
    tspec = pl.BlockSpec((tm, NSA_HEAD_DIM), lambda i, j: (i, 0))
    return pl.pallas_call(
        _norm_mm_rope_kernel,
        grid=(m // tm, n // tn),
        in_specs=[
            pl.BlockSpec((tm, k), lambda i, j: (i, 0)),
            pl.BlockSpec((1, k), lambda i, j: (0, 0)),
            pl.BlockSpec((k, tn), lambda i, j: (0, j)),
            tspec, tspec, tspec,
            pl.BlockSpec((1, tn), lambda i, j: (0, j)),
        ],
        out_specs=pl.BlockSpec((tm, tn), lambda i, j: (i, j)),
        out_shape=jax.ShapeDtypeStruct((m, n), BF16),
        scratch_shapes=[pltpu.VMEM((tm, k), BF16)],
        compiler_params=_cparams("parallel", "arbitrary"),
        name="norm_matmul_rope",
    )(x, nw.reshape(1, k), w, *tables, col_scale.reshape(1, n))


def _compress_kernel(is_key, tok_ref, pe_ref, w1_ref, w2_ref, cos_ref, slo_ref, shi_ref, o_ref):
    d = w2_ref.shape[0]
    nrows = tok_ref.shape[0] // CMP_STRIDE
    top = jnp.zeros((nrows, d), F32)
    bot = jnp.zeros((nrows, d), F32)
    for pos in range(CMP_STRIDE):
        tok = tok_ref[pl.ds(pos, nrows, stride=CMP_STRIDE), :]
        lo, hi = pos * d, (CMP_STRIDE + pos) * d
        top = top + _dot((tok + pe_ref[pos:pos + 1, :]).astype(BF16), w1_ref[lo:lo + d, :])
        bot = bot + _dot((tok + pe_ref[CMP_STRIDE + pos:CMP_STRIDE + pos + 1, :]).astype(BF16), w1_ref[hi:hi + d, :])
    pre = top + pltpu.roll(bot, nrows - 1, axis=0)
    out = _dot(jax.nn.gelu(pre, approximate=True).astype(BF16), w2_ref[...])
    if is_key:
        o_ref[...] = _apply_rope(out, cos_ref[...], slo_ref[...], shi_ref[...]).astype(o_ref.dtype)
    else:
        o_ref[...] = out.T.astype(o_ref.dtype)


def nsa_compress(proj, col, batch, seq, groups, pe, w1, w2, tables, is_key):
    d = w2.shape[0]
    nrows = seq // CMP_STRIDE
    assert CMP_LEN == 2 * CMP_STRIDE and col % d == 0
    tspec = pl.BlockSpec((None, nrows, d), lambda i, j: (i, 0, 0))
    const = lambda a: pl.BlockSpec(a.shape, lambda i, j: (0,) * a.ndim)
    out_block = (None, None, nrows, d) if is_key else (None, None, d, nrows)
    out_dims = (batch, groups, nrows, d) if is_key else (batch, groups, d, nrows)
    consts = [pe.astype(F32), w1.astype(BF16), w2.astype(BF16)]
    return pl.pallas_call(
        functools.partial(_compress_kernel, is_key),
        grid=(batch, groups),
        in_specs=[pl.BlockSpec((seq, d), lambda i, j: (i, col // d + j))] + [const(a) for a in consts]
        + [tspec, tspec, tspec],
        out_specs=pl.BlockSpec(out_block, lambda i, j: (i, j, 0, 0)),
        out_shape=jax.ShapeDtypeStruct(out_dims, BF16),
        compiler_params=_cparams("parallel", "arbitrary"),
        name="nsa_compress_k" if is_key else "nsa_compress_v",
    )(proj, *consts, *tables)


def _nsa_attn_kernel(n_sel, q_ref, kcmp_ref, vcmp_ref, ks_ref, vs_ref, kw_ref, vw_ref, gate_ref, c2s_ref, e_ref,
                     o_ref, impt_ref, m_ref, l_ref, acc_ref, out_ref, q4_ref, sa_ref, sb_ref):
    tq = Q_BLOCK
    d = NSA_HEAD_DIM
    hpg = q_ref.shape[1] // d
    n_cmp_pad = kcmp_ref.shape[0]
    n_sel_pad = c2s_ref.shape[0]
    t0 = pl.multiple_of(pl.program_id(2) * tq, tq)
    t_row = t0 + lax.broadcasted_iota(jnp.int32, (1, tq), 1)
    gates = _sigmoid(gate_ref[...]).T
    head = lambda r: slice(r * d, (r + 1) * d)
    cols = lambda r: slice(r * tq, (r + 1) * tq)
    per_head = lambda a: jnp.concatenate([a] * hpg, axis=1)
    for r in range(hpg):
        q4_ref[cols(r), 0:d] = q_ref[:, head(r)]
    q4 = q4_ref[:, 0:d]

    n_col = lax.broadcasted_iota(jnp.int32, (n_cmp_pad, 1), 0)
    ok_c = (n_col * CMP_STRIDE + (CMP_LEN - 1) <= t_row) & (n_col < n_cmp_pad - 1)
    bias_c = jnp.where(ok_c, 0.0, NEG_INF)
    keep_c = jnp.where(ok_c, 1.0, 0.0)
    s = _dot_nt(kcmp_ref[...], q4) + per_head(bias_c)

    n_win = WINDOW + tq
    w0 = pl.multiple_of(jnp.maximum(t0 - WINDOW, 0), tq)
    kp = w0 + lax.broadcasted_iota(jnp.int32, (n_win, 1), 0)
    bias_w = jnp.where((kp <= t_row) & (kp > t_row - WINDOW), 0.0, NEG_INF)
    s_w = _dot_nt(kw_ref[pl.ds(w0, n_win), :], q4) + per_head(bias_w)
    causal = (lax.broadcasted_iota(jnp.int32, (tq, 1), 0) <= lax.broadcasted_iota(jnp.int32, (1, tq), 1))
    s_d = _dot_nt(ks_ref[pl.ds(t0, tq), :], q4) + per_head(jnp.where(causal, 0.0, NEG_INF))

    e = jnp.exp2(s - jnp.max(s, axis=0, keepdims=True)) * per_head(keep_c)
    den = jnp.sum(e, axis=0, keepdims=True)
    p = e * jnp.where(den > 0.0, 1.0 / den, 0.0)
    o_c = _dot(vcmp_ref[...], p.astype(BF16))
    p_sum = p[:, cols(0)]
    for r in range(1, hpg):
        p_sum = p_sum + p[:, cols(r)]
    for r in range(hpg):
        out_ref[:, cols(r)] = gates[3 * r:3 * r + 1, :] * o_c[:, cols(r)]

    imp = _dot_exact_lhs(c2s_ref[...], p_sum)
    j_col = lax.broadcasted_iota(jnp.int32, (n_sel_pad, 1), 0)
    cur = t_row // SEL_LEN
    forced = (j_col == 0) | (j_col == cur) | (j_col == cur - 1)
    imp = jnp.where(forced, FORCED_SCORE, imp)
    impt_ref[...] = jnp.where(j_col <= cur, imp, -jnp.inf)
    n_blk = n_sel // 8
    blks = [impt_ref[jb * 8:(jb + 1) * 8, :] for jb in range(n_blk)]
    cnt = [jnp.zeros((8, tq), F32) for _ in range(n_blk)]
    sub = lax.broadcasted_iota(jnp.int32, (8, 1), 0)
    for i in range(n_sel):
        other = impt_ref[i:i + 1, :]
        for jb in range(n_blk):
            if jb * 8 > i:
                ahead = other >= blks[jb]
            elif jb * 8 + 7 < i:
                ahead = other > blks[jb]
            else:
                ahead = (other > blks[jb]) | ((other == blks[jb]) & (sub > i - jb * 8))
            cnt[jb] = cnt[jb] + jnp.where(ahead, 1.0, 0.0)
    sel_t = [jnp.where(c < float(SEL_TOPK), 1.0, 0.0) for c in cnt]
    if n_sel_pad > n_sel:
        sel_t.append(jnp.zeros((n_sel_pad - n_sel, tq), F32))
    sel = jnp.concatenate(sel_t, axis=0)
    sel_bias = jnp.where((sel > 0.5) & (j_col * SEL_LEN < t0), 0.0, NEG_INF).T.astype(BF16)
    for r in range(hpg):
        q4_ref[cols(r), d:2 * d] = sel_bias

    m_0 = jnp.max(s_d, axis=0, keepdims=True)
    p = jnp.exp2(s_d - m_0)
    m_ref[...] = m_0
    l_ref[...] = jnp.sum(p, axis=0, keepdims=True)
    acc_ref[...] = _dot(vs_ref[:, pl.ds(t0, tq)], p.astype(BF16))

    kt_n = SEL_KEY_TILE

    last_tile = ks_ref.shape[0] // kt_n - 1

    def sel_scores(kt):
        k0 = pl.multiple_of(jnp.minimum(kt, last_tile) * kt_n, kt_n)
        lhs = jnp.concatenate([ks_ref[pl.ds(k0, kt_n), :], e_ref[pl.ds(k0, kt_n), :]], axis=1)
        return _dot_nt(lhs, q4_ref[...])

    def sel_update(kt, s):
        k0 = pl.multiple_of(kt * kt_n, kt_n)
        m_old = m_ref[...]
        m_new = jnp.maximum(m_old, jnp.max(s, axis=0, keepdims=True))
        alpha = jnp.exp2(m_old - m_new)
        p = jnp.exp2(s - m_new)
        l_ref[...] = alpha * l_ref[...] + jnp.sum(p, axis=0, keepdims=True)
        acc_ref[...] = alpha * acc_ref[...] + _dot(vs_ref[:, pl.ds(k0, kt_n)], p.astype(BF16))
        m_ref[...] = m_new

    sa_ref[...] = sel_scores(0)

    e = jnp.exp2(s_w - jnp.max(s_w, axis=0, keepdims=True))
    p = e * (1.0 / jnp.sum(e, axis=0, keepdims=True))
    o_w = _dot(vw_ref[:, pl.ds(w0, n_win)], p.astype(BF16))
    for r in range(hpg):
        out_ref[:, cols(r)] = out_ref[:, cols(r)] + gates[3 * r + 2:3 * r + 3, :] * o_w[:, cols(r)]

    def sel_step(pair, carry):
        sb_ref[...] = sel_scores(2 * pair + 1)
        sel_update(2 * pair, sa_ref[...])
        sa_ref[...] = sel_scores(2 * pair + 2)
        sel_update(2 * pair + 1, sb_ref[...])
        return carry

    lax.fori_loop(0, (t0 + 2 * kt_n - 1) // (2 * kt_n), sel_step, 0)

    o_s = acc_ref[...] * (1.0 / l_ref[...])
    for r in range(hpg):
        o_r = out_ref[:, cols(r)] + gates[3 * r + 1:3 * r + 2, :] * o_s[:, cols(r)]
        o_ref[:, head(r)] = o_r.T.astype(o_ref.dtype)


def nsa_attention(qk, v_sw_t, k_cmp, v_cmp_t, proj, c2s_t, batch, seq, qdim, kvdim, gate_col):
    m = qk.shape[0]
    d = NSA_HEAD_DIM
    groups = kvdim // d
    gq = qdim // groups
    hpg = gq // d
    nq = seq // Q_BLOCK
    n_sel = seq // SEL_LEN
    assert n_sel % 8 == 0 and seq % (2 * SEL_KEY_TILE) == 0 and seq >= WINDOW + Q_BLOCK
    n_cmp_pad = k_cmp.shape[2]
    n_sel_pad = c2s_t.shape[0]
    assert n_sel_pad == d
    block_onehot = jnp.asarray(np.arange(seq)[:, None] // SEL_LEN == np.arange(n_sel_pad)[None, :], BF16)
    k_blk = lambda off: pl.BlockSpec((seq, d), lambda b, g, t: (b, off + g))
    v_blk = lambda off: pl.BlockSpec((d, seq), lambda b, g, t: (off + g, b))
    return pl.pallas_call(
        functools.partial(_nsa_attn_kernel, n_sel),
        grid=(batch, groups, nq),
        in_specs=[
            pl.BlockSpec((Q_BLOCK, gq), lambda b, g, t: (b * nq + t, g)),
            pl.BlockSpec((None, None, n_cmp_pad, d), lambda b, g, t: (b, g, 0, 0)),
            pl.BlockSpec((None, None, d, n_cmp_pad), lambda b, g, t: (b, g, 0, 0)),
            k_blk(qdim // d), v_blk(0), k_blk(qdim // d + groups), v_blk(groups),
            pl.BlockSpec((Q_BLOCK, LANES), lambda b, g, t: (b * nq + t, gate_col // LANES + g)),
            pl.BlockSpec(c2s_t.shape, lambda b, g, t: (0, 0)),
            pl.BlockSpec(block_onehot.shape, lambda b, g, t: (0, 0)),
        ],
        out_specs=pl.BlockSpec((Q_BLOCK, gq), lambda b, g, t: (b * nq + t, g)),
        out_shape=jax.ShapeDtypeStruct((m, qdim), BF16),
        scratch_shapes=[
            pltpu.VMEM((n_sel_pad, Q_BLOCK), F32),
            pltpu.VMEM((1, hpg * Q_BLOCK), F32),
            pltpu.VMEM((1, hpg * Q_BLOCK), F32),
            pltpu.VMEM((d, hpg * Q_BLOCK), F32),
            pltpu.VMEM((d, hpg * Q_BLOCK), F32),
            pltpu.VMEM((hpg * Q_BLOCK, 2 * d), BF16),
            pltpu.VMEM((SEL_KEY_TILE, hpg * Q_BLOCK), F32),
            pltpu.VMEM((SEL_KEY_TILE, hpg * Q_BLOCK), F32),
        ],
        compiler_params=_cparams("parallel", "parallel", "arbitrary"),
        name="nsa_attention",
    )(qk, k_cmp, v_cmp_t, qk, v_sw_t, qk, v_sw_t, proj, c2s_t, block_onehot)


def _cmp_to_sel(seq):
    n_cmp = (seq - CMP_LEN) // CMP_STRIDE + 1
    n_sel = seq // SEL_LEN
    c_start = np.arange(n_cmp)[:, None] * CMP_STRIDE
    s_start = np.arange(n_sel)[None, :] * SEL_LEN
    overlap = np.clip(np.minimum(c_start + CMP_LEN, s_start + SEL_LEN) - np.maximum(c_start, s_start), 0, None)
    out = np.zeros((max(n_sel, LANES), seq // CMP_STRIDE), np.float32)
    out[:n_sel, :n_cmp] = (overlap / CMP_STRIDE).T
    return jnp.asarray(out, BF16)


def ssd_mixer(h, nw, batch, seq, in_proj, conv_w, conv_b, dt_bias, a_log, d_skip, norm_w, out_proj, layer=None):
    used = in_proj.shape[-1]
    tn = 1536
    w = jnp.pad(in_proj, [(0, 0)] * (in_proj.ndim - 1) + [(0, -(-used // tn) * tn - used)]).astype(BF16)
    zxd = norm_matmul(h, nw, w, F32, tn=tn, layer=layer)
    y = ssd_core(zxd, batch, seq, conv_w, conv_b, dt_bias, a_log, d_skip, norm_w)
    return matmul_residual(y, out_proj.astype(BF16), h, layer=layer)


def rglru_mixer(h, nw, batch, seq, in_proj, conv_w, conv_b, wa, ba, wx, bx, a_param, out_proj):
    gx = norm_matmul(h, nw, in_proj.astype(BF16), F32, tn=1024)
    y = lru_core(gx, batch, seq, conv_w, conv_b, wa, ba, wx, bx, a_param)
    return matmul_residual(y, out_proj.astype(BF16), h, tm=512, tn=out_proj.shape[1])


def nsa_mixer(h, nw, positions, batch, seq, in_proj, cmp_pe, cmp_w1, cmp_w2, out_proj):
    d = NSA_HEAD_DIM
    groups = NSA_KV_GROUPS
    kvdim = groups * d
    qdim = out_proj.shape[0]
    n_heads = qdim // d
    hpg = n_heads // groups
    col = lambda k: qdim + k * kvdim
    gate_col = 2 * kvdim

    tables = tuple(t.reshape(batch * seq, d) for t in _rope_tables(positions))
    w_rot = jnp.concatenate([in_proj[:, :qdim], in_proj[:, col(2):col(3)], in_proj[:, col(4):col(5)]], axis=1)
    col_scale = jnp.concatenate([jnp.full((qdim,), d ** -0.5 * LOG2_E, F32), jnp.ones((2 * kvdim,), F32)])
    qk = norm_matmul_rope(h, nw, w_rot.astype(BF16), tables, col_scale)

    wg = in_proj[:, col(6):].reshape(-1, groups, 3 * hpg)
    wg = jnp.pad(wg, ((0, 0), (0, 0), (0, LANES - 3 * hpg))).reshape(-1, groups * LANES)
    w = jnp.concatenate([in_proj[:, col(0):col(2)], wg], axis=1)
    proj = norm_matmul(h, nw, w.astype(BF16), F32, tn=w.shape[1])
    w_v_t = jnp.concatenate([in_proj[:, col(3):col(4)], in_proj[:, col(5):col(6)]], axis=1).T.astype(BF16)
    v_sw_t = norm_matmul_t(h, nw, w_v_t, BF16)

    n_rows = seq // CMP_STRIDE
    cmp_end = jnp.minimum(jnp.arange(n_rows) * CMP_STRIDE + CMP_LEN - 1, seq - 1)
    cmp_tables = _rope_tables(positions[:, cmp_end])

    k_cmp = nsa_compress(proj, 0, batch, seq, groups, cmp_pe[0], cmp_w1[0], cmp_w2[0], cmp_tables, True)
    v_cmp_t = nsa_compress(proj, kvdim, batch, seq, groups, cmp_pe[1], cmp_w1[1], cmp_w2[1], cmp_tables, False)

    o = nsa_attention(qk, v_sw_t, k_cmp, v_cmp_t, proj, _cmp_to_sel(seq), batch, seq, qdim, kvdim, gate_col)
    return matmul_residual(o, out_proj.astype(BF16), h, tm=512, tn=out_proj.shape[1])


def kernel(x, p, positions, norm_mix, norm_ffn, norm_ple, w_ple_up, w_ple_gate, w_ffn_in, w_ffn_out, norm_final, ssd_in_proj, ssd_conv_w, ssd_conv_b, ssd_dt_bias, ssd_a_log, ssd_d, ssd_norm, ssd_out_proj, lru_in_proj, lru_conv_w, lru_conv_b, lru_wa, lru_ba, lru_wx, lru_bx, lru_a_param, lru_out_proj, nsa_in_proj, nsa_cmp_pe, nsa_cmp_w1, nsa_cmp_w2, nsa_out_proj):
    batch, seq, d_model = x.shape
    depth = norm_mix.shape[0]
    n_mixers = 3
    m = batch * seq
    h = x.reshape(m, d_model)
    for i in range(depth):
        kind, j = i % n_mixers, i // n_mixers
        if kind == 0:
            h = ssd_mixer(h, norm_mix[i], batch, seq, ssd_in_proj, ssd_conv_w[j], ssd_conv_b[j], ssd_dt_bias[j],
                          ssd_a_log[j], ssd_d[j], ssd_norm[j], ssd_out_proj, layer=j)
        elif kind == 1:
            h = rglru_mixer(h, norm_mix[i], batch, seq, lru_in_proj[j], lru_conv_w[j], lru_conv_b[j], lru_wa[j],
                            lru_ba[j], lru_wx[j], lru_bx[j], lru_a_param[j], lru_out_proj[j])
        else:
            h = nsa_mixer(h, norm_mix[i], positions, batch, seq, nsa_in_proj[j], nsa_cmp_pe[j], nsa_cmp_w1[j],
                          nsa_cmp_w2[j], nsa_out_proj[j])
        hidden = norm_swiglu_in(h, norm_ffn[i], w_ffn_in.astype(BF16), layer=i)
        h = matmul_residual(hidden, w_ffn_out.astype(BF16), h, layer=i)
        h = ple_layer(h, norm_ple[i], w_ple_gate.astype(BF16), p.reshape(depth, m, -1), w_ple_up.astype(BF16),
                      layer=i, final_nw=norm_final if i == depth - 1 else None)
    return h.reshape(batch, seq, d_model)
```

```python
import functools

import numpy as np
import jax
import jax.numpy as jnp
from jax import lax
from jax.experimental import pallas as pl
from jax.experimental.pallas import tpu as pltpu

F32 = jnp.float32
BF16 = jnp.bfloat16

RMS_EPS = 1e-6
ROPE_THETA = 500000.0
NEG_INF = -1e30
FORCED_SCORE = 1e9
LOG2_E = 1.4426950408889634

VMEM_LIMIT_BYTES = 52 * 1024 * 1024
LANES = 128

SSD_HEAD_DIM = 64
SSD_GROUPS = 8
SSD_STATE = 128
SSD_CONV = 4
SSD_CHUNK = 128

LRU_BLOCK_DIM = 256
LRU_CONV = 4
LRU_C = 8.0
LRU_ROWS = 256

NSA_HEAD_DIM = 128
NSA_KV_GROUPS = 4
ROT_DIM = NSA_HEAD_DIM // 4
CMP_LEN = 32
CMP_STRIDE = 16
SEL_LEN = 64
SEL_TOPK = 16
WINDOW = 512
Q_BLOCK = 128
SEL_KEY_TILE = 512
NSA_GROUPS_PER_STEP = 2

CONV_HIST = 8


def _cparams(*sem):
    return pltpu.CompilerParams(dimension_semantics=sem, vmem_limit_bytes=VMEM_LIMIT_BYTES)


def _layer_spec(block, index_map, layer):
    if layer is None:
        return pl.BlockSpec(block, index_map)
    return pl.BlockSpec((None,) + block, lambda *g: (layer,) + index_map(*g))


def _dot(a, b):
    return jnp.dot(a, b, preferred_element_type=F32)


def _dot_nt(a, b):
    return lax.dot_general(a, b, (((1,), (1,)), ((), ())), preferred_element_type=F32)


def _split3(x):
    hi = x.astype(BF16)
    r1 = x - hi.astype(F32)
    mid = r1.astype(BF16)
    lo = (r1 - mid.astype(F32)).astype(BF16)
    return hi, mid, lo


def _dot_exact_rhs(x, e):
    hi, mid, lo = _split3(x)
    return _dot(hi, e) + _dot(mid, e) + _dot(lo, e)


def _dot_exact_lhs(e, x):
    hi, mid, lo = _split3(x)
    return _dot(e, hi) + _dot(e, mid) + _dot(e, lo)


def _rms_scale(x):
    return lax.rsqrt(jnp.mean(x * x, axis=-1, keepdims=True) + RMS_EPS)


def _softplus(x):
    return jnp.maximum(x, 0.0) + jnp.log(1.0 + jnp.exp(-jnp.abs(x)))


def _sigmoid(x):
    return jax.nn.sigmoid(x)


def _norm_mm_kernel(x_ref, nw_ref, w_ref, o_ref, u_ref):
    @pl.when(pl.program_id(1) == 0)
    def _():
        x = x_ref[...]
        u_ref[...] = (x * _rms_scale(x) * nw_ref[...]).astype(BF16)

    o_ref[...] = _dot(u_ref[...], w_ref[...]).astype(o_ref.dtype)


def norm_matmul(x, nw, w, out_dtype, tm=1024, tn=512, layer=None):
    m, k = x.shape
    n = w.shape[-1]
    return pl.pallas_call(
        _norm_mm_kernel,
        grid=(m // tm, n // tn),
        in_specs=[
            pl.BlockSpec((tm, k), lambda i, j: (i, 0)),
            pl.BlockSpec((1, k), lambda i, j: (0, 0)),
            _layer_spec((k, tn), lambda i, j: (0, j), layer),
        ],
        out_specs=pl.BlockSpec((tm, tn), lambda i, j: (i, j)),
        out_shape=jax.ShapeDtypeStruct((m, n), out_dtype),
        scratch_shapes=[pltpu.VMEM((tm, k), BF16)],
        compiler_params=_cparams("parallel", "arbitrary"),
        name="norm_matmul",
    )(x, nw.reshape(1, k), w)


def _norm_mm_t_kernel(x_ref, nw_ref, wt_ref, o_ref):
    x = x_ref[...]
    u = (x * _rms_scale(x) * nw_ref[...]).astype(BF16)
    o_ref[...] = _dot_nt(wt_ref[...], u).astype(o_ref.dtype)


def norm_matmul_t(x, nw, w_t, out_dtype, tm=1024):
    m, k = x.shape
    n = w_t.shape[0]
    return pl.pallas_call(
        _norm_mm_t_kernel,
        grid=(m // tm,),
        in_specs=[
            pl.BlockSpec((tm, k), lambda i: (i, 0)),
            pl.BlockSpec((1, k), lambda i: (0, 0)),
            pl.BlockSpec((n, k), lambda i: (0, 0)),
        ],
        out_specs=pl.BlockSpec((n, tm), lambda i: (0, i)),
        out_shape=jax.ShapeDtypeStruct((n, m), out_dtype),
        compiler_params=_cparams("parallel"),
        name="norm_matmul_t",
    )(x, nw.reshape(1, k), w_t)


def _norm_swiglu_kernel(x_ref, nw_ref, wg_ref, wu_ref, o_ref, u_ref):
    @pl.when(pl.program_id(1) == 0)
    def _():
        x = x_ref[...]
        u_ref[...] = (x * _rms_scale(x) * nw_ref[...]).astype(BF16)

    u = u_ref[...]
    gate = _dot(u, wg_ref[...])
    up = _dot(u, wu_ref[...])
    o_ref[...] = (gate * _sigmoid(gate) * up).astype(o_ref.dtype)


def norm_swiglu_in(x, nw, w_in, tm=1024, tn=512, layer=None):
    m, k = x.shape
    f = w_in.shape[-1] // 2
    nj = f // tn
    return pl.pallas_call(
        _norm_swiglu_kernel,
        grid=(m // tm, nj),
        in_specs=[
            pl.BlockSpec((tm, k), lambda i, j: (i, 0)),
            pl.BlockSpec((1, k), lambda i, j: (0, 0)),
            _layer_spec((k, tn), lambda i, j: (0, j), layer),
            _layer_spec((k, tn), lambda i, j: (0, j + nj), layer),
        ],
        out_specs=pl.BlockSpec((tm, tn), lambda i, j: (i, j)),
        out_shape=jax.ShapeDtypeStruct((m, f), BF16),
        scratch_shapes=[pltpu.VMEM((tm, k), BF16)],
        compiler_params=_cparams("parallel", "arbitrary"),
        name="norm_swiglu_in",
    )(x, nw.reshape(1, k), w_in, w_in)


def _mm_residual_kernel(x_ref, w_ref, r_ref, o_ref):
    o_ref[...] = r_ref[...] + _dot(x_ref[...], w_ref[...])


def matmul_residual(x, w, res, tm=1024, tn=512, layer=None):
    m, k = x.shape
    n = w.shape[-1]
    return pl.pallas_call(
        _mm_residual_kernel,
        grid=(m // tm, n // tn),
        in_specs=[
            pl.BlockSpec((tm, k), lambda i, j: (i, 0)),
            _layer_spec((k, tn), lambda i, j: (0, j), layer),
            pl.BlockSpec((tm, tn), lambda i, j: (i, j)),
        ],
        out_specs=pl.BlockSpec((tm, tn), lambda i, j: (i, j)),
        out_shape=jax.ShapeDtypeStruct((m, n), F32),
        compiler_params=_cparams("parallel", "arbitrary"),
        name="matmul_residual",
    )(x, w, res)


def _ple_kernel(final, x_ref, nw_ref, wg_ref, p_ref, wu_ref, fw_ref, o_ref):
    x = x_ref[...]
    u = (x * _rms_scale(x) * nw_ref[...]).astype(BF16)
    y = x + _sigmoid(_dot(u, wg_ref[...])) * _dot(p_ref[...].astype(BF16), wu_ref[...])
    o_ref[...] = y * _rms_scale(y) * fw_ref[...] if final else y


def ple_layer(x, nw, w_gate, p, w_up, layer=None, final_nw=None, tm=512):
    m, k = x.shape
    n = w_gate.shape[-1]
    kp = p.shape[-1]
    final = final_nw is not None
    fw = final_nw.reshape(1, n) if final else jnp.ones((1, n), F32)
    return pl.pallas_call(
        functools.partial(_ple_kernel, final),
        grid=(m // tm,),
        in_specs=[
            pl.BlockSpec((tm, k), lambda i: (i, 0)),
            pl.BlockSpec((1, k), lambda i: (0, 0)),
            _layer_spec((k, n), lambda i: (0, 0), layer),
            _layer_spec((tm, kp), lambda i: (i, 0), layer),
            _layer_spec((kp, n), lambda i: (0, 0), layer),
            pl.BlockSpec((1, n), lambda i: (0, 0)),
        ],
        out_specs=pl.BlockSpec((tm, n), lambda i: (i, 0)),
        out_shape=jax.ShapeDtypeStruct((m, n), F32),
        compiler_params=_cparams("parallel"),
        name="ple_layer",
    )(x, nw.reshape(1, k), w_gate, p, w_up, fw)


def _causal_conv(pad_ref, w_ref, b_ref, rows, lo, width):
    taps = w_ref.shape[0]
    x = pad_ref[0:CONV_HIST + rows, lo:lo + width]
    acc = w_ref[0:1, lo:lo + width] * x
    for k in range(1, taps):
        acc = pltpu.roll(acc, 1, axis=0) + w_ref[k:k + 1, lo:lo + width] * x
    return acc[CONV_HIST:CONV_HIST + rows, :] + b_ref[:, lo:lo + width]


def _ssd_kernel(z_ref, xs_ref, bc_ref, dt_ref, cwx_ref, cbx_ref, cwbc_ref, cbbc_ref, dtb_ref, alog_ref,
                dskip_ref, nw_ref, tril_ref, ehead_ref, o_ref,
                xpad_ref, bcpad_ref, bcact_ref, state_ref, acs_ref, acst_ref, dt_s_ref, xdt_ref, ydiag_ref):
    q = SSD_CHUNK
    gw = xs_ref.shape[1] // SSD_GROUPS
    heads_per_group = gw // SSD_HEAD_DIM
    n_bc = bc_ref.shape[1] // 2

    @pl.when(pl.program_id(1) == 0)
    def _():
        xpad_ref[0:CONV_HIST, :] = jnp.zeros((CONV_HIST, xpad_ref.shape[1]), F32)
        bcpad_ref[0:CONV_HIST, :] = jnp.zeros((CONV_HIST, bcpad_ref.shape[1]), F32)
        state_ref[...] = jnp.zeros(state_ref.shape, F32)

    xpad_ref[CONV_HIST:CONV_HIST + q, :] = xs_ref[...]
    bcpad_ref[CONV_HIST:CONV_HIST + q, :] = bc_ref[...]

    for lo in range(0, bc_ref.shape[1], 512):
        v = _causal_conv(bcpad_ref, cwbc_ref, cbbc_ref, q, lo, 512)
        bcact_ref[:, lo:lo + 512] = v * _sigmoid(v)

    dt = _softplus(dt_ref[...] + dtb_ref[...])
    adt = dt * (-jnp.exp(alog_ref[...]))
    acs = _dot_exact_lhs(tril_ref[...], adt)
    acs_ref[...] = acs
    acst_ref[...] = acs.T
    dt_s_ref[...] = dt

    li = lax.broadcasted_iota(jnp.int32, (q, q), 0)
    si = lax.broadcasted_iota(jnp.int32, (q, q), 1)
    causal = li >= si

    for g in range(SSD_GROUPS):
        glo = g * gw
        e_g = ehead_ref[:, glo:glo + gw]
        xc = _causal_conv(xpad_ref, cwx_ref, cbx_ref, q, glo, gw)
        xs = xc * _sigmoid(xc)
        dt_g = _dot_exact_rhs(dt_s_ref[...], e_g)
        a_g = _dot_exact_rhs(acs_ref[...], e_g)
        a_last = a_g[q - 1:q, :]
        xdt = xs * dt_g
        xdt_ref[...] = xdt.astype(BF16)

        bm = bcact_ref[:, g * SSD_STATE:(g + 1) * SSD_STATE]
        cm = bcact_ref[:, n_bc + g * SSD_STATE:n_bc + (g + 1) * SSD_STATE]
        bm16 = bm.astype(BF16)
        cm16 = cm.astype(BF16)
        cb = _dot_nt(cm16, bm16)

        prev = state_ref[:, glo:glo + gw]
        y_off = _dot(cm16, prev.astype(BF16)) * jnp.exp(a_g)
        st_new = _dot(bm.T.astype(BF16), (xdt * jnp.exp(a_last - a_g)).astype(BF16))
        state_ref[:, glo:glo + gw] = prev * jnp.exp(a_last) + st_new

        for r in range(heads_per_group):
            h = g * heads_per_group + r
            seg = acs_ref[:, h:h + 1] - acst_ref[h:h + 1, :]
            decay = jnp.exp(jnp.where(causal, seg, NEG_INF))
            m_h = (cb * decay).astype(BF16)
            ydiag_ref[:, r * SSD_HEAD_DIM:(r + 1) * SSD_HEAD_DIM] = _dot(
                m_h, xdt_ref[:, r * SSD_HEAD_DIM:(r + 1) * SSD_HEAD_DIM])

        y = ydiag_ref[...] + y_off + dskip_ref[:, glo:glo + gw] * xs
        zg = z_ref[:, glo:glo + gw]
        y = y * (zg * _sigmoid(zg))
        y = y * _rms_scale(y) * nw_ref[:, glo:glo + gw]
        o_ref[:, glo:glo + gw] = y.astype(o_ref.dtype)

    xpad_ref[0:CONV_HIST, :] = xpad_ref[q:q + CONV_HIST, :]
    bcpad_ref[0:CONV_HIST, :] = bcpad_ref[q:q + CONV_HIST, :]


def ssd_core(zxd, batch, seq, conv_w, conv_b, dt_bias, a_log, d_skip, norm_w):
    m = zxd.shape[0]
    n_heads = dt_bias.shape[0]
    inner = n_heads * SSD_HEAD_DIM
    n_bc = SSD_GROUPS * SSD_STATE
    q = SSD_CHUNK
    nc = seq // q
    assert inner % (2 * n_bc) == 0 and n_heads <= LANES
    pad_h = LANES - n_heads

    row = lambda v: v.reshape(1, -1).astype(F32)
    padh = lambda v: jnp.pad(v.astype(F32), (0, pad_h)).reshape(1, LANES)
    tril = jnp.asarray(np.tril(np.ones((q, q), np.float32)), BF16)
    ehead = jnp.asarray(np.repeat(np.eye(LANES, n_heads, dtype=np.float32), SSD_HEAD_DIM, axis=1), BF16)

    rows = lambda b, c: b * nc + c
    full = lambda a: pl.BlockSpec(a.shape, lambda b, c: (0,) * a.ndim)
    consts = [conv_w[:, :inner].astype(F32), row(conv_b[:inner]), conv_w[:, inner:].astype(F32), row(conv_b[inner:]),
              padh(dt_bias), padh(a_log), row(jnp.repeat(d_skip, SSD_HEAD_DIM)), row(norm_w), tril, ehead]
    return pl.pallas_call(
        _ssd_kernel,
        grid=(batch, nc),
        in_specs=[
            pl.BlockSpec((q, inner), lambda b, c: (rows(b, c), 0)),
            pl.BlockSpec((q, inner), lambda b, c: (rows(b, c), 1)),
            pl.BlockSpec((q, 2 * n_bc), lambda b, c: (rows(b, c), 2 * inner // (2 * n_bc))),
            pl.BlockSpec((q, LANES), lambda b, c: (rows(b, c), (2 * inner + 2 * n_bc) // LANES)),
        ] + [full(a) for a in consts],
        out_specs=pl.BlockSpec((q, inner), lambda b, c: (rows(b, c), 0)),
        out_shape=jax.ShapeDtypeStruct((m, inner), BF16),
        scratch_shapes=[
            pltpu.VMEM((q + CONV_HIST, inner), F32),
            pltpu.VMEM((q + CONV_HIST, 2 * n_bc), F32),
            pltpu.VMEM((q, 2 * n_bc), F32),
            pltpu.VMEM((SSD_STATE, inner), F32),
            pltpu.VMEM((q, LANES), F32),
            pltpu.VMEM((LANES, q), F32),
            pltpu.VMEM((q, LANES), F32),
            pltpu.VMEM((q, inner // SSD_GROUPS), BF16),
            pltpu.VMEM((q, inner // SSD_GROUPS), F32),
        ],
        compiler_params=_cparams("parallel", "arbitrary"),
        name="ssd_core",
    )(zxd, zxd, zxd, zxd, *consts)


def _lru_kernel(gate_ref, xr_ref, cw_ref, cb_ref, wa_ref, ba_ref, wx_ref, bx_ref, ap_ref, o_ref,
                xpad_ref, a_ref, b_ref, h_ref, carry_ref):
    rows = gate_ref.shape[0]
    width = gate_ref.shape[1]

    @pl.when(pl.program_id(1) == 0)
    def _():
        xpad_ref[0:CONV_HIST, :] = jnp.zeros((CONV_HIST, width), F32)
        carry_ref[...] = jnp.zeros(carry_ref.shape, F32)

    xpad_ref[CONV_HIST:CONV_HIST + rows, :] = xr_ref[...]
    for k in range(width // LRU_BLOCK_DIM):
        lo = k * LRU_BLOCK_DIM
        sl = slice(lo, lo + LRU_BLOCK_DIM)
        xc = _causal_conv(xpad_ref, cw_ref, cb_ref, rows, lo, LRU_BLOCK_DIM)
        x16 = xc.astype(BF16)
        r_t = _sigmoid(_dot(x16, wa_ref[k]) + ba_ref[:, sl])
        i_t = _sigmoid(_dot(x16, wx_ref[k]) + bx_ref[:, sl])
        log_a = -LRU_C * r_t * _softplus(-ap_ref[:, sl])
        a_t = jnp.exp(log_a)
        a_ref[:, sl] = a_t
        b_ref[:, sl] = jnp.sqrt(1.0 - a_t * a_t) * (i_t * xc)
    xpad_ref[0:CONV_HIST, :] = xpad_ref[rows:rows + CONV_HIST, :]

    def step(t, h):
        h = a_ref[pl.ds(t, 1), :] * h + b_ref[pl.ds(t, 1), :]
        h_ref[pl.ds(t, 1), :] = h
        return h

    carry_ref[0:1, :] = lax.fori_loop(0, rows, step, carry_ref[0:1, :], unroll=8)
    o_ref[...] = (jax.nn.gelu(gate_ref[...], approximate=True) * h_ref[...]).astype(o_ref.dtype)


def lru_core(gx, batch, seq, conv_w, conv_b, wa, ba, wx, bx, a_param):
    m = gx.shape[0]
    width = gx.shape[1] // 2
    rows = LRU_ROWS
    nt = seq // rows
    row = lambda v: v.reshape(1, -1).astype(F32)
    consts = [conv_w.astype(F32), row(conv_b), wa.astype(BF16), row(ba), wx.astype(BF16), row(bx), row(a_param)]
    full = lambda a: pl.BlockSpec(a.shape, lambda b, t: (0,) * a.ndim)
    return pl.pallas_call(
        _lru_kernel,
        grid=(batch, nt),
        in_specs=[
            pl.BlockSpec((rows, width), lambda b, t: (b * nt + t, 0)),
            pl.BlockSpec((rows, width), lambda b, t: (b * nt + t, 1)),
        ] + [full(a) for a in consts],
        out_specs=pl.BlockSpec((rows, width), lambda b, t: (b * nt + t, 0)),
        out_shape=jax.ShapeDtypeStruct((m, width), BF16),
        scratch_shapes=[
            pltpu.VMEM((rows + CONV_HIST, width), F32),
            pltpu.VMEM((rows, width), F32),
            pltpu.VMEM((rows, width), F32),
            pltpu.VMEM((rows, width), F32),
            pltpu.VMEM((8, width), F32),
        ],
        compiler_params=_cparams("parallel", "arbitrary"),
        name="lru_core",
    )(gx, gx, *consts)


def _rope_tables(pos):
    half = ROT_DIM // 2
    inv = ROPE_THETA ** (-jnp.arange(half, dtype=F32) * 2.0 / ROT_DIM)
    ang = pos.astype(F32)[..., None] * inv
    cos, sin = jnp.cos(ang), jnp.sin(ang)
    shape = pos.shape + (NSA_HEAD_DIM - ROT_DIM,)
    zeros_h = jnp.zeros(pos.shape + (half,), F32)
    cos_t = jnp.concatenate([cos, cos, jnp.ones(shape, F32)], axis=-1)
    sin_lo = jnp.concatenate([-sin, zeros_h, jnp.zeros(shape, F32)], axis=-1)
    sin_hi = jnp.concatenate([zeros_h, sin, jnp.zeros(shape, F32)], axis=-1)
    return cos_t, sin_lo, sin_hi


def _apply_rope(x, cos_t, sin_lo, sin_hi):
    n = x.shape[1] // NSA_HEAD_DIM
    half = ROT_DIM // 2
    tile = lambda t: jnp.concatenate([t] * n, axis=1) if n > 1 else t
    up = pltpu.roll(x, x.shape[1] - half, axis=1)
    down = pltpu.roll(x, half, axis=1)
    return x * tile(cos_t) + up * tile(sin_lo) + down * tile(sin_hi)


def _norm_mm_rope_kernel(x_ref, nw_ref, w_ref, cos_ref, slo_ref, shi_ref, cs_ref, o_ref, u_ref):
    @pl.when(pl.program_id(1) == 0)
    def _():
        x = x_ref[...]
        u_ref[...] = (x * _rms_scale(x) * nw_ref[...]).astype(BF16)

    y = _apply_rope(_dot(u_ref[...], w_ref[...]), cos_ref[...], slo_ref[...], shi_ref[...])
    o_ref[...] = (y * cs_ref[...]).astype(o_ref.dtype)


def norm_matmul_rope(x, nw, w, tables, col_scale, tm=1024, tn=768):
    m, k = x.shape
    n = w.shape[1]
    tspec = pl.BlockSpec((tm, NSA_HEAD_DIM), lambda i, j: (i, 0))
    return pl.pallas_call(
        _norm_mm_rope_kernel,
        grid=(m // tm, n // tn),
        in_specs=[
            pl.BlockSpec((tm, k), lambda i, j: (i, 0)),
            pl.BlockSpec((1, k), lambda i, j: (0, 0)),
            pl.BlockSpec((k, tn), lambda i, j: (0, j)),
            tspec, tspec, tspec,
            pl.BlockSpec((1, tn), lambda i, j: (0, j)),
        ],
        out_specs=pl.BlockSpec((tm, tn), lambda i, j: (i, j)),
        out_shape=jax.ShapeDtypeStruct((m, n), BF16),
        scratch_shapes=[pltpu.VMEM((tm, k), BF16)],
        compiler_params=_cparams("parallel", "arbitrary"),
        name="norm_matmul_rope",
    )(x, nw.reshape(1, k), w, *tables, col_scale.reshape(1, n))


def _compress_kernel(is_key, tok_ref, pe_ref, w1_ref, w2_ref, cos_ref, slo_ref, shi_ref, o_ref):
    d = w2_ref.shape[0]
    nrows = tok_ref.shape[0] // CMP_STRIDE
    top = jnp.zeros((nrows, d), F32)
    bot = jnp.zeros((nrows, d), F32)
    for pos in range(CMP_STRIDE):
        tok = tok_ref[pl.ds(pos, nrows, stride=CMP_STRIDE), :]
        lo, hi = pos * d, (CMP_STRIDE + pos) * d
        top = top + _dot((tok + pe_ref[pos:pos + 1, :]).astype(BF16), w1_ref[lo:lo + d, :])
        bot = bot + _dot((tok + pe_ref[CMP_STRIDE + pos:CMP_STRIDE + pos + 1, :]).astype(BF16), w1_ref[hi:hi + d, :])
    pre = top + pltpu.roll(bot, nrows - 1, axis=0)
    out = _dot(jax.nn.gelu(pre, approximate=True).astype(BF16), w2_ref[...])
    if is_key:
        o_ref[...] = _apply_rope(out, cos_ref[...], slo_ref[...], shi_ref[...]).astype(o_ref.dtype)
    else:
        o_ref[...] = out.T.astype(o_ref.dtype)


def nsa_compress(proj, col, batch, seq, groups, pe, w1, w2, tables, is_key):
    d = w2.shape[0]
    nrows = seq // CMP_STRIDE
    assert CMP_LEN == 2 * CMP_STRIDE and col % d == 0
    tspec = pl.BlockSpec((None, nrows, d), lambda i, j: (i, 0, 0))
    const = lambda a: pl.BlockSpec(a.shape, lambda i, j: (0,) * a.ndim)
    out_block = (None, None, nrows, d) if is_key else (None, None, d, nrows)
    out_dims = (batch, groups, nrows, d) if is_key else (batch, groups, d, nrows)
    consts = [pe.astype(F32), w1.astype(BF16), w2.astype(BF16)]
    return pl.pallas_call(
        functools.partial(_compress_kernel, is_key),
        grid=(batch, groups),
        in_specs=[pl.BlockSpec((seq, d), lambda i, j: (i, col // d + j))] + [const(a) for a in consts]
        + [tspec, tspec, tspec],
        out_specs=pl.BlockSpec(out_block, lambda i, j: (i, j, 0, 0)),
        out_shape=jax.ShapeDtypeStruct(out_dims, BF16),
        compiler_params=_cparams("parallel", "arbitrary"),
        name="nsa_compress_k" if is_key else "nsa_compress_v",
    )(proj, *consts, *tables)


def _nsa_attn_kernel(n_sel, q_ref, kcmp_ref, vcmp_ref, ks_ref, vs_ref, kw_ref, vw_ref, gate_ref, c2s_ref, e_ref,
                     o_ref, impt_ref, m_ref, l_ref, acc_ref, out_ref, q4_ref, sa_ref, sb_ref):
    tq = Q_BLOCK
    d = NSA_HEAD_DIM
    gps = kcmp_ref.shape[0]
    hpg = q_ref.shape[1] // (gps * d)
    n_cmp_pad = kcmp_ref.shape[1]
    n_sel_pad = c2s_ref.shape[0]
    groups = range(gps)
    t0 = pl.multiple_of(pl.program_id(2) * tq, tq)
    t_row = t0 + lax.broadcasted_iota(jnp.int32, (1, tq), 1)
    gates = [_sigmoid(gate_ref[:, gi * LANES:(gi + 1) * LANES]).T for gi in groups]
    grp = lambda gi: slice(gi * d, (gi + 1) * d)
    head = lambda gi, r: slice((gi * hpg + r) * d, (gi * hpg + r + 1) * d)
    cols = lambda r: slice(r * tq, (r + 1) * tq)
    per_head = lambda a: jnp.concatenate([a] * hpg, axis=1)
    for gi in groups:
        for r in range(hpg):
            q4_ref[gi, cols(r), 0:d] = q_ref[:, head(gi, r)]
    q4 = [q4_ref[gi, :, 0:d] for gi in groups]

    n_col = lax.broadcasted_iota(jnp.int32, (n_cmp_pad, 1), 0)
    ok_c = (n_col * CMP_STRIDE + (CMP_LEN - 1) <= t_row) & (n_col < n_cmp_pad - 1)
    bias_c = per_head(jnp.where(ok_c, 0.0, NEG_INF))
    keep_c = per_head(jnp.where(ok_c, 1.0, 0.0))
    n_win = WINDOW + tq
    w0 = pl.multiple_of(jnp.maximum(t0 - WINDOW, 0), tq)
    kp = w0 + lax.broadcasted_iota(jnp.int32, (n_win, 1), 0)
    bias_w = per_head(jnp.where((kp <= t_row) & (kp > t_row - WINDOW), 0.0, NEG_INF))
    causal = (lax.broadcasted_iota(jnp.int32, (tq, 1), 0) <= lax.broadcasted_iota(jnp.int32, (1, tq), 1))
    bias_d = per_head(jnp.where(causal, 0.0, NEG_INF))

    s_c = [_dot_nt(kcmp_ref[gi], q4[gi]) + bias_c for gi in groups]
    s_w = [_dot_nt(kw_ref[pl.ds(w0, n_win), grp(gi)], q4[gi]) + bias_w for gi in groups]
    s_d = [_dot_nt(ks_ref[pl.ds(t0, tq), grp(gi)], q4[gi]) + bias_d for gi in groups]

    j_col = lax.broadcasted_iota(jnp.int32, (n_sel_pad, 1), 0)
    cur = t_row // SEL_LEN
    forced = (j_col == 0) | (j_col == cur) | (j_col == cur - 1)
    sub = lax.broadcasted_iota(jnp.int32, (8, 1), 0)
    n_blk = n_sel // 8
    for gi in groups:
        e = jnp.exp2(s_c[gi] - jnp.max(s_c[gi], axis=0, keepdims=True)) * keep_c
        den = jnp.sum(e, axis=0, keepdims=True)
        p = e * jnp.where(den > 0.0, 1.0 / den, 0.0)
        o_c = _dot(vcmp_ref[gi], p.astype(BF16))
        p_sum = p[:, cols(0)]
        for r in range(1, hpg):
            p_sum = p_sum + p[:, cols(r)]
        for r in range(hpg):
            out_ref[gi, :, cols(r)] = gates[gi][3 * r:3 * r + 1, :] * o_c[:, cols(r)]

        imp = _dot_exact_lhs(c2s_ref[...], p_sum)
        imp = jnp.where(forced, FORCED_SCORE, imp)
        impt_ref[gi] = jnp.where(j_col <= cur, imp, -jnp.inf)
        blks = [impt_ref[gi, jb * 8:(jb + 1) * 8, :] for jb in range(n_blk)]
        cnt = [jnp.zeros((8, tq), F32) for _ in range(n_blk)]
        for i in range(n_sel):
            other = impt_ref[gi, i:i + 1, :]
            for jb in range(n_blk):
                if jb * 8 > i:
                    ahead = other >= blks[jb]
                elif jb * 8 + 7 < i:
                    ahead = other > blks[jb]
                else:
                    ahead = (other > blks[jb]) | ((other == blks[jb]) & (sub > i - jb * 8))
                cnt[jb] = cnt[jb] + jnp.where(ahead, 1.0, 0.0)
        sel_t = [jnp.where(c < float(SEL_TOPK), 1.0, 0.0) for c in cnt]
        if n_sel_pad > n_sel:
            sel_t.append(jnp.zeros((n_sel_pad - n_sel, tq), F32))
        sel = jnp.concatenate(sel_t, axis=0)
        sel_bias = jnp.where((sel > 0.5) & (j_col * SEL_LEN < t0), 0.0, NEG_INF).T.astype(BF16)
        for r in range(hpg):
            q4_ref[gi, cols(r), d:2 * d] = sel_bias

        m_0 = jnp.max(s_d[gi], axis=0, keepdims=True)
        p = jnp.exp2(s_d[gi] - m_0)
        m_ref[gi] = m_0
        l_ref[gi] = jnp.sum(p, axis=0, keepdims=True)
        acc_ref[gi] = _dot(vs_ref[grp(gi), pl.ds(t0, tq)], p.astype(BF16))

    kt_n = SEL_KEY_TILE
    last_tile = ks_ref.shape[0] // kt_n - 1
    n_tiles = (t0 + kt_n - 1) // kt_n

    def sel_scores(gi, kt):
        k0 = pl.multiple_of(jnp.minimum(kt, last_tile) * kt_n, kt_n)
        lhs = jnp.concatenate([ks_ref[pl.ds(k0, kt_n), grp(gi)], e_ref[pl.ds(k0, kt_n), :]], axis=1)
        return _dot_nt(lhs, q4_ref[gi])

    def sel_update(gi, kt, s):
        k0 = pl.multiple_of(kt * kt_n, kt_n)
        m_old = m_ref[gi]
        m_new = jnp.maximum(m_old, jnp.max(s, axis=0, keepdims=True))
        alpha = jnp.exp2(m_old - m_new)
        p = jnp.exp2(s - m_new)
        l_ref[gi] = alpha * l_ref[gi] + jnp.sum(p, axis=0, keepdims=True)
        acc_ref[gi] = alpha * acc_ref[gi] + _dot(vs_ref[grp(gi), pl.ds(k0, kt_n)], p.astype(BF16))
        m_ref[gi] = m_new

    for gi in groups:
        sa_ref[gi] = sel_scores(gi, 0)

    for gi in groups:
        e = jnp.exp2(s_w[gi] - jnp.max(s_w[gi], axis=0, keepdims=True))
        p = e * (1.0 / jnp.sum(e, axis=0, keepdims=True))
        o_w = _dot(vw_ref[grp(gi), pl.ds(w0, n_win)], p.astype(BF16))
        for r in range(hpg):
            out_ref[gi, :, cols(r)] = out_ref[gi, :, cols(r)] + gates[gi][3 * r + 2:3 * r + 3, :] * o_w[:, cols(r)]

    def sel_step(pair, carry):
        for gi in groups:
            sb_ref[gi] = sel_scores(gi, 2 * pair + 1)
        for gi in groups:
            sel_update(gi, 2 * pair, sa_ref[gi])
        for gi in groups:
            sa_ref[gi] = sel_scores(gi, 2 * pair + 2)
        for gi in groups:
            sel_update(gi, 2 * pair + 1, sb_ref[gi])
        return carry

    lax.fori_loop(0, n_tiles // 2, sel_step, 0)

    @pl.when(n_tiles % 2 == 1)
    def _():
        for gi in groups:
            sel_update(gi, n_tiles - 1, sa_ref[gi])

    for gi in groups:
        o_s = acc_ref[gi] * (1.0 / l_ref[gi])
        for r in range(hpg):
            o_r = out_ref[gi, :, cols(r)] + gates[gi][3 * r + 1:3 * r + 2, :] * o_s[:, cols(r)]
            o_ref[:, head(gi, r)] = o_r.T.astype(o_ref.dtype)


def nsa_attention(qk, v_sw_t, k_cmp, v_cmp_t, proj, c2s_t, batch, seq, qdim, kvdim, gate_col):
    m = qk.shape[0]
    d = NSA_HEAD_DIM
    groups = kvdim // d
    gq = qdim // groups
    hpg = gq // d
    nq = seq // Q_BLOCK
    n_sel = seq // SEL_LEN
    gps = NSA_GROUPS_PER_STEP
    assert n_sel % 8 == 0 and seq % SEL_KEY_TILE == 0 and seq >= WINDOW + Q_BLOCK and groups % gps == 0
    n_cmp_pad = k_cmp.shape[2]
    n_sel_pad = c2s_t.shape[0]
    assert n_sel_pad == d
    block_onehot = jnp.asarray(np.arange(seq)[:, None] // SEL_LEN == np.arange(n_sel_pad)[None, :], BF16)
    k_blk = lambda off: pl.BlockSpec((seq, gps * d), lambda b, g, t: (b, off // (gps * d) + g))
    v_blk = lambda off: pl.BlockSpec((gps * d, seq), lambda b, g, t: (off // (gps * d) + g, b))
    lanes_q = hpg * Q_BLOCK
    return pl.pallas_call(
        functools.partial(_nsa_attn_kernel, n_sel),
        grid=(batch, groups // gps, nq),
        in_specs=[
            pl.BlockSpec((Q_BLOCK, gps * gq), lambda b, g, t: (b * nq + t, g)),
            pl.BlockSpec((None, gps, n_cmp_pad, d), lambda b, g, t: (b, g, 0, 0)),
            pl.BlockSpec((None, gps, d, n_cmp_pad), lambda b, g, t: (b, g, 0, 0)),
            k_blk(qdim), v_blk(0), k_blk(qdim + kvdim), v_blk(kvdim),
            pl.BlockSpec((Q_BLOCK, gps * LANES), lambda b, g, t: (b * nq + t, gate_col // (gps * LANES) + g)),
            pl.BlockSpec(c2s_t.shape, lambda b, g, t: (0, 0)),
            pl.BlockSpec(block_onehot.shape, lambda b, g, t: (0, 0)),
        ],
        out_specs=pl.BlockSpec((Q_BLOCK, gps * gq), lambda b, g, t: (b * nq + t, g)),
        out_shape=jax.ShapeDtypeStruct((m, qdim), BF16),
        scratch_shapes=[
            pltpu.VMEM((gps, n_sel_pad, Q_BLOCK), F32),
            pltpu.VMEM((gps, 1, lanes_q), F32),
            pltpu.VMEM((gps, 1, lanes_q), F32),
            pltpu.VMEM((gps, d, lanes_q), F32),
            pltpu.VMEM((gps, d, lanes_q), F32),
            pltpu.VMEM((gps, lanes_q, 2 * d), BF16),
            pltpu.VMEM((gps, SEL_KEY_TILE, lanes_q), F32),
            pltpu.VMEM((gps, SEL_KEY_TILE, lanes_q), F32),
        ],
        compiler_params=_cparams("parallel", "parallel", "arbitrary"),
        name="nsa_attention",
    )(qk, k_cmp, v_cmp_t, qk, v_sw_t, qk, v_sw_t, proj, c2s_t, block_onehot)


def _cmp_to_sel(seq):
    n_cmp = (seq - CMP_LEN) // CMP_STRIDE + 1
    n_sel = seq // SEL_LEN
    c_start = np.arange(n_cmp)[:, None] * CMP_STRIDE
    s_start = np.arange(n_sel)[None, :] * SEL_LEN
    overlap = np.clip(np.minimum(c_start + CMP_LEN, s_start + SEL_LEN) - np.maximum(c_start, s_start), 0, None)
    out = np.zeros((max(n_sel, LANES), seq // CMP_STRIDE), np.float32)
    out[:n_sel, :n_cmp] = (overlap / CMP_STRIDE).T
    return jnp.asarray(out, BF16)


def ssd_mixer(h, nw, batch, seq, in_proj, conv_w, conv_b, dt_bias, a_log, d_skip, norm_w, out_proj, layer=None):
    used = in_proj.shape[-1]
    tn = 1536
    w = jnp.pad(in_proj, [(0, 0)] * (in_proj.ndim - 1) + [(0, -(-used // tn) * tn - used)]).astype(BF16)
    zxd = norm_matmul(h, nw, w, F32, tn=tn, layer=layer)
    y = ssd_core(zxd, batch, seq, conv_w, conv_b, dt_bias, a_log, d_skip, norm_w)
    return matmul_residual(y, out_proj.astype(BF16), h, layer=layer)


def rglru_mixer(h, nw, batch, seq, in_proj, conv_w, conv_b, wa, ba, wx, bx, a_param, out_proj):
    gx = norm_matmul(h, nw, in_proj.astype(BF16), F32, tn=1024)
    y = lru_core(gx, batch, seq, conv_w, conv_b, wa, ba, wx, bx, a_param)
    return matmul_residual(y, out_proj.astype(BF16), h, tm=512, tn=out_proj.shape[1])


def nsa_mixer(h, nw, positions, batch, seq, in_proj, cmp_pe, cmp_w1, cmp_w2, out_proj):
    d = NSA_HEAD_DIM
    groups = NSA_KV_GROUPS
    kvdim = groups * d
    qdim = out_proj.shape[0]
    n_heads = qdim // d
    hpg = n_heads // groups
    col = lambda k: qdim + k * kvdim
    gate_col = 2 * kvdim

    tables = tuple(t.reshape(batch * seq, d) for t in _rope_tables(positions))
    w_rot = jnp.concatenate([in_proj[:, :qdim], in_proj[:, col(2):col(3)], in_proj[:, col(4):col(5)]], axis=1)
    col_scale = jnp.concatenate([jnp.full((qdim,), d ** -0.5 * LOG2_E, F32), jnp.ones((2 * kvdim,), F32)])
    qk = norm_matmul_rope(h, nw, w_rot.astype(BF16), tables, col_scale)

    wg = in_proj[:, col(6):].reshape(-1, groups, 3 * hpg)
    wg = jnp.pad(wg, ((0, 0), (0, 0), (0, LANES - 3 * hpg))).reshape(-1, groups * LANES)
    w = jnp.concatenate([in_proj[:, col(0):col(2)], wg], axis=1)
    proj = norm_matmul(h, nw, w.astype(BF16), F32, tn=w.shape[1])
    w_v_t = jnp.concatenate([in_proj[:, col(3):col(4)], in_proj[:, col(5):col(6)]], axis=1).T.astype(BF16)
    v_sw_t = norm_matmul_t(h, nw, w_v_t, BF16)

    n_rows = seq // CMP_STRIDE
    cmp_end = jnp.minimum(jnp.arange(n_rows) * CMP_STRIDE + CMP_LEN - 1, seq - 1)
    cmp_tables = _rope_tables(positions[:, cmp_end])

    k_cmp = nsa_compress(proj, 0, batch, seq, groups, cmp_pe[0], cmp_w1[0], cmp_w2[0], cmp_tables, True)
    v_cmp_t = nsa_compress(proj, kvdim, batch, seq, groups, cmp_pe[1], cmp_w1[1], cmp_w2[1], cmp_tables, False)

    o = nsa_attention(qk, v_sw_t, k_cmp, v_cmp_t, proj, _cmp_to_sel(seq), batch, seq, qdim, kvdim, gate_col)
    return matmul_residual(o, out_proj.astype(BF16), h, tm=512, tn=out_proj.shape[1])


def kernel(x, p, positions, norm_mix, norm_ffn, norm_ple, w_ple_up, w_ple_gate, w_ffn_in, w_ffn_out, norm_final, ssd_in_proj, ssd_conv_w, ssd_conv_b, ssd_dt_bias, ssd_a_log, ssd_d, ssd_norm, ssd_out_proj, lru_in_proj, lru_conv_w, lru_conv_b, lru_wa, lru_ba, lru_wx, lru_bx, lru_a_param, lru_out_proj, nsa_in_proj, nsa_cmp_pe, nsa_cmp_w1, nsa_cmp_w2, nsa_out_proj):
    batch, seq, d_model = x.shape
    depth = norm_mix.shape[0]
    n_mixers = 3
    m = batch * seq
    h = x.reshape(m, d_model)
    for i in range(depth):
        kind, j = i % n_mixers, i // n_mixers
        if kind == 0:
            h = ssd_mixer(h, norm_mix[i], batch, seq, ssd_in_proj, ssd_conv_w[j], ssd_conv_b[j], ssd_dt_bias[j],
                          ssd_a_log[j], ssd_d[j], ssd_norm[j], ssd_out_proj, layer=j)
        elif kind == 1:
            h = rglru_mixer(h, norm_mix[i], batch, seq, lru_in_proj[j], lru_conv_w[j], lru_conv_b[j], lru_wa[j],
                            lru_ba[j], lru_wx[j], lru_bx[j], lru_a_param[j], lru_out_proj[j])
        else:
            h = nsa_mixer(h, norm_mix[i], positions, batch, seq, nsa_in_proj[j], nsa_cmp_pe[j], nsa_cmp_w1[j],
                          nsa_cmp_w2[j], nsa_out_proj[j])
        hidden = norm_swiglu_in(h, norm_ffn[i], w_ffn_in.astype(BF16), layer=i)
        h = matmul_residual(hidden, w_ffn_out.astype(BF16), h, layer=i)
        h = ple_layer(h, norm_ple[i], w_ple_gate.astype(BF16), p.reshape(depth, m, -1), w_ple_up.astype(BF16),
                      layer=i, final_nw=norm_final if i == depth - 1 else None)
    return h.reshape(batch, seq, d_model)
```

```python
import functools

import numpy as np
import jax
import jax.numpy as jnp
from jax import lax
from jax.experimental import pallas as pl
from jax.experimental.pallas import tpu as pltpu

F32 = jnp.float32
BF16 = jnp.bfloat16

RMS_EPS = 1e-6
ROPE_THETA = 500000.0
NEG_INF = -1e30
FORCED_SCORE = 1e9
LOG2_E = 1.4426950408889634

VMEM_LIMIT_BYTES = 52 * 1024 * 1024
LANES = 128

SSD_HEAD_DIM = 64
SSD_GROUPS = 8
SSD_STATE = 128
SSD_CONV = 4
SSD_CHUNK = 128

LRU_BLOCK_DIM = 256
LRU_CONV = 4
LRU_C = 8.0
LRU_ROWS = 256

NSA_HEAD_DIM = 128
NSA_KV_GROUPS = 4
ROT_DIM = NSA_HEAD_DIM // 4
CMP_LEN = 32
CMP_STRIDE = 16
SEL_LEN = 64
SEL_TOPK = 16
WINDOW = 512
Q_BLOCK = 128
SEL_KEY_TILE = 512
NSA_GROUPS_PER_STEP = 2

CONV_HIST = 8


def _cparams(*sem):
    return pltpu.CompilerParams(dimension_semantics=sem, vmem_limit_bytes=VMEM_LIMIT_BYTES)


def _layer_spec(block, index_map, layer):
    if layer is None:
        return pl.BlockSpec(block, index_map)
    return pl.BlockSpec((None,) + block, lambda *g: (layer,) + index_map(*g))


def _dot(a, b):
    return jnp.dot(a, b, preferred_element_type=F32)


def _dot_nt(a, b):
    return lax.dot_general(a, b, (((1,), (1,)), ((), ())), preferred_element_type=F32)


def _split3(x):
    hi = x.astype(BF16)
    r1 = x - hi.astype(F32)
    mid = r1.astype(BF16)
    lo = (r1 - mid.astype(F32)).astype(BF16)
    return hi, mid, lo


def _dot_exact_rhs(x, e):
    hi, mid, lo = _split3(x)
    return _dot(hi, e) + _dot(mid, e) + _dot(lo, e)


def _dot_exact_lhs(e, x):
    hi, mid, lo = _split3(x)
    return _dot(e, hi) + _dot(e, mid) + _dot(e, lo)


def _rms_scale(x):
    return lax.rsqrt(jnp.mean(x * x, axis=-1, keepdims=True) + RMS_EPS)


def _softplus(x):
    return jnp.maximum(x, 0.0) + jnp.log(1.0 + jnp.exp(-jnp.abs(x)))


def _sigmoid(x):
    return 0.5 + 0.5 * jnp.tanh(0.5 * x)


def _silu(x):
    h = 0.5 * x
    return h + h * jnp.tanh(h)


def _norm_mm_kernel(x_ref, nw_ref, w_ref, o_ref, u_ref):
    @pl.when(pl.program_id(1) == 0)
    def _():
        x = x_ref[...]
        u_ref[...] = (x * _rms_scale(x) * nw_ref[...]).astype(BF16)

    o_ref[...] = _dot(u_ref[...], w_ref[...]).astype(o_ref.dtype)


def norm_matmul(x, nw, w, out_dtype, tm=1024, tn=512, layer=None):
    m, k = x.shape
    n = w.shape[-1]
    return pl.pallas_call(
        _norm_mm_kernel,
        grid=(m // tm, n // tn),
        in_specs=[
            pl.BlockSpec((tm, k), lambda i, j: (i, 0)),
            pl.BlockSpec((1, k), lambda i, j: (0, 0)),
            _layer_spec((k, tn), lambda i, j: (0, j), layer),
        ],
        out_specs=pl.BlockSpec((tm, tn), lambda i, j: (i, j)),
        out_shape=jax.ShapeDtypeStruct((m, n), out_dtype),
        scratch_shapes=[pltpu.VMEM((tm, k), BF16)],
        compiler_params=_cparams("parallel", "arbitrary"),
        name="norm_matmul",
    )(x, nw.reshape(1, k), w)


def _norm_mm_t_kernel(x_ref, nw_ref, wt_ref, o_ref):
    x = x_ref[...]
    u = (x * _rms_scale(x) * nw_ref[...]).astype(BF16)
    o_ref[...] = _dot_nt(wt_ref[...], u).astype(o_ref.dtype)


def norm_matmul_t(x, nw, w_t, out_dtype, tm=1024):
    m, k = x.shape
    n = w_t.shape[0]
    return pl.pallas_call(
        _norm_mm_t_kernel,
        grid=(m // tm,),
        in_specs=[
            pl.BlockSpec((tm, k), lambda i: (i, 0)),
            pl.BlockSpec((1, k), lambda i: (0, 0)),
            pl.BlockSpec((n, k), lambda i: (0, 0)),
        ],
        out_specs=pl.BlockSpec((n, tm), lambda i: (0, i)),
        out_shape=jax.ShapeDtypeStruct((n, m), out_dtype),
        compiler_params=_cparams("parallel"),
        name="norm_matmul_t",
    )(x, nw.reshape(1, k), w_t)


def _norm_swiglu_kernel(x_ref, nw_ref, wg_ref, wu_ref, o_ref, u_ref):
    @pl.when(pl.program_id(1) == 0)
    def _():
        x = x_ref[...]
        u_ref[...] = (x * _rms_scale(x) * nw_ref[...]).astype(BF16)

    u = u_ref[...]
    gate = _dot(u, wg_ref[...])
    up = _dot(u, wu_ref[...])
    o_ref[...] = (_silu(gate) * up).astype(o_ref.dtype)


def norm_swiglu_in(x, nw, w_in, tm=1024, tn=512, layer=None):
    m, k = x.shape
    f = w_in.shape[-1] // 2
    nj = f // tn
    return pl.pallas_call(
        _norm_swiglu_kernel,
        grid=(m // tm, nj),
        in_specs=[
            pl.BlockSpec((tm, k), lambda i, j: (i, 0)),
            pl.BlockSpec((1, k), lambda i, j: (0, 0)),
            _layer_spec((k, tn), lambda i, j: (0, j), layer),
            _layer_spec((k, tn), lambda i, j: (0, j + nj), layer),
        ],
        out_specs=pl.BlockSpec((tm, tn), lambda i, j: (i, j)),
        out_shape=jax.ShapeDtypeStruct((m, f), BF16),
        scratch_shapes=[pltpu.VMEM((tm, k), BF16)],
        compiler_params=_cparams("parallel", "arbitrary"),
        name="norm_swiglu_in",
    )(x, nw.reshape(1, k), w_in, w_in)


def _mm_residual_kernel(x_ref, w_ref, r_ref, o_ref):
    o_ref[...] = r_ref[...] + _dot(x_ref[...], w_ref[...])


def matmul_residual(x, w, res, tm=1024, tn=512, layer=None):
    m, k = x.shape
    n = w.shape[-1]
    return pl.pallas_call(
        _mm_residual_kernel,
        grid=(m // tm, n // tn),
        in_specs=[
            pl.BlockSpec((tm, k), lambda i, j: (i, 0)),
            _layer_spec((k, tn), lambda i, j: (0, j), layer),
            pl.BlockSpec((tm, tn), lambda i, j: (i, j)),
        ],
        out_specs=pl.BlockSpec((tm, tn), lambda i, j: (i, j)),
        out_shape=jax.ShapeDtypeStruct((m, n), F32),
        compiler_params=_cparams("parallel", "arbitrary"),
        name="matmul_residual",
    )(x, w, res)


def _ple_kernel(final, x_ref, nw_ref, wg_ref, p_ref, wu_ref, fw_ref, o_ref):
    x = x_ref[...]
    u = (x * _rms_scale(x) * nw_ref[...]).astype(BF16)
    y = x + _sigmoid(_dot(u, wg_ref[...])) * _dot(p_ref[...].astype(BF16), wu_ref[...])
    o_ref[...] = y * _rms_scale(y) * fw_ref[...] if final else y


def ple_layer(x, nw, w_gate, p, w_up, layer=None, final_nw=None, tm=512):
    m, k = x.shape
    n = w_gate.shape[-1]
    kp = p.shape[-1]
    final = final_nw is not None
    fw = final_nw.reshape(1, n) if final else jnp.ones((1, n), F32)
    return pl.pallas_call(
        functools.partial(_ple_kernel, final),
        grid=(m // tm,),
        in_specs=[
            pl.BlockSpec((tm, k), lambda i: (i, 0)),
            pl.BlockSpec((1, k), lambda i: (0, 0)),
            _layer_spec((k, n), lambda i: (0, 0), layer),
            _layer_spec((tm, kp), lambda i: (i, 0), layer),
            _layer_spec((kp, n), lambda i: (0, 0), layer),
            pl.BlockSpec((1, n), lambda i: (0, 0)),
        ],
        out_specs=pl.BlockSpec((tm, n), lambda i: (i, 0)),
        out_shape=jax.ShapeDtypeStruct((m, n), F32),
        compiler_params=_cparams("parallel"),
        name="ple_layer",
    )(x, nw.reshape(1, k), w_gate, p, w_up, fw)


def _causal_conv(pad_ref, w_ref, b_ref, rows, lo, width):
    taps = w_ref.shape[0]
    x = pad_ref[0:CONV_HIST + rows, lo:lo + width]
    acc = w_ref[0:1, lo:lo + width] * x
    for k in range(1, taps):
        acc = pltpu.roll(acc, 1, axis=0) + w_ref[k:k + 1, lo:lo + width] * x
    return acc[CONV_HIST:CONV_HIST + rows, :] + b_ref[:, lo:lo + width]


def _ssd_kernel(z_ref, xs_ref, bc_ref, dt_ref, cwx_ref, cbx_ref, cwbc_ref, cbbc_ref, dtb_ref, alog_ref,
                dskip_ref, nw_ref, tril_ref, ehead_ref, o_ref,
                xpad_ref, bcpad_ref, bcact_ref, state_ref, acs_ref, acst_ref, xdt_ref, ydiag_ref):
    q = SSD_CHUNK
    gw = xs_ref.shape[1] // SSD_GROUPS
    heads_per_group = gw // SSD_HEAD_DIM
    n_bc = bc_ref.shape[1] // 2

    @pl.when(pl.program_id(1) == 0)
    def _():
        xpad_ref[0:CONV_HIST, :] = jnp.zeros((CONV_HIST, xpad_ref.shape[1]), F32)
        bcpad_ref[0:CONV_HIST, :] = jnp.zeros((CONV_HIST, bcpad_ref.shape[1]), F32)
        state_ref[...] = jnp.zeros(state_ref.shape, F32)

    xpad_ref[CONV_HIST:CONV_HIST + q, :] = xs_ref[...]
    bcpad_ref[CONV_HIST:CONV_HIST + q, :] = bc_ref[...]

    for lo in range(0, bc_ref.shape[1], 512):
        v = _causal_conv(bcpad_ref, cwbc_ref, cbbc_ref, q, lo, 512)
        bcact_ref[:, lo:lo + 512] = _silu(v)

    dt = _softplus(dt_ref[...] + dtb_ref[...])
    adt = dt * (-jnp.exp(alog_ref[...]))
    acs = _dot_exact_lhs(tril_ref[...], adt) * LOG2_E
    acs_ref[...] = acs
    acst_ref[...] = acs.T
    dt_parts = _split3(dt)
    acs_parts = _split3(acs)

    li = lax.broadcasted_iota(jnp.int32, (q, q), 0)
    si = lax.broadcasted_iota(jnp.int32, (q, q), 1)
    causal = li >= si

    for g in range(SSD_GROUPS):
        glo = g * gw
        e_g = ehead_ref[:, glo:glo + gw]
        expand = lambda parts: _dot(parts[0], e_g) + _dot(parts[1], e_g) + _dot(parts[2], e_g)
        xc = _causal_conv(xpad_ref, cwx_ref, cbx_ref, q, glo, gw)
        xs = _silu(xc)
        dt_g = expand(dt_parts)
        a_g = expand(acs_parts)
        a_last = a_g[q - 1:q, :]
        xdt = xs * dt_g
        xdt_ref[...] = xdt.astype(BF16)

        bm = bcact_ref[:, g * SSD_STATE:(g + 1) * SSD_STATE]
        cm = bcact_ref[:, n_bc + g * SSD_STATE:n_bc + (g + 1) * SSD_STATE]
        bm16 = bm.astype(BF16)
        cm16 = cm.astype(BF16)
        cb = _dot_nt(cm16, bm16)

        prev = state_ref[:, glo:glo + gw]
        y_off = _dot(cm16, prev.astype(BF16)) * jnp.exp2(a_g)
        st_new = _dot(bm.T.astype(BF16), (xdt * jnp.exp2(a_last - a_g)).astype(BF16))
        state_ref[:, glo:glo + gw] = prev * jnp.exp2(a_last) + st_new

        for r in range(heads_per_group):
            h = g * heads_per_group + r
            seg = acs_ref[:, h:h + 1] - acst_ref[h:h + 1, :]
            decay = jnp.exp2(jnp.where(causal, seg, NEG_INF))
            m_h = (cb * decay).astype(BF16)
            ydiag_ref[:, r * SSD_HEAD_DIM:(r + 1) * SSD_HEAD_DIM] = _dot(
                m_h, xdt_ref[:, r * SSD_HEAD_DIM:(r + 1) * SSD_HEAD_DIM])

        y = ydiag_ref[...] + y_off + dskip_ref[:, glo:glo + gw] * xs
        zg = z_ref[:, glo:glo + gw]
        y = y * _silu(zg)
        y = y * _rms_scale(y) * nw_ref[:, glo:glo + gw]
        o_ref[:, glo:glo + gw] = y.astype(o_ref.dtype)

    xpad_ref[0:CONV_HIST, :] = xpad_ref[q:q + CONV_HIST, :]
    bcpad_ref[0:CONV_HIST, :] = bcpad_ref[q:q + CONV_HIST, :]


def ssd_core(zxd, batch, seq, conv_w, conv_b, dt_bias, a_log, d_skip, norm_w):
    m = zxd.shape[0]
    n_heads = dt_bias.shape[0]
    inner = n_heads * SSD_HEAD_DIM
    n_bc = SSD_GROUPS * SSD_STATE
    q = SSD_CHUNK
    nc = seq // q
    assert inner % (2 * n_bc) == 0 and n_heads <= LANES
    pad_h = LANES - n_heads

    row = lambda v: v.reshape(1, -1).astype(F32)
    padh = lambda v: jnp.pad(v.astype(F32), (0, pad_h)).reshape(1, LANES)
    tril = jnp.asarray(np.tril(np.ones((q, q), np.float32)), BF16)
    ehead = jnp.asarray(np.repeat(np.eye(LANES, n_heads, dtype=np.float32), SSD_HEAD_DIM, axis=1), BF16)

    rows = lambda b, c: b * nc + c
    full = lambda a: pl.BlockSpec(a.shape, lambda b, c: (0,) * a.ndim)
    consts = [conv_w[:, :inner].astype(F32), row(conv_b[:inner]), conv_w[:, inner:].astype(F32), row(conv_b[inner:]),
              padh(dt_bias), padh(a_log), row(jnp.repeat(d_skip, SSD_HEAD_DIM)), row(norm_w), tril, ehead]
    return pl.pallas_call(
        _ssd_kernel,
        grid=(batch, nc),
        in_specs=[
            pl.BlockSpec((q, inner), lambda b, c: (rows(b, c), 0)),
            pl.BlockSpec((q, inner), lambda b, c: (rows(b, c), 1)),
            pl.BlockSpec((q, 2 * n_bc), lambda b, c: (rows(b, c), 2 * inner // (2 * n_bc))),
            pl.BlockSpec((q, LANES), lambda b, c: (rows(b, c), (2 * inner + 2 * n_bc) // LANES)),
        ] + [full(a) for a in consts],
        out_specs=pl.BlockSpec((q, inner), lambda b, c: (rows(b, c), 0)),
        out_shape=jax.ShapeDtypeStruct((m, inner), BF16),
        scratch_shapes=[
            pltpu.VMEM((q + CONV_HIST, inner), F32),
            pltpu.VMEM((q + CONV_HIST, 2 * n_bc), F32),
            pltpu.VMEM((q, 2 * n_bc), F32),
            pltpu.VMEM((SSD_STATE, inner), F32),
            pltpu.VMEM((q, LANES), F32),
            pltpu.VMEM((LANES, q), F32),
            pltpu.VMEM((q, inner // SSD_GROUPS), BF16),
            pltpu.VMEM((q, inner // SSD_GROUPS), F32),
        ],
        compiler_params=_cparams("parallel", "arbitrary"),
        name="ssd_core",
    )(zxd, zxd, zxd, zxd, *consts)


def _lru_kernel(gate_ref, xr_ref, cw_ref, cb_ref, wa_ref, ba_ref, wx_ref, bx_ref, ap_ref, o_ref,
                xpad_ref, a_ref, b_ref, h_ref, carry_ref):
    rows = gate_ref.shape[0]
    width = gate_ref.shape[1]

    @pl.when(pl.program_id(1) == 0)
    def _():
        xpad_ref[0:CONV_HIST, :] = jnp.zeros((CONV_HIST, width), F32)
        carry_ref[...] = jnp.zeros(carry_ref.shape, F32)

    xpad_ref[CONV_HIST:CONV_HIST + rows, :] = xr_ref[...]
    for k in range(width // LRU_BLOCK_DIM):
        lo = k * LRU_BLOCK_DIM
        sl = slice(lo, lo + LRU_BLOCK_DIM)
        xc = _causal_conv(xpad_ref, cw_ref, cb_ref, rows, lo, LRU_BLOCK_DIM)
        x16 = xc.astype(BF16)
        r_t = _sigmoid(_dot(x16, wa_ref[k]) + ba_ref[:, sl])
        i_t = _sigmoid(_dot(x16, wx_ref[k]) + bx_ref[:, sl])
        log_a = -LRU_C * r_t * _softplus(-ap_ref[:, sl])
        a_t = jnp.exp(log_a)
        a_ref[:, sl] = a_t
        b_ref[:, sl] = jnp.sqrt(1.0 - a_t * a_t) * (i_t * xc)
    xpad_ref[0:CONV_HIST, :] = xpad_ref[rows:rows + CONV_HIST, :]

    def step(t, h):
        h = a_ref[pl.ds(t, 1), :] * h + b_ref[pl.ds(t, 1), :]
        h_ref[pl.ds(t, 1), :] = h
        return h

    carry_ref[0:1, :] = lax.fori_loop(0, rows, step, carry_ref[0:1, :], unroll=8)
    o_ref[...] = (jax.nn.gelu(gate_ref[...], approximate=True) * h_ref[...]).astype(o_ref.dtype)


def lru_core(gx, batch, seq, conv_w, conv_b, wa, ba, wx, bx, a_param):
    m = gx.shape[0]
    width = gx.shape[1] // 2
    rows = LRU_ROWS
    nt = seq // rows
    row = lambda v: v.reshape(1, -1).astype(F32)
    consts = [conv_w.astype(F32), row(conv_b), wa.astype(BF16), row(ba), wx.astype(BF16), row(bx), row(a_param)]
    full = lambda a: pl.BlockSpec(a.shape, lambda b, t: (0,) * a.ndim)
    return pl.pallas_call(
        _lru_kernel,
        grid=(batch, nt),
        in_specs=[
            pl.BlockSpec((rows, width), lambda b, t: (b * nt + t, 0)),
            pl.BlockSpec((rows, width), lambda b, t: (b * nt + t, 1)),
        ] + [full(a) for a in consts],
        out_specs=pl.BlockSpec((rows, width), lambda b, t: (b * nt + t, 0)),
        out_shape=jax.ShapeDtypeStruct((m, width), BF16),
        scratch_shapes=[
            pltpu.VMEM((rows + CONV_HIST, width), F32),
            pltpu.VMEM((rows, width), F32),
            pltpu.VMEM((rows, width), F32),
            pltpu.VMEM((rows, width), F32),
            pltpu.VMEM((8, width), F32),
        ],
        compiler_params=_cparams("parallel", "arbitrary"),
        name="lru_core",
    )(gx, gx, *consts)


def _rope_tables(pos):
    half = ROT_DIM // 2
    inv = ROPE_THETA ** (-jnp.arange(half, dtype=F32) * 2.0 / ROT_DIM)
    ang = pos.astype(F32)[..., None] * inv
    cos, sin = jnp.cos(ang), jnp.sin(ang)
    shape = pos.shape + (NSA_HEAD_DIM - ROT_DIM,)
    zeros_h = jnp.zeros(pos.shape + (half,), F32)
    cos_t = jnp.concatenate([cos, cos, jnp.ones(shape, F32)], axis=-1)
    sin_lo = jnp.concatenate([-sin, zeros_h, jnp.zeros(shape, F32)], axis=-1)
    sin_hi = jnp.concatenate([zeros_h, sin, jnp.zeros(shape, F32)], axis=-1)
    return cos_t, sin_lo, sin_hi


def _apply_rope(x, cos_t, sin_lo, sin_hi):
    n = x.shape[1] // NSA_HEAD_DIM
    half = ROT_DIM // 2
    tile = lambda t: jnp.concatenate([t] * n, axis=1) if n > 1 else t
    up = pltpu.roll(x, x.shape[1] - half, axis=1)
    down = pltpu.roll(x, half, axis=1)
    return x * tile(cos_t) + up * tile(sin_lo) + down * tile(sin_hi)


def _norm_mm_rope_kernel(x_ref, nw_ref, w_ref, cos_ref, slo_ref, shi_ref, cs_ref, o_ref, u_ref):
    @pl.when(pl.program_id(1) == 0)
    def _():
        x = x_ref[...]
        u_ref[...] = (x * _rms_scale(x) * nw_ref[...]).astype(BF16)

    y = _apply_rope(_dot(u_ref[...], w_ref[...]), cos_ref[...], slo_ref[...], shi_ref[...])
    o_ref[...] = (y * cs_ref[...]).astype(o_ref.dtype)


def norm_matmul_rope(x, nw, w, tables, col_scale, tm=1024, tn=768):
    m, k = x.shape
    n = w.shape[1]
    tspec = pl.BlockSpec((tm, NSA_HEAD_DIM), lambda i, j: (i, 0))
    return pl.pallas_call(
        _norm_mm_rope_kernel,
        grid=(m // tm, n // tn),
        in_specs=[
            pl.BlockSpec((tm, k), lambda i, j: (i, 0)),
            pl.BlockSpec((1, k), lambda i, j: (0, 0)),
            pl.BlockSpec((k, tn), lambda i, j: (0, j)),
            tspec, tspec, tspec,
            pl.BlockSpec((1, tn), lambda i, j: (0, j)),
        ],
        out_specs=pl.BlockSpec((tm, tn), lambda i, j: (i, j)),
        out_shape=jax.ShapeDtypeStruct((m, n), BF16),
        scratch_shapes=[pltpu.VMEM((tm, k), BF16)],
        compiler_params=_cparams("parallel", "arbitrary"),
        name="norm_matmul_rope",
    )(x, nw.reshape(1, k), w, *tables, col_scale.reshape(1, n))


def _compress_kernel(is_key, tok_ref, pe_ref, w1_ref, w2_ref, cos_ref, slo_ref, shi_ref, o_ref):
    d = w2_ref.shape[0]
    nrows = tok_ref.shape[0] // CMP_STRIDE
    top = jnp.zeros((nrows, d), F32)
    bot = jnp.zeros((nrows, d), F32)
    for pos in range(CMP_STRIDE):
        tok = tok_ref[pl.ds(pos, nrows, stride=CMP_STRIDE), :]
        lo, hi = pos * d, (CMP_STRIDE + pos) * d
        top = top + _dot((tok + pe_ref[pos:pos + 1, :]).astype(BF16), w1_ref[lo:lo + d, :])
        bot = bot + _dot((tok + pe_ref[CMP_STRIDE + pos:CMP_STRIDE + pos + 1, :]).astype(BF16), w1_ref[hi:hi + d, :])
    pre = top + pltpu.roll(bot, nrows - 1, axis=0)
    out = _dot(jax.nn.gelu(pre, approximate=True).astype(BF16), w2_ref[...])
    if is_key:
        o_ref[...] = _apply_rope(out, cos_ref[...], slo_ref[...], shi_ref[...]).astype(o_ref.dtype)
    else:
        o_ref[...] = out.T.astype(o_ref.dtype)


def nsa_compress(proj, col, batch, seq, groups, pe, w1, w2, tables, is_key):
    d = w2.shape[0]
    nrows = seq // CMP_STRIDE
    assert CMP_LEN == 2 * CMP_STRIDE and col % d == 0
    tspec = pl.BlockSpec((None, nrows, d), lambda i, j: (i, 0, 0))
    const = lambda a: pl.BlockSpec(a.shape, lambda i, j: (0,) * a.ndim)
    out_block = (None, None, nrows, d) if is_key else (None, None, d, nrows)
    out_dims = (batch, groups, nrows, d) if is_key else (batch, groups, d, nrows)
    consts = [pe.astype(F32), w1.astype(BF16), w2.astype(BF16)]
    return pl.pallas_call(
        functools.partial(_compress_kernel, is_key),
        grid=(batch, groups),
        in_specs=[pl.BlockSpec((seq, d), lambda i, j: (i, col // d + j))] + [const(a) for a in consts]
        + [tspec, tspec, tspec],
        out_specs=pl.BlockSpec(out_block, lambda i, j: (i, j, 0, 0)),
        out_shape=jax.ShapeDtypeStruct(out_dims, BF16),
        compiler_params=_cparams("parallel", "arbitrary"),
        name="nsa_compress_k" if is_key else "nsa_compress_v",
    )(proj, *consts, *tables)


def _nsa_attn_kernel(n_sel, q_ref, kcmp_ref, vcmp_ref, ks_ref, vs_ref, kw_ref, vw_ref, gate_ref, c2s_ref, e_ref,
                     o_ref, impt_ref, m_ref, l_ref, acc_ref, out_ref, q4_ref, sa_ref, sb_ref):
    tq = Q_BLOCK
    d = NSA_HEAD_DIM
    gps = kcmp_ref.shape[0]
    hpg = q_ref.shape[1] // (gps * d)
    n_cmp_pad = kcmp_ref.shape[1]
    n_sel_pad = c2s_ref.shape[0]
    groups = range(gps)
    t0 = pl.multiple_of(pl.program_id(2) * tq, tq)
    t_row = t0 + lax.broadcasted_iota(jnp.int32, (1, tq), 1)
    gates = [_sigmoid(gate_ref[:, gi * LANES:(gi + 1) * LANES]).T for gi in groups]
    grp = lambda gi: slice(gi * d, (gi + 1) * d)
    head = lambda gi, r: slice((gi * hpg + r) * d, (gi * hpg + r + 1) * d)
    cols = lambda r: slice(r * tq, (r + 1) * tq)
    per_head = lambda a: jnp.concatenate([a] * hpg, axis=1)
    for gi in groups:
        for r in range(hpg):
            q4_ref[gi, cols(r), 0:d] = q_ref[:, head(gi, r)]
    q4 = [q4_ref[gi, :, 0:d] for gi in groups]

    n_col = lax.broadcasted_iota(jnp.int32, (n_cmp_pad, 1), 0)
    ok_c = (n_col * CMP_STRIDE + (CMP_LEN - 1) <= t_row) & (n_col < n_cmp_pad - 1)
    bias_c = per_head(jnp.where(ok_c, 0.0, NEG_INF))
    keep_c = per_head(jnp.where(ok_c, 1.0, 0.0))
    n_win = WINDOW + tq
    w0 = pl.multiple_of(jnp.maximum(t0 - WINDOW, 0), tq)
    kp = w0 + lax.broadcasted_iota(jnp.int32, (n_win, 1), 0)
    bias_w = per_head(jnp.where((kp <= t_row) & (kp > t_row - WINDOW), 0.0, NEG_INF))
    causal = (lax.broadcasted_iota(jnp.int32, (tq, 1), 0) <= lax.broadcasted_iota(jnp.int32, (1, tq), 1))
    bias_d = per_head(jnp.where(causal, 0.0, NEG_INF))

    s_c = [_dot_nt(kcmp_ref[gi], q4[gi]) + bias_c for gi in groups]
    s_w = [_dot_nt(kw_ref[pl.ds(w0, n_win), grp(gi)], q4[gi]) + bias_w for gi in groups]
    s_d = [_dot_nt(ks_ref[pl.ds(t0, tq), grp(gi)], q4[gi]) + bias_d for gi in groups]

    j_col = lax.broadcasted_iota(jnp.int32, (n_sel_pad, 1), 0)
    cur = t_row // SEL_LEN
    forced = (j_col == 0) | (j_col == cur) | (j_col == cur - 1)
    sub = lax.broadcasted_iota(jnp.int32, (8, 1), 0)
    n_blk = n_sel // 8
    for gi in groups:
        e = jnp.exp2(s_c[gi] - jnp.max(s_c[gi], axis=0, keepdims=True)) * keep_c
        den = jnp.sum(e, axis=0, keepdims=True)
        p = e * jnp.where(den > 0.0, 1.0 / den, 0.0)
        o_c = _dot(vcmp_ref[gi], p.astype(BF16))
        p_sum = p[:, cols(0)]
        for r in range(1, hpg):
            p_sum = p_sum + p[:, cols(r)]
        for r in range(hpg):
            out_ref[gi, :, cols(r)] = gates[gi][3 * r:3 * r + 1, :] * o_c[:, cols(r)]

        imp = _dot_exact_lhs(c2s_ref[...], p_sum)
        imp = jnp.where(forced, FORCED_SCORE, imp)
        impt_ref[gi] = jnp.where(j_col <= cur, imp, -jnp.inf)
        blks = [impt_ref[gi, jb * 8:(jb + 1) * 8, :] for jb in range(n_blk)]
        cnt = [jnp.zeros((8, tq), F32) for _ in range(n_blk)]
        for i in range(n_sel):
            other = impt_ref[gi, i:i + 1, :]
            for jb in range(n_blk):
                if jb * 8 > i:
                    ahead = other >= blks[jb]
                elif jb * 8 + 7 < i:
                    ahead = other > blks[jb]
                else:
                    ahead = (other > blks[jb]) | ((other == blks[jb]) & (sub > i - jb * 8))
                cnt[jb] = cnt[jb] + jnp.where(ahead, 1.0, 0.0)
        sel_t = [jnp.where(c < float(SEL_TOPK), 1.0, 0.0) for c in cnt]
        if n_sel_pad > n_sel:
            sel_t.append(jnp.zeros((n_sel_pad - n_sel, tq), F32))
        sel = jnp.concatenate(sel_t, axis=0)
        sel_bias = jnp.where((sel > 0.5) & (j_col * SEL_LEN < t0), 0.0, NEG_INF).T.astype(BF16)
        for r in range(hpg):
            q4_ref[gi, cols(r), d:2 * d] = sel_bias

        m_0 = jnp.max(s_d[gi], axis=0, keepdims=True)
        p = jnp.exp2(s_d[gi] - m_0)
        m_ref[gi] = m_0
        l_ref[gi] = jnp.sum(p, axis=0, keepdims=True)
        acc_ref[gi] = _dot(vs_ref[grp(gi), pl.ds(t0, tq)], p.astype(BF16))

    kt_n = SEL_KEY_TILE
    last_tile = ks_ref.shape[0] // kt_n - 1
    n_tiles = (t0 + kt_n - 1) // kt_n

    def sel_scores(gi, kt):
        k0 = pl.multiple_of(jnp.minimum(kt, last_tile) * kt_n, kt_n)
        lhs = jnp.concatenate([ks_ref[pl.ds(k0, kt_n), grp(gi)], e_ref[pl.ds(k0, kt_n), :]], axis=1)
        return _dot_nt(lhs, q4_ref[gi])

    def sel_update(gi, kt, s):
        k0 = pl.multiple_of(kt * kt_n, kt_n)
        m_old = m_ref[gi]
        m_new = jnp.maximum(m_old, jnp.max(s, axis=0, keepdims=True))
        alpha = jnp.exp2(m_old - m_new)
        p = jnp.exp2(s - m_new)
        l_ref[gi] = alpha * l_ref[gi] + jnp.sum(p, axis=0, keepdims=True)
        acc_ref[gi] = alpha * acc_ref[gi] + _dot(vs_ref[grp(gi), pl.ds(k0, kt_n)], p.astype(BF16))
        m_ref[gi] = m_new

    for gi in groups:
        sa_ref[gi] = sel_scores(gi, 0)

    for gi in groups:
        e = jnp.exp2(s_w[gi] - jnp.max(s_w[gi], axis=0, keepdims=True))
        p = e * (1.0 / jnp.sum(e, axis=0, keepdims=True))
        o_w = _dot(vw_ref[grp(gi), pl.ds(w0, n_win)], p.astype(BF16))
        for r in range(hpg):
            out_ref[gi, :, cols(r)] = out_ref[gi, :, cols(r)] + gates[gi][3 * r + 2:3 * r + 3, :] * o_w[:, cols(r)]

    def sel_step(pair, carry):
        for gi in groups:
            sb_ref[gi] = sel_scores(gi, 2 * pair + 1)
        for gi in groups:
            sel_update(gi, 2 * pair, sa_ref[gi])
        for gi in groups:
            sa_ref[gi] = sel_scores(gi, 2 * pair + 2)
        for gi in groups:
            sel_update(gi, 2 * pair + 1, sb_ref[gi])
        return carry

    lax.fori_loop(0, n_tiles // 2, sel_step, 0)

    @pl.when(n_tiles % 2 == 1)
    def _():
        for gi in groups:
            sel_update(gi, n_tiles - 1, sa_ref[gi])

    for gi in groups:
        o_s = acc_ref[gi] * (1.0 / l_ref[gi])
        for r in range(hpg):
            o_r = out_ref[gi, :, cols(r)] + gates[gi][3 * r + 1:3 * r + 2, :] * o_s[:, cols(r)]
            o_ref[:, head(gi, r)] = o_r.T.astype(o_ref.dtype)


def nsa_attention(qk, v_sw_t, k_cmp, v_cmp_t, proj, c2s_t, batch, seq, qdim, kvdim, gate_col):
    m = qk.shape[0]
    d = NSA_HEAD_DIM
    groups = kvdim // d
    gq = qdim // groups
    hpg = gq // d
    nq = seq // Q_BLOCK
    n_sel = seq // SEL_LEN
    gps = NSA_GROUPS_PER_STEP
    assert n_sel % 8 == 0 and seq % SEL_KEY_TILE == 0 and seq >= WINDOW + Q_BLOCK and groups % gps == 0
    n_cmp_pad = k_cmp.shape[2]
    n_sel_pad = c2s_t.shape[0]
    assert n_sel_pad == d
    block_onehot = jnp.asarray(np.arange(seq)[:, None] // SEL_LEN == np.arange(n_sel_pad)[None, :], BF16)
    k_blk = lambda off: pl.BlockSpec((seq, gps * d), lambda b, g, t: (b, off // (gps * d) + g))
    v_blk = lambda off: pl.BlockSpec((gps * d, seq), lambda b, g, t: (off // (gps * d) + g, b))
    lanes_q = hpg * Q_BLOCK
    return pl.pallas_call(
        functools.partial(_nsa_attn_kernel, n_sel),
        grid=(batch, groups // gps, nq),
        in_specs=[
            pl.BlockSpec((Q_BLOCK, gps * gq), lambda b, g, t: (b * nq + t, g)),
            pl.BlockSpec((None, gps, n_cmp_pad, d), lambda b, g, t: (b, g, 0, 0)),
            pl.BlockSpec((None, gps, d, n_cmp_pad), lambda b, g, t: (b, g, 0, 0)),
            k_blk(qdim), v_blk(0), k_blk(qdim + kvdim), v_blk(kvdim),
            pl.BlockSpec((Q_BLOCK, gps * LANES), lambda b, g, t: (b * nq + t, gate_col // (gps * LANES) + g)),
            pl.BlockSpec(c2s_t.shape, lambda b, g, t: (0, 0)),
            pl.BlockSpec(block_onehot.shape, lambda b, g, t: (0, 0)),
        ],
        out_specs=pl.BlockSpec((Q_BLOCK, gps * gq), lambda b, g, t: (b * nq + t, g)),
        out_shape=jax.ShapeDtypeStruct((m, qdim), BF16),
        scratch_shapes=[
            pltpu.VMEM((gps, n_sel_pad, Q_BLOCK), F32),
            pltpu.VMEM((gps, 1, lanes_q), F32),
            pltpu.VMEM((gps, 1, lanes_q), F32),
            pltpu.VMEM((gps, d, lanes_q), F32),
            pltpu.VMEM((gps, d, lanes_q), F32),
            pltpu.VMEM((gps, lanes_q, 2 * d), BF16),
            pltpu.VMEM((gps, SEL_KEY_TILE, lanes_q), F32),
            pltpu.VMEM((gps, SEL_KEY_TILE, lanes_q), F32),
        ],
        compiler_params=_cparams("parallel", "parallel", "arbitrary"),
        name="nsa_attention",
    )(qk, k_cmp, v_cmp_t, qk, v_sw_t, qk, v_sw_t, proj, c2s_t, block_onehot)


def _cmp_to_sel(seq):
    n_cmp = (seq - CMP_LEN) // CMP_STRIDE + 1
    n_sel = seq // SEL_LEN
    c_start = np.arange(n_cmp)[:, None] * CMP_STRIDE
    s_start = np.arange(n_sel)[None, :] * SEL_LEN
    overlap = np.clip(np.minimum(c_start + CMP_LEN, s_start + SEL_LEN) - np.maximum(c_start, s_start), 0, None)
    out = np.zeros((max(n_sel, LANES), seq // CMP_STRIDE), np.float32)
    out[:n_sel, :n_cmp] = (overlap / CMP_STRIDE).T
    return jnp.asarray(out, BF16)


def ssd_mixer(h, nw, batch, seq, in_proj, conv_w, conv_b, dt_bias, a_log, d_skip, norm_w, out_proj, layer=None):
    used = in_proj.shape[-1]
    tn = 1536
    w = jnp.pad(in_proj, [(0, 0)] * (in_proj.ndim - 1) + [(0, -(-used // tn) * tn - used)]).astype(BF16)
    zxd = norm_matmul(h, nw, w, F32, tn=tn, layer=layer)
    y = ssd_core(zxd, batch, seq, conv_w, conv_b, dt_bias, a_log, d_skip, norm_w)
    return matmul_residual(y, out_proj.astype(BF16), h, layer=layer)


def rglru_mixer(h, nw, batch, seq, in_proj, conv_w, conv_b, wa, ba, wx, bx, a_param, out_proj):
    gx = norm_matmul(h, nw, in_proj.astype(BF16), F32, tn=1024)
    y = lru_core(gx, batch, seq, conv_w, conv_b, wa, ba, wx, bx, a_param)
    return matmul_residual(y, out_proj.astype(BF16), h, tm=512, tn=out_proj.shape[1])


def nsa_mixer(h, nw, positions, batch, seq, in_proj, cmp_pe, cmp_w1, cmp_w2, out_proj):
    d = NSA_HEAD_DIM
    groups = NSA_KV_GROUPS
    kvdim = groups * d
    qdim = out_proj.shape[0]
    n_heads = qdim // d
    hpg = n_heads // groups
    col = lambda k: qdim + k * kvdim
    gate_col = 2 * kvdim

    tables = tuple(t.reshape(batch * seq, d) for t in _rope_tables(positions))
    w_rot = jnp.concatenate([in_proj[:, :qdim], in_proj[:, col(2):col(3)], in_proj[:, col(4):col(5)]], axis=1)
    col_scale = jnp.concatenate([jnp.full((qdim,), d ** -0.5 * LOG2_E, F32), jnp.ones((2 * kvdim,), F32)])
    qk = norm_matmul_rope(h, nw, w_rot.astype(BF16), tables, col_scale)

    wg = in_proj[:, col(6):].reshape(-1, groups, 3 * hpg)
    wg = jnp.pad(wg, ((0, 0), (0, 0), (0, LANES - 3 * hpg))).reshape(-1, groups * LANES)
    w = jnp.concatenate([in_proj[:, col(0):col(2)], wg], axis=1)
    proj = norm_matmul(h, nw, w.astype(BF16), F32, tn=w.shape[1])
    w_v_t = jnp.concatenate([in_proj[:, col(3):col(4)], in_proj[:, col(5):col(6)]], axis=1).T.astype(BF16)
    v_sw_t = norm_matmul_t(h, nw, w_v_t, BF16)

    n_rows = seq // CMP_STRIDE
    cmp_end = jnp.minimum(jnp.arange(n_rows) * CMP_STRIDE + CMP_LEN - 1, seq - 1)
    cmp_tables = _rope_tables(positions[:, cmp_end])

    k_cmp = nsa_compress(proj, 0, batch, seq, groups, cmp_pe[0], cmp_w1[0], cmp_w2[0], cmp_tables, True)
    v_cmp_t = nsa_compress(proj, kvdim, batch, seq, groups, cmp_pe[1], cmp_w1[1], cmp_w2[1], cmp_tables, False)

    o = nsa_attention(qk, v_sw_t, k_cmp, v_cmp_t, proj, _cmp_to_sel(seq), batch, seq, qdim, kvdim, gate_col)
    return matmul_residual(o, out_proj.astype(BF16), h, tm=512, tn=out_proj.shape[1])


def kernel(x, p, positions, norm_mix, norm_ffn, norm_ple, w_ple_up, w_ple_gate, w_ffn_in, w_ffn_out, norm_final, ssd_in_proj, ssd_conv_w, ssd_conv_b, ssd_dt_bias, ssd_a_log, ssd_d, ssd_norm, ssd_out_proj, lru_in_proj, lru_conv_w, lru_conv_b, lru_wa, lru_ba, lru_wx, lru_bx, lru_a_param, lru_out_proj, nsa_in_proj, nsa_cmp_pe, nsa_cmp_w1, nsa_cmp_w2, nsa_out_proj):
    batch, seq, d_model = x.shape
    depth = norm_mix.shape[0]
    n_mixers = 3
    m = batch * seq
    h = x.reshape(m, d_model)
    for i in range(depth):
        kind, j = i % n_mixers, i // n_mixers
        if kind == 0:
            h = ssd_mixer(h, norm_mix[i], batch, seq, ssd_in_proj, ssd_conv_w[j], ssd_conv_b[j], ssd_dt_bias[j],
                          ssd_a_log[j], ssd_d[j], ssd_norm[j], ssd_out_proj, layer=j)
        elif kind == 1:
            h = rglru_mixer(h, norm_mix[i], batch, seq, lru_in_proj[j], lru_conv_w[j], lru_conv_b[j], lru_wa[j],
                            lru_ba[j], lru_wx[j], lru_bx[j], lru_a_param[j], lru_out_proj[j])
        else:
            h = nsa_mixer(h, norm_mix[i], positions, batch, seq, nsa_in_proj[j], nsa_cmp_pe[j], nsa_cmp_w1[j],
                          nsa_cmp_w2[j], nsa_out_proj[j])
        hidden = norm_swiglu_in(h, norm_ffn[i], w_ffn_in.astype(BF16), layer=i)
        h = matmul_residual(hidden, w_ffn_out.astype(BF16), h, layer=i)
        h = ple_layer(h, norm_ple[i], w_ple_gate.astype(BF16), p.reshape(depth, m, -1), w_ple_up.astype(BF16),
                      layer=i, final_nw=norm_final if i == depth - 1 else None)
    return h.reshape(batch, seq, d_model)
```

```python
import functools

import numpy as np
import jax
import jax.numpy as jnp
from jax import lax
from jax.experimental import pallas as pl
from jax.experimental.pallas import tpu as pltpu

F32 = jnp.float32
BF16 = jnp.bfloat16

RMS_EPS = 1e-6
ROPE_THETA = 500000.0
NEG_INF = -1e30
FORCED_SCORE = 1e9
LOG2_E = 1.4426950408889634

VMEM_LIMIT_BYTES = 52 * 1024 * 1024
LANES = 128

SSD_HEAD_DIM = 64
SSD_GROUPS = 8
SSD_STATE = 128
SSD_CHUNK = 128

LRU_BLOCK_DIM = 256
LRU_C = 8.0
LRU_ROWS = 256

NSA_HEAD_DIM = 128
NSA_KV_GROUPS = 4
ROT_DIM = NSA_HEAD_DIM // 4
CMP_LEN = 32
CMP_STRIDE = 16
SEL_LEN = 64
SEL_TOPK = 16
WINDOW = 512
Q_BLOCK = 128
SEL_KEY_TILE = 512
NSA_GROUPS_PER_STEP = 2

CONV_HIST = 8


def _cparams(*sem):
    return pltpu.CompilerParams(dimension_semantics=sem, vmem_limit_bytes=VMEM_LIMIT_BYTES)


def _layer_spec(block, index_map, layer):
    if layer is None:
        return pl.BlockSpec(block, index_map)
    return pl.BlockSpec((None,) + block, lambda *g: (layer,) + index_map(*g))


def _dot(a, b):
    return jnp.dot(a, b, preferred_element_type=F32)


def _dot_nt(a, b):
    return lax.dot_general(a, b, (((1,), (1,)), ((), ())), preferred_element_type=F32)


def _split3(x):
    hi = x.astype(BF16)
    r1 = x - hi.astype(F32)
    mid = r1.astype(BF16)
    lo = (r1 - mid.astype(F32)).astype(BF16)
    return hi, mid, lo


def _dot_exact_lhs(e, x):
    hi, mid, lo = _split3(x)
    return _dot(e, hi) + _dot(e, mid) + _dot(e, lo)


def _rms_scale(x):
    return lax.rsqrt(jnp.mean(x * x, axis=-1, keepdims=True) + RMS_EPS)


def _softplus(x):
    return jnp.maximum(x, 0.0) + jnp.log(1.0 + jnp.exp(-jnp.abs(x)))


def _sigmoid(x):
    return 0.5 + 0.5 * jnp.tanh(0.5 * x)


def _silu(x):
    h = 0.5 * x
    return h + h * jnp.tanh(h)


def _norm_mm_kernel(x_ref, nw_ref, w_ref, o_ref, u_ref):
    @pl.when(pl.program_id(1) == 0)
    def _():
        rows = x_ref.shape[0] // 4
        for lo in range(0, x_ref.shape[0], rows):
            sl = slice(lo, lo + rows)
            x = x_ref[sl, :]
            u = (x * _rms_scale(x) * nw_ref[...]).astype(BF16)
            u_ref[sl, :] = u
            o_ref[sl, :] = _dot(u, w_ref[...]).astype(o_ref.dtype)

    @pl.when(pl.program_id(1) != 0)
    def _():
        o_ref[...] = _dot(u_ref[...], w_ref[...]).astype(o_ref.dtype)


def norm_matmul(x, nw, w, out_dtype, tm=1024, tn=512, layer=None):
    m, k = x.shape
    n = w.shape[-1]
    return pl.pallas_call(
        _norm_mm_kernel,
        grid=(m // tm, n // tn),
        in_specs=[
            pl.BlockSpec((tm, k), lambda i, j: (i, 0)),
            pl.BlockSpec((1, k), lambda i, j: (0, 0)),
            _layer_spec((k, tn), lambda i, j: (0, j), layer),
        ],
        out_specs=pl.BlockSpec((tm, tn), lambda i, j: (i, j)),
        out_shape=jax.ShapeDtypeStruct((m, n), out_dtype),
        scratch_shapes=[pltpu.VMEM((tm, k), BF16)],
        compiler_params=_cparams("parallel", "arbitrary"),
        name="norm_matmul",
    )(x, nw.reshape(1, k), w)


def _norm_mm_t_kernel(x_ref, nw_ref, wt_ref, o_ref):
    x = x_ref[...]
    u = (x * _rms_scale(x) * nw_ref[...]).astype(BF16)
    o_ref[...] = _dot_nt(wt_ref[...], u).astype(o_ref.dtype)


def norm_matmul_t(x, nw, w_t, out_dtype, tm=1024):
    m, k = x.shape
    n = w_t.shape[0]
    return pl.pallas_call(
        _norm_mm_t_kernel,
        grid=(m // tm,),
        in_specs=[
            pl.BlockSpec((tm, k), lambda i: (i, 0)),
            pl.BlockSpec((1, k), lambda i: (0, 0)),
            pl.BlockSpec((n, k), lambda i: (0, 0)),
        ],
        out_specs=pl.BlockSpec((n, tm), lambda i: (0, i)),
        out_shape=jax.ShapeDtypeStruct((n, m), out_dtype),
        compiler_params=_cparams("parallel"),
        name="norm_matmul_t",
    )(x, nw.reshape(1, k), w_t)


def _norm_swiglu_kernel(x_ref, nw_ref, wg_ref, wu_ref, o_ref, u_ref):
    def gated(u):
        return (_silu(_dot(u, wg_ref[...])) * _dot(u, wu_ref[...])).astype(o_ref.dtype)

    @pl.when(pl.program_id(1) == 0)
    def _():
        rows = x_ref.shape[0] // 4
        for lo in range(0, x_ref.shape[0], rows):
            sl = slice(lo, lo + rows)
            x = x_ref[sl, :]
            u = (x * _rms_scale(x) * nw_ref[...]).astype(BF16)
            u_ref[sl, :] = u
            o_ref[sl, :] = gated(u)

    @pl.when(pl.program_id(1) != 0)
    def _():
        o_ref[...] = gated(u_ref[...])


def norm_swiglu_in(x, nw, w_in, tm=1024, tn=512, layer=None):
    m, k = x.shape
    f = w_in.shape[-1] // 2
    nj = f // tn
    return pl.pallas_call(
        _norm_swiglu_kernel,
        grid=(m // tm, nj),
        in_specs=[
            pl.BlockSpec((tm, k), lambda i, j: (i, 0)),
            pl.BlockSpec((1, k), lambda i, j: (0, 0)),
            _layer_spec((k, tn), lambda i, j: (0, j), layer),
            _layer_spec((k, tn), lambda i, j: (0, j + nj), layer),
        ],
        out_specs=pl.BlockSpec((tm, tn), lambda i, j: (i, j)),
        out_shape=jax.ShapeDtypeStruct((m, f), BF16),
        scratch_shapes=[pltpu.VMEM((tm, k), BF16)],
        compiler_params=_cparams("parallel", "arbitrary"),
        name="norm_swiglu_in",
    )(x, nw.reshape(1, k), w_in, w_in)


def _mm_residual_kernel(x_ref, w_ref, r_ref, o_ref):
    o_ref[...] = r_ref[...] + _dot(x_ref[...], w_ref[...])


def matmul_residual(x, w, res, tm=1024, tn=512, layer=None):
    m, k = x.shape
    n = w.shape[-1]
    return pl.pallas_call(
        _mm_residual_kernel,
        grid=(m // tm, n // tn),
        in_specs=[
            pl.BlockSpec((tm, k), lambda i, j: (i, 0)),
            _layer_spec((k, tn), lambda i, j: (0, j), layer),
            pl.BlockSpec((tm, tn), lambda i, j: (i, j)),
        ],
        out_specs=pl.BlockSpec((tm, tn), lambda i, j: (i, j)),
        out_shape=jax.ShapeDtypeStruct((m, n), F32),
        compiler_params=_cparams("parallel", "arbitrary"),
        name="matmul_residual",
    )(x, w, res)


def _ple_kernel(final, x_ref, nw_ref, wg_ref, p_ref, wu_ref, fw_ref, o_ref):
    x = x_ref[...]
    u = (x * _rms_scale(x) * nw_ref[...]).astype(BF16)
    y = x + _sigmoid(_dot(u, wg_ref[...])) * _dot(p_ref[...].astype(BF16), wu_ref[...])
    o_ref[...] = y * _rms_scale(y) * fw_ref[...] if final else y


def ple_layer(x, nw, w_gate, p, w_up, layer=None, final_nw=None, tm=512):
    m, k = x.shape
    n = w_gate.shape[-1]
    kp = p.shape[-1]
    final = final_nw is not None
    fw = final_nw.reshape(1, n) if final else jnp.ones((1, n), F32)
    return pl.pallas_call(
        functools.partial(_ple_kernel, final),
        grid=(m // tm,),
        in_specs=[
            pl.BlockSpec((tm, k), lambda i: (i, 0)),
            pl.BlockSpec((1, k), lambda i: (0, 0)),
            _layer_spec((k, n), lambda i: (0, 0), layer),
            _layer_spec((tm, kp), lambda i: (i, 0), layer),
            _layer_spec((kp, n), lambda i: (0, 0), layer),
            pl.BlockSpec((1, n), lambda i: (0, 0)),
        ],
        out_specs=pl.BlockSpec((tm, n), lambda i: (i, 0)),
        out_shape=jax.ShapeDtypeStruct((m, n), F32),
        compiler_params=_cparams("parallel"),
        name="ple_layer",
    )(x, nw.reshape(1, k), w_gate, p, w_up, fw)


def _causal_conv(pad_ref, w_ref, b_ref, rows, lo, width):
    taps = w_ref.shape[0]
    x = pad_ref[0:CONV_HIST + rows, lo:lo + width]
    acc = w_ref[0:1, lo:lo + width] * x
    for k in range(1, taps):
        acc = pltpu.roll(acc, 1, axis=0) + w_ref[k:k + 1, lo:lo + width] * x
    return acc[CONV_HIST:CONV_HIST + rows, :] + b_ref[:, lo:lo + width]


def _ssd_kernel(z_ref, xs_ref, bc_ref, dt_ref, cwx_ref, cbx_ref, cwbc_ref, cbbc_ref, dtb_ref, alog_ref,
                dskip_ref, nw_ref, tril_ref, ehead_ref, o_ref,
                xpad_ref, bcpad_ref, bcact_ref, state_ref, acs_ref, acst_ref, xdt_ref, ydiag_ref):
    q = SSD_CHUNK
    gw = xs_ref.shape[1] // SSD_GROUPS
    heads_per_group = gw // SSD_HEAD_DIM
    n_bc = bc_ref.shape[1] // 2

    @pl.when(pl.program_id(1) == 0)
    def _():
        xpad_ref[0:CONV_HIST, :] = jnp.zeros((CONV_HIST, xpad_ref.shape[1]), F32)
        bcpad_ref[0:CONV_HIST, :] = jnp.zeros((CONV_HIST, bcpad_ref.shape[1]), F32)
        state_ref[...] = jnp.zeros(state_ref.shape, F32)

    xpad_ref[CONV_HIST:CONV_HIST + q, :] = xs_ref[...]
    bcpad_ref[CONV_HIST:CONV_HIST + q, :] = bc_ref[...]

    for lo in range(0, bc_ref.shape[1], 512):
        v = _causal_conv(bcpad_ref, cwbc_ref, cbbc_ref, q, lo, 512)
        bcact_ref[:, lo:lo + 512] = _silu(v)

    dt = _softplus(dt_ref[...] + dtb_ref[...])
    adt = dt * (-jnp.exp(alog_ref[...]))
    acs = _dot_exact_lhs(tril_ref[...], adt) * LOG2_E
    acs_ref[...] = acs
    acst_ref[...] = acs.T
    dt_parts = _split3(dt)
    acs_parts = _split3(acs)

    li = lax.broadcasted_iota(jnp.int32, (q, q), 0)
    si = lax.broadcasted_iota(jnp.int32, (q, q), 1)
    causal = li >= si

    for g in range(SSD_GROUPS):
        glo = g * gw
        e_g = ehead_ref[:, glo:glo + gw]
        expand = lambda parts: _dot(parts[0], e_g) + _dot(parts[1], e_g) + _dot(parts[2], e_g)
        xc = _causal_conv(xpad_ref, cwx_ref, cbx_ref, q, glo, gw)
        xs = _silu(xc)
        dt_g = expand(dt_parts)
        a_g = expand(acs_parts)
        a_last = a_g[q - 1:q, :]
        xdt = xs * dt_g
        xdt_ref[...] = xdt.astype(BF16)

        bm = bcact_ref[:, g * SSD_STATE:(g + 1) * SSD_STATE]
        cm = bcact_ref[:, n_bc + g * SSD_STATE:n_bc + (g + 1) * SSD_STATE]
        bm16 = bm.astype(BF16)
        cm16 = cm.astype(BF16)
        cb = _dot_nt(cm16, bm16)

        prev = state_ref[:, glo:glo + gw]
        y_off = _dot(cm16, prev.astype(BF16)) * jnp.exp2(a_g)
        st_new = _dot(bm.T.astype(BF16), (xdt * jnp.exp2(a_last - a_g)).astype(BF16))
        state_ref[:, glo:glo + gw] = prev * jnp.exp2(a_last) + st_new

        for r in range(heads_per_group):
            h = g * heads_per_group + r
            seg = acs_ref[:, h:h + 1] - acst_ref[h:h + 1, :]
            decay = jnp.exp2(jnp.where(causal, seg, NEG_INF))
            m_h = (cb * decay).astype(BF16)
            ydiag_ref[:, r * SSD_HEAD_DIM:(r + 1) * SSD_HEAD_DIM] = _dot(
                m_h, xdt_ref[:, r * SSD_HEAD_DIM:(r + 1) * SSD_HEAD_DIM])

        y = ydiag_ref[...] + y_off + dskip_ref[:, glo:glo + gw] * xs
        zg = z_ref[:, glo:glo + gw]
        y = y * _silu(zg)
        y = y * _rms_scale(y) * nw_ref[:, glo:glo + gw]
        o_ref[:, glo:glo + gw] = y.astype(o_ref.dtype)

    xpad_ref[0:CONV_HIST, :] = xpad_ref[q:q + CONV_HIST, :]
    bcpad_ref[0:CONV_HIST, :] = bcpad_ref[q:q + CONV_HIST, :]


def ssd_core(zxd, batch, seq, conv_w, conv_b, dt_bias, a_log, d_skip, norm_w):
    m = zxd.shape[0]
    n_heads = dt_bias.shape[0]
    inner = n_heads * SSD_HEAD_DIM
    n_bc = SSD_GROUPS * SSD_STATE
    q = SSD_CHUNK
    nc = seq // q
    assert inner % (2 * n_bc) == 0 and n_heads <= LANES
    pad_h = LANES - n_heads

    row = lambda v: v.reshape(1, -1).astype(F32)
    padh = lambda v: jnp.pad(v.astype(F32), (0, pad_h)).reshape(1, LANES)
    tril = jnp.asarray(np.tril(np.ones((q, q), np.float32)), BF16)
    ehead = jnp.asarray(np.repeat(np.eye(LANES, n_heads, dtype=np.float32), SSD_HEAD_DIM, axis=1), BF16)

    rows = lambda b, c: b * nc + c
    full = lambda a: pl.BlockSpec(a.shape, lambda b, c: (0,) * a.ndim)
    consts = [conv_w[:, :inner].astype(F32), row(conv_b[:inner]), conv_w[:, inner:].astype(F32), row(conv_b[inner:]),
              padh(dt_bias), padh(a_log), row(jnp.repeat(d_skip, SSD_HEAD_DIM)), row(norm_w), tril, ehead]
    return pl.pallas_call(
        _ssd_kernel,
        grid=(batch, nc),
        in_specs=[
            pl.BlockSpec((q, inner), lambda b, c: (rows(b, c), 0)),
            pl.BlockSpec((q, inner), lambda b, c: (rows(b, c), 1)),
            pl.BlockSpec((q, 2 * n_bc), lambda b, c: (rows(b, c), 2 * inner // (2 * n_bc))),
            pl.BlockSpec((q, LANES), lambda b, c: (rows(b, c), (2 * inner + 2 * n_bc) // LANES)),
        ] + [full(a) for a in consts],
        out_specs=pl.BlockSpec((q, inner), lambda b, c: (rows(b, c), 0)),
        out_shape=jax.ShapeDtypeStruct((m, inner), BF16),
        scratch_shapes=[
            pltpu.VMEM((q + CONV_HIST, inner), F32),
            pltpu.VMEM((q + CONV_HIST, 2 * n_bc), F32),
            pltpu.VMEM((q, 2 * n_bc), F32),
            pltpu.VMEM((SSD_STATE, inner), F32),
            pltpu.VMEM((q, LANES), F32),
            pltpu.VMEM((LANES, q), F32),
            pltpu.VMEM((q, inner // SSD_GROUPS), BF16),
            pltpu.VMEM((q, inner // SSD_GROUPS), F32),
        ],
        compiler_params=_cparams("parallel", "arbitrary"),
        name="ssd_core",
    )(zxd, zxd, zxd, zxd, *consts)


def _lru_kernel(gate_ref, xr_ref, cw_ref, cb_ref, wa_ref, ba_ref, wx_ref, bx_ref, ap_ref, o_ref,
                xpad_ref, a_ref, b_ref, h_ref, carry_ref):
    rows = gate_ref.shape[0]
    width = gate_ref.shape[1]

    @pl.when(pl.program_id(1) == 0)
    def _():
        xpad_ref[0:CONV_HIST, :] = jnp.zeros((CONV_HIST, width), F32)
        carry_ref[...] = jnp.zeros(carry_ref.shape, F32)

    xpad_ref[CONV_HIST:CONV_HIST + rows, :] = xr_ref[...]
    for k in range(width // LRU_BLOCK_DIM):
        lo = k * LRU_BLOCK_DIM
        sl = slice(lo, lo + LRU_BLOCK_DIM)
        xc = _causal_conv(xpad_ref, cw_ref, cb_ref, rows, lo, LRU_BLOCK_DIM)
        x16 = xc.astype(BF16)
        r_t = _sigmoid(_dot(x16, wa_ref[k]) + ba_ref[:, sl])
        i_t = _sigmoid(_dot(x16, wx_ref[k]) + bx_ref[:, sl])
        log_a = -LRU_C * r_t * _softplus(-ap_ref[:, sl])
        a_t = jnp.exp(log_a)
        a_ref[:, sl] = a_t
        b_ref[:, sl] = jnp.sqrt(1.0 - a_t * a_t) * (i_t * xc)
    xpad_ref[0:CONV_HIST, :] = xpad_ref[rows:rows + CONV_HIST, :]

    def step(t, h):
        h = a_ref[pl.ds(t, 1), :] * h + b_ref[pl.ds(t, 1), :]
        h_ref[pl.ds(t, 1), :] = h
        return h

    carry_ref[0:1, :] = lax.fori_loop(0, rows, step, carry_ref[0:1, :], unroll=8)
    o_ref[...] = (jax.nn.gelu(gate_ref[...], approximate=True) * h_ref[...]).astype(o_ref.dtype)


def lru_core(gx, batch, seq, conv_w, conv_b, wa, ba, wx, bx, a_param):
    m = gx.shape[0]
    width = gx.shape[1] // 2
    rows = LRU_ROWS
    nt = seq // rows
    row = lambda v: v.reshape(1, -1).astype(F32)
    consts = [conv_w.astype(F32), row(conv_b), wa.astype(BF16), row(ba), wx.astype(BF16), row(bx), row(a_param)]
    full = lambda a: pl.BlockSpec(a.shape, lambda b, t: (0,) * a.ndim)
    return pl.pallas_call(
        _lru_kernel,
        grid=(batch, nt),
        in_specs=[
            pl.BlockSpec((rows, width), lambda b, t: (b * nt + t, 0)),
            pl.BlockSpec((rows, width), lambda b, t: (b * nt + t, 1)),
        ] + [full(a) for a in consts],
        out_specs=pl.BlockSpec((rows, width), lambda b, t: (b * nt + t, 0)),
        out_shape=jax.ShapeDtypeStruct((m, width), BF16),
        scratch_shapes=[
            pltpu.VMEM((rows + CONV_HIST, width), F32),
            pltpu.VMEM((rows, width), F32),
            pltpu.VMEM((rows, width), F32),
            pltpu.VMEM((rows, width), F32),
            pltpu.VMEM((8, width), F32),
        ],
        compiler_params=_cparams("parallel", "arbitrary"),
        name="lru_core",
    )(gx, gx, *consts)


def _rope_tables(pos):
    half = ROT_DIM // 2
    inv = ROPE_THETA ** (-jnp.arange(half, dtype=F32) * 2.0 / ROT_DIM)
    ang = pos.astype(F32)[..., None] * inv
    cos, sin = jnp.cos(ang), jnp.sin(ang)
    shape = pos.shape + (NSA_HEAD_DIM - ROT_DIM,)
    zeros_h = jnp.zeros(pos.shape + (half,), F32)
    cos_t = jnp.concatenate([cos, cos, jnp.ones(shape, F32)], axis=-1)
    sin_lo = jnp.concatenate([-sin, zeros_h, jnp.zeros(shape, F32)], axis=-1)
    sin_hi = jnp.concatenate([zeros_h, sin, jnp.zeros(shape, F32)], axis=-1)
    return cos_t, sin_lo, sin_hi


def _apply_rope(x, cos_t, sin_lo, sin_hi):
    n = x.shape[1] // NSA_HEAD_DIM
    half = ROT_DIM // 2
    tile = lambda t: jnp.concatenate([t] * n, axis=1) if n > 1 else t
    up = pltpu.roll(x, x.shape[1] - half, axis=1)
    down = pltpu.roll(x, half, axis=1)
    return x * tile(cos_t) + up * tile(sin_lo) + down * tile(sin_hi)


def _norm_mm_rope_kernel(x_ref, nw_ref, w_ref, cos_ref, slo_ref, shi_ref, cs_ref, o_ref, u_ref):
    rows = u_ref.shape[0] // 4
    chunks = [slice(lo, lo + rows) for lo in range(0, u_ref.shape[0], rows)]

    def project(sl, u):
        y = _apply_rope(_dot(u, w_ref[...]), cos_ref[sl, :], slo_ref[sl, :], shi_ref[sl, :])
        o_ref[sl, :] = (y * cs_ref[...]).astype(o_ref.dtype)

    @pl.when(pl.program_id(1) == 0)
    def _():
        for sl in chunks:
            x = x_ref[sl, :]
            u = (x * _rms_scale(x) * nw_ref[...]).astype(BF16)
            u_ref[sl, :] = u
            project(sl, u)

    @pl.when(pl.program_id(1) != 0)
    def _():
        for sl in chunks:
            project(sl, u_ref[sl, :])


def norm_matmul_rope(x, nw, w, tables, col_scale, tm=1024, tn=768):
    m, k = x.shape
    n = w.shape[1]
    tspec = pl.BlockSpec((tm, NSA_HEAD_DIM), lambda i, j: (i, 0))
    return pl.pallas_call(
        _norm_mm_rope_kernel,
        grid=(m // tm, n // tn),
        in_specs=[
            pl.BlockSpec((tm, k), lambda i, j: (i, 0)),
            pl.BlockSpec((1, k), lambda i, j: (0, 0)),
            pl.BlockSpec((k, tn), lambda i, j: (0, j)),
            tspec, tspec, tspec,
            pl.BlockSpec((1, tn), lambda i, j: (0, j)),
        ],
        out_specs=pl.BlockSpec((tm, tn), lambda i, j: (i, j)),
        out_shape=jax.ShapeDtypeStruct((m, n), BF16),
        scratch_shapes=[pltpu.VMEM((tm, k), BF16)],
        compiler_params=_cparams("parallel", "arbitrary"),
        name="norm_matmul_rope",
    )(x, nw.reshape(1, k), w, *tables, col_scale.reshape(1, n))


def _compress_kernel(is_key, tok_ref, pe_ref, w1_ref, w2_ref, cos_ref, slo_ref, shi_ref, o_ref):
    d = w2_ref.shape[0]
    nrows = tok_ref.shape[0] // CMP_STRIDE
    top = jnp.zeros((nrows, d), F32)
    bot = jnp.zeros((nrows, d), F32)
    for pos in range(CMP_STRIDE):
        tok = tok_ref[pl.ds(pos, nrows, stride=CMP_STRIDE), :]
        lo, hi = pos * d, (CMP_STRIDE + pos) * d
        top = top + _dot((tok + pe_ref[pos:pos + 1, :]).astype(BF16), w1_ref[lo:lo + d, :])
        bot = bot + _dot((tok + pe_ref[CMP_STRIDE + pos:CMP_STRIDE + pos + 1, :]).astype(BF16), w1_ref[hi:hi + d, :])
    pre = top + pltpu.roll(bot, nrows - 1, axis=0)
    out = _dot(jax.nn.gelu(pre, approximate=True).astype(BF16), w2_ref[...])
    if is_key:
        o_ref[...] = _apply_rope(out, cos_ref[...], slo_ref[...], shi_ref[...]).astype(o_ref.dtype)
    else:
        o_ref[...] = out.T.astype(o_ref.dtype)


def nsa_compress(proj, col, batch, seq, groups, pe, w1, w2, tables, is_key):
    d = w2.shape[0]
    nrows = seq // CMP_STRIDE
    assert CMP_LEN == 2 * CMP_STRIDE and col % d == 0
    tspec = pl.BlockSpec((None, nrows, d), lambda i, j: (i, 0, 0))
    const = lambda a: pl.BlockSpec(a.shape, lambda i, j: (0,) * a.ndim)
    out_block = (None, None, nrows, d) if is_key else (None, None, d, nrows)
    out_dims = (batch, groups, nrows, d) if is_key else (batch, groups, d, nrows)
    consts = [pe.astype(F32), w1.astype(BF16), w2.astype(BF16)]
    return pl.pallas_call(
        functools.partial(_compress_kernel, is_key),
        grid=(batch, groups),
        in_specs=[pl.BlockSpec((seq, d), lambda i, j: (i, col // d + j))] + [const(a) for a in consts]
        + [tspec, tspec, tspec],
        out_specs=pl.BlockSpec(out_block, lambda i, j: (i, j, 0, 0)),
        out_shape=jax.ShapeDtypeStruct(out_dims, BF16),
        compiler_params=_cparams("parallel", "arbitrary"),
        name="nsa_compress_k" if is_key else "nsa_compress_v",
    )(proj, *consts, *tables)


def _nsa_attn_kernel(n_sel, q_ref, kcmp_ref, vcmp_ref, ks_ref, vs_ref, kw_ref, vw_ref, gate_ref, c2s_ref, e_ref,
                     o_ref, impt_ref, m_ref, l_ref, acc_ref, out_ref, q4_ref, sa_ref, sb_ref):
    tq = Q_BLOCK
    d = NSA_HEAD_DIM
    gps = kcmp_ref.shape[0]
    hpg = q_ref.shape[1] // (gps * d)
    n_cmp_pad = kcmp_ref.shape[1]
    n_sel_pad = c2s_ref.shape[0]
    groups = range(gps)
    t0 = pl.multiple_of(pl.program_id(2) * tq, tq)
    t_row = t0 + lax.broadcasted_iota(jnp.int32, (1, tq), 1)
    gates = [_sigmoid(gate_ref[:, gi * LANES:(gi + 1) * LANES]).T for gi in groups]
    grp = lambda gi: slice(gi * d, (gi + 1) * d)
    head = lambda gi, r: slice((gi * hpg + r) * d, (gi * hpg + r + 1) * d)
    cols = lambda r: slice(r * tq, (r + 1) * tq)
    per_head = lambda a: jnp.concatenate([a] * hpg, axis=1)
    for gi in groups:
        for r in range(hpg):
            q4_ref[gi, cols(r), 0:d] = q_ref[:, head(gi, r)]
    q4 = [q4_ref[gi, :, 0:d] for gi in groups]

    n_col = lax.broadcasted_iota(jnp.int32, (n_cmp_pad, 1), 0)
    ok_c = (n_col * CMP_STRIDE + (CMP_LEN - 1) <= t_row) & (n_col < n_cmp_pad - 1)
    bias_c = per_head(jnp.where(ok_c, 0.0, NEG_INF))
    keep_c = per_head(jnp.where(ok_c, 1.0, 0.0))
    n_win = WINDOW + tq
    w0 = pl.multiple_of(jnp.maximum(t0 - WINDOW, 0), tq)
    kp = w0 + lax.broadcasted_iota(jnp.int32, (n_win, 1), 0)
    bias_w = per_head(jnp.where((kp <= t_row) & (kp > t_row - WINDOW), 0.0, NEG_INF))
    causal = (lax.broadcasted_iota(jnp.int32, (tq, 1), 0) <= lax.broadcasted_iota(jnp.int32, (1, tq), 1))
    bias_d = per_head(jnp.where(causal, 0.0, NEG_INF))

    s_c = [_dot_nt(kcmp_ref[gi], q4[gi]) + bias_c for gi in groups]
    s_w = [_dot_nt(kw_ref[pl.ds(w0, n_win), grp(gi)], q4[gi]) + bias_w for gi in groups]
    s_d = [_dot_nt(ks_ref[pl.ds(t0, tq), grp(gi)], q4[gi]) + bias_d for gi in groups]

    j_col = lax.broadcasted_iota(jnp.int32, (n_sel_pad, 1), 0)
    cur = t_row // SEL_LEN
    forced = (j_col == 0) | (j_col == cur) | (j_col == cur - 1)
    sub = lax.broadcasted_iota(jnp.int32, (8, 1), 0)
    n_blk = n_sel // 8
    for gi in groups:
        e = jnp.exp2(s_c[gi] - jnp.max(s_c[gi], axis=0, keepdims=True)) * keep_c
        den = jnp.sum(e, axis=0, keepdims=True)
        p = e * jnp.where(den > 0.0, 1.0 / den, 0.0)
        o_c = _dot(vcmp_ref[gi], p.astype(BF16))
        p_sum = p[:, cols(0)]
        for r in range(1, hpg):
            p_sum = p_sum + p[:, cols(r)]
        for r in range(hpg):
            out_ref[gi, :, cols(r)] = gates[gi][3 * r:3 * r + 1, :] * o_c[:, cols(r)]

        imp = _dot_exact_lhs(c2s_ref[...], p_sum)
        imp = jnp.where(forced, FORCED_SCORE, imp)
        impt_ref[gi] = jnp.where(j_col <= cur, imp, -jnp.inf)
        blks = [impt_ref[gi, jb * 8:(jb + 1) * 8, :] for jb in range(n_blk)]
        cnt = [jnp.zeros((8, tq), F32) for _ in range(n_blk)]
        for i in range(n_sel):
            other = impt_ref[gi, i:i + 1, :]
            for jb in range(n_blk):
                if jb * 8 > i:
                    ahead = other >= blks[jb]
                elif jb * 8 + 7 < i:
                    ahead = other > blks[jb]
                else:
                    ahead = (other > blks[jb]) | ((other == blks[jb]) & (sub > i - jb * 8))
                cnt[jb] = cnt[jb] + jnp.where(ahead, 1.0, 0.0)
        sel_t = [jnp.where(c < float(SEL_TOPK), 1.0, 0.0) for c in cnt]
        if n_sel_pad > n_sel:
            sel_t.append(jnp.zeros((n_sel_pad - n_sel, tq), F32))
        sel = jnp.concatenate(sel_t, axis=0)
        sel_bias = jnp.where((sel > 0.5) & (j_col * SEL_LEN < t0), 0.0, NEG_INF).T.astype(BF16)
        for r in range(hpg):
            q4_ref[gi, cols(r), d:2 * d] = sel_bias

        m_0 = jnp.max(s_d[gi], axis=0, keepdims=True)
        p = jnp.exp2(s_d[gi] - m_0)
        m_ref[gi] = m_0
        l_ref[gi] = jnp.sum(p, axis=0, keepdims=True)
        acc_ref[gi] = _dot(vs_ref[grp(gi), pl.ds(t0, tq)], p.astype(BF16))

    kt_n = SEL_KEY_TILE
    last_tile = ks_ref.shape[0] // kt_n - 1
    n_tiles = (t0 + kt_n - 1) // kt_n

    def sel_scores(gi, kt):
        k0 = pl.multiple_of(jnp.minimum(kt, last_tile) * kt_n, kt_n)
        lhs = jnp.concatenate([ks_ref[pl.ds(k0, kt_n), grp(gi)], e_ref[pl.ds(k0, kt_n), :]], axis=1)
        return _dot_nt(lhs, q4_ref[gi])

    def sel_update(gi, kt, s):
        k0 = pl.multiple_of(kt * kt_n, kt_n)
        m_old = m_ref[gi]
        m_new = jnp.maximum(m_old, jnp.max(s, axis=0, keepdims=True))
        alpha = jnp.exp2(m_old - m_new)
        p = jnp.exp2(s - m_new)
        l_ref[gi] = alpha * l_ref[gi] + jnp.sum(p, axis=0, keepdims=True)
        acc_ref[gi] = alpha * acc_ref[gi] + _dot(vs_ref[grp(gi), pl.ds(k0, kt_n)], p.astype(BF16))
        m_ref[gi] = m_new

    for gi in groups:
        sa_ref[gi] = sel_scores(gi, 0)

    for gi in groups:
        e = jnp.exp2(s_w[gi] - jnp.max(s_w[gi], axis=0, keepdims=True))
        p = e * (1.0 / jnp.sum(e, axis=0, keepdims=True))
        o_w = _dot(vw_ref[grp(gi), pl.ds(w0, n_win)], p.astype(BF16))
        for r in range(hpg):
            out_ref[gi, :, cols(r)] = out_ref[gi, :, cols(r)] + gates[gi][3 * r + 2:3 * r + 3, :] * o_w[:, cols(r)]

    def sel_step(pair, carry):
        for gi in groups:
            sb_ref[gi] = sel_scores(gi, 2 * pair + 1)
        for gi in groups:
            sel_update(gi, 2 * pair, sa_ref[gi])
        for gi in groups:
            sa_ref[gi] = sel_scores(gi, 2 * pair + 2)
        for gi in groups:
            sel_update(gi, 2 * pair + 1, sb_ref[gi])
        return carry

    lax.fori_loop(0, n_tiles // 2, sel_step, 0)

    @pl.when(n_tiles % 2 == 1)
    def _():
        for gi in groups:
            sel_update(gi, n_tiles - 1, sa_ref[gi])

    for gi in groups:
        o_s = acc_ref[gi] * (1.0 / l_ref[gi])
        for r in range(hpg):
            o_r = out_ref[gi, :, cols(r)] + gates[gi][3 * r + 1:3 * r + 2, :] * o_s[:, cols(r)]
            o_ref[:, head(gi, r)] = o_r.T.astype(o_ref.dtype)


def nsa_attention(qk, v_sw_t, k_cmp, v_cmp_t, proj, c2s_t, batch, seq, qdim, kvdim, gate_col):
    m = qk.shape[0]
    d = NSA_HEAD_DIM
    groups = kvdim // d
    gq = qdim // groups
    hpg = gq // d
    nq = seq // Q_BLOCK
    n_sel = seq // SEL_LEN
    gps = NSA_GROUPS_PER_STEP
    assert n_sel % 8 == 0 and seq % SEL_KEY_TILE == 0 and seq >= WINDOW + Q_BLOCK and groups % gps == 0
    n_cmp_pad = k_cmp.shape[2]
    n_sel_pad = c2s_t.shape[0]
    assert n_sel_pad == d
    block_onehot = jnp.asarray(np.arange(seq)[:, None] // SEL_LEN == np.arange(n_sel_pad)[None, :], BF16)
    k_blk = lambda off: pl.BlockSpec((seq, gps * d), lambda b, g, t: (b, off // (gps * d) + g))
    v_blk = lambda off: pl.BlockSpec((gps * d, seq), lambda b, g, t: (off // (gps * d) + g, b))
    lanes_q = hpg * Q_BLOCK
    return pl.pallas_call(
        functools.partial(_nsa_attn_kernel, n_sel),
        grid=(batch, groups // gps, nq),
        in_specs=[
            pl.BlockSpec((Q_BLOCK, gps * gq), lambda b, g, t: (b * nq + t, g)),
            pl.BlockSpec((None, gps, n_cmp_pad, d), lambda b, g, t: (b, g, 0, 0)),
            pl.BlockSpec((None, gps, d, n_cmp_pad), lambda b, g, t: (b, g, 0, 0)),
            k_blk(qdim), v_blk(0), k_blk(qdim + kvdim), v_blk(kvdim),
            pl.BlockSpec((Q_BLOCK, gps * LANES), lambda b, g, t: (b * nq + t, gate_col // (gps * LANES) + g)),
            pl.BlockSpec(c2s_t.shape, lambda b, g, t: (0, 0)),
            pl.BlockSpec(block_onehot.shape, lambda b, g, t: (0, 0)),
        ],
        out_specs=pl.BlockSpec((Q_BLOCK, gps * gq), lambda b, g, t: (b * nq + t, g)),
        out_shape=jax.ShapeDtypeStruct((m, qdim), BF16),
        scratch_shapes=[
            pltpu.VMEM((gps, n_sel_pad, Q_BLOCK), F32),
            pltpu.VMEM((gps, 1, lanes_q), F32),
            pltpu.VMEM((gps, 1, lanes_q), F32),
            pltpu.VMEM((gps, d, lanes_q), F32),
            pltpu.VMEM((gps, d, lanes_q), F32),
            pltpu.VMEM((gps, lanes_q, 2 * d), BF16),
            pltpu.VMEM((gps, SEL_KEY_TILE, lanes_q), F32),
            pltpu.VMEM((gps, SEL_KEY_TILE, lanes_q), F32),
        ],
        compiler_params=_cparams("parallel", "parallel", "arbitrary"),
        name="nsa_attention",
    )(qk, k_cmp, v_cmp_t, qk, v_sw_t, qk, v_sw_t, proj, c2s_t, block_onehot)


def _cmp_to_sel(seq):
    n_cmp = (seq - CMP_LEN) // CMP_STRIDE + 1
    n_sel = seq // SEL_LEN
    c_start = np.arange(n_cmp)[:, None] * CMP_STRIDE
    s_start = np.arange(n_sel)[None, :] * SEL_LEN
    overlap = np.clip(np.minimum(c_start + CMP_LEN, s_start + SEL_LEN) - np.maximum(c_start, s_start), 0, None)
    out = np.zeros((max(n_sel, LANES), seq // CMP_STRIDE), np.float32)
    out[:n_sel, :n_cmp] = (overlap / CMP_STRIDE).T
    return jnp.asarray(out, BF16)


def ssd_mixer(h, nw, batch, seq, in_proj, conv_w, conv_b, dt_bias, a_log, d_skip, norm_w, out_proj, layer=None):
    used = in_proj.shape[-1]
    tn = 1536
    w = jnp.pad(in_proj, [(0, 0)] * (in_proj.ndim - 1) + [(0, -(-used // tn) * tn - used)]).astype(BF16)
    zxd = norm_matmul(h, nw, w, F32, tn=tn, layer=layer)
    y = ssd_core(zxd, batch, seq, conv_w, conv_b, dt_bias, a_log, d_skip, norm_w)
    return matmul_residual(y, out_proj.astype(BF16), h, layer=layer)


def rglru_mixer(h, nw, batch, seq, in_proj, conv_w, conv_b, wa, ba, wx, bx, a_param, out_proj):
    gx = norm_matmul(h, nw, in_proj.astype(BF16), F32, tn=1024)
    y = lru_core(gx, batch, seq, conv_w, conv_b, wa, ba, wx, bx, a_param)
    return matmul_residual(y, out_proj.astype(BF16), h, tm=512, tn=out_proj.shape[1])


def nsa_mixer(h, nw, positions, batch, seq, in_proj, cmp_pe, cmp_w1, cmp_w2, out_proj):
    d = NSA_HEAD_DIM
    groups = NSA_KV_GROUPS
    kvdim = groups * d
    qdim = out_proj.shape[0]
    n_heads = qdim // d
    hpg = n_heads // groups
    col = lambda k: qdim + k * kvdim
    gate_col = 2 * kvdim

    tables = tuple(t.reshape(batch * seq, d) for t in _rope_tables(positions))
    w_rot = jnp.concatenate([in_proj[:, :qdim], in_proj[:, col(2):col(3)], in_proj[:, col(4):col(5)]], axis=1)
    col_scale = jnp.concatenate([jnp.full((qdim,), d ** -0.5 * LOG2_E, F32), jnp.ones((2 * kvdim,), F32)])
    qk = norm_matmul_rope(h, nw, w_rot.astype(BF16), tables, col_scale)

    wg = in_proj[:, col(6):].reshape(-1, groups, 3 * hpg)
    wg = jnp.pad(wg, ((0, 0), (0, 0), (0, LANES - 3 * hpg))).reshape(-1, groups * LANES)
    w = jnp.concatenate([in_proj[:, col(0):col(2)], wg], axis=1)
    proj = norm_matmul(h, nw, w.astype(BF16), F32, tn=w.shape[1])
    w_v_t = jnp.concatenate([in_proj[:, col(3):col(4)], in_proj[:, col(5):col(6)]], axis=1).T.astype(BF16)
    v_sw_t = norm_matmul_t(h, nw, w_v_t, BF16)

    n_rows = seq // CMP_STRIDE
    cmp_end = jnp.minimum(jnp.arange(n_rows) * CMP_STRIDE + CMP_LEN - 1, seq - 1)
    cmp_tables = _rope_tables(positions[:, cmp_end])

    k_cmp = nsa_compress(proj, 0, batch, seq, groups, cmp_pe[0], cmp_w1[0], cmp_w2[0], cmp_tables, True)
    v_cmp_t = nsa_compress(proj, kvdim, batch, seq, groups, cmp_pe[1], cmp_w1[1], cmp_w2[1], cmp_tables, False)

    o = nsa_attention(qk, v_sw_t, k_cmp, v_cmp_t, proj, _cmp_to_sel(seq), batch, seq, qdim, kvdim, gate_col)
    return matmul_residual(o, out_proj.astype(BF16), h, tm=512, tn=out_proj.shape[1])


def kernel(x, p, positions, norm_mix, norm_ffn, norm_ple, w_ple_up, w_ple_gate, w_ffn_in, w_ffn_out, norm_final, ssd_in_proj, ssd_conv_w, ssd_conv_b, ssd_dt_bias, ssd_a_log, ssd_d, ssd_norm, ssd_out_proj, lru_in_proj, lru_conv_w, lru_conv_b, lru_wa, lru_ba, lru_wx, lru_bx, lru_a_param, lru_out_proj, nsa_in_proj, nsa_cmp_pe, nsa_cmp_w1, nsa_cmp_w2, nsa_out_proj):
    batch, seq, d_model = x.shape
    depth = norm_mix.shape[0]
    n_mixers = 3
    m = batch * seq
    h = x.reshape(m, d_model)
    for i in range(depth):
        kind, j = i % n_mixers, i // n_mixers
        if kind == 0:
            h = ssd_mixer(h, norm_mix[i], batch, seq, ssd_in_proj, ssd_conv_w[j], ssd_conv_b[j], ssd_dt_bias[j],
                          ssd_a_log[j], ssd_d[j], ssd_norm[j], ssd_out_proj, layer=j)
        elif kind == 1:
            h = rglru_mixer(h, norm_mix[i], batch, seq, lru_in_proj[j], lru_conv_w[j], lru_conv_b[j], lru_wa[j],
                            lru_ba[j], lru_wx[j], lru_bx[j], lru_a_param[j], lru_out_proj[j])
        else:
            h = nsa_mixer(h, norm_mix[i], positions, batch, seq, nsa_in_proj[j], nsa_cmp_pe[j], nsa_cmp_w1[j],
                          nsa_cmp_w2[j], nsa_out_proj[j])
        hidden = norm_swiglu_in(h, norm_ffn[i], w_ffn_in.astype(BF16), layer=i)
        h = matmul_residual(hidden, w_ffn_out.astype(BF16), h, layer=i)
        h = ple_layer(h, norm_ple[i], w_ple_gate.astype(BF16), p.reshape(depth, m, -1), w_ple_up.astype(BF16),
                      layer=i, final_nw=norm_final if i == depth - 1 else None)
    return h.reshape(batch, seq, d_model)
```

```python
import functools

import numpy as np
import jax
import jax.numpy as jnp
from jax import lax
from jax.experimental import pallas as pl
from jax.experimental.pallas import tpu as pltpu

F32 = jnp.float32
BF16 = jnp.bfloat16

RMS_EPS = 1e-6
ROPE_THETA = 500000.0
NEG_INF = -1e30
FORCED_SCORE = 1e9
LOG2_E = 1.4426950408889634

VMEM_LIMIT_BYTES = 52 * 1024 * 1024
LANES = 128

SSD_HEAD_DIM = 64
SSD_GROUPS = 8
SSD_STATE = 128
SSD_CHUNK = 128

LRU_BLOCK_DIM = 256
LRU_C = 8.0
LRU_ROWS = 256

NSA_HEAD_DIM = 128
NSA_KV_GROUPS = 4
ROT_DIM = NSA_HEAD_DIM // 4
CMP_LEN = 32
CMP_STRIDE = 16
SEL_LEN = 64
SEL_TOPK = 16
WINDOW = 512
Q_BLOCK = 128
SEL_KEY_TILE = 512
NSA_GROUPS_PER_STEP = 2

CONV_HIST = 8


def _cparams(*sem):
    return pltpu.CompilerParams(dimension_semantics=sem, vmem_limit_bytes=VMEM_LIMIT_BYTES)


def _layer_spec(block, index_map, layer):
    if layer is None:
        return pl.BlockSpec(block, index_map)
    return pl.BlockSpec((None,) + block, lambda *g: (layer,) + index_map(*g))


def _dot(a, b):
    return jnp.dot(a, b, preferred_element_type=F32)


def _dot_nt(a, b):
    return lax.dot_general(a, b, (((1,), (1,)), ((), ())), preferred_element_type=F32)


def _split3(x):
    hi = x.astype(BF16)
    r1 = x - hi.astype(F32)
    mid = r1.astype(BF16)
    lo = (r1 - mid.astype(F32)).astype(BF16)
    return hi, mid, lo


def _dot_exact_lhs(e, x):
    hi, mid, lo = _split3(x)
    return _dot(e, hi) + _dot(e, mid) + _dot(e, lo)


def _rms_scale(x):
    return lax.rsqrt(jnp.mean(x * x, axis=-1, keepdims=True) + RMS_EPS)


def _softplus(x):
    return jnp.maximum(x, 0.0) + jnp.log(1.0 + jnp.exp(-jnp.abs(x)))


def _sigmoid(x):
    return 0.5 + 0.5 * jnp.tanh(0.5 * x)


def _silu(x):
    h = 0.5 * x
    return h + h * jnp.tanh(h)


def _norm_mm_kernel(x_ref, nw_ref, w_ref, o_ref, u_ref):
    @pl.when(pl.program_id(1) == 0)
    def _():
        rows = x_ref.shape[0] // 4
        for lo in range(0, x_ref.shape[0], rows):
            sl = slice(lo, lo + rows)
            x = x_ref[sl, :]
            u = (x * _rms_scale(x) * nw_ref[...]).astype(BF16)
            u_ref[sl, :] = u
            o_ref[sl, :] = _dot(u, w_ref[...]).astype(o_ref.dtype)

    @pl.when(pl.program_id(1) != 0)
    def _():
        o_ref[...] = _dot(u_ref[...], w_ref[...]).astype(o_ref.dtype)


def norm_matmul(x, nw, w, out_dtype, tm=1024, tn=512, layer=None):
    m, k = x.shape
    n = w.shape[-1]
    return pl.pallas_call(
        _norm_mm_kernel,
        grid=(m // tm, n // tn),
        in_specs=[
            pl.BlockSpec((tm, k), lambda i, j: (i, 0)),
            pl.BlockSpec((1, k), lambda i, j: (0, 0)),
            _layer_spec((k, tn), lambda i, j: (0, j), layer),
        ],
        out_specs=pl.BlockSpec((tm, tn), lambda i, j: (i, j)),
        out_shape=jax.ShapeDtypeStruct((m, n), out_dtype),
        scratch_shapes=[pltpu.VMEM((tm, k), BF16)],
        compiler_params=_cparams("parallel", "arbitrary"),
        name="norm_matmul",
    )(x, nw.reshape(1, k), w)


def _norm_mm_t_kernel(x_ref, nw_ref, wt_ref, o_ref):
    x = x_ref[...]
    u = (x * _rms_scale(x) * nw_ref[...]).astype(BF16)
    o_ref[...] = _dot_nt(wt_ref[...], u).astype(o_ref.dtype)


def norm_matmul_t(x, nw, w_t, out_dtype, tm=1024):
    m, k = x.shape
    n = w_t.shape[0]
    return pl.pallas_call(
        _norm_mm_t_kernel,
        grid=(m // tm,),
        in_specs=[
            pl.BlockSpec((tm, k), lambda i: (i, 0)),
            pl.BlockSpec((1, k), lambda i: (0, 0)),
            pl.BlockSpec((n, k), lambda i: (0, 0)),
        ],
        out_specs=pl.BlockSpec((n, tm), lambda i: (0, i)),
        out_shape=jax.ShapeDtypeStruct((n, m), out_dtype),
        compiler_params=_cparams("parallel"),
        name="norm_matmul_t",
    )(x, nw.reshape(1, k), w_t)


def _norm_swiglu_kernel(x_ref, nw_ref, wg_ref, wu_ref, o_ref, u_ref):
    def gated(u):
        return (_silu(_dot(u, wg_ref[...])) * _dot(u, wu_ref[...])).astype(o_ref.dtype)

    @pl.when(pl.program_id(1) == 0)
    def _():
        rows = x_ref.shape[0] // 4
        for lo in range(0, x_ref.shape[0], rows):
            sl = slice(lo, lo + rows)
            x = x_ref[sl, :]
            u = (x * _rms_scale(x) * nw_ref[...]).astype(BF16)
            u_ref[sl, :] = u
            o_ref[sl, :] = gated(u)

    @pl.when(pl.program_id(1) != 0)
    def _():
        o_ref[...] = gated(u_ref[...])


def norm_swiglu_in(x, nw, w_in, tm=1024, tn=512, layer=None):
    m, k = x.shape
    f = w_in.shape[-1] // 2
    nj = f // tn
    return pl.pallas_call(
        _norm_swiglu_kernel,
        grid=(m // tm, nj),
        in_specs=[
            pl.BlockSpec((tm, k), lambda i, j: (i, 0)),
            pl.BlockSpec((1, k), lambda i, j: (0, 0)),
            _layer_spec((k, tn), lambda i, j: (0, j), layer),
            _layer_spec((k, tn), lambda i, j: (0, j + nj), layer),
        ],
        out_specs=pl.BlockSpec((tm, tn), lambda i, j: (i, j)),
        out_shape=jax.ShapeDtypeStruct((m, f), BF16),
        scratch_shapes=[pltpu.VMEM((tm, k), BF16)],
        compiler_params=_cparams("parallel", "arbitrary"),
        name="norm_swiglu_in",
    )(x, nw.reshape(1, k), w_in, w_in)


def _mm_residual_kernel(x_ref, w_ref, r_ref, o_ref):
    o_ref[...] = r_ref[...] + _dot(x_ref[...], w_ref[...])


def matmul_residual(x, w, res, tm=1024, tn=512, layer=None):
    m, k = x.shape
    n = w.shape[-1]
    return pl.pallas_call(
        _mm_residual_kernel,
        grid=(m // tm, n // tn),
        in_specs=[
            pl.BlockSpec((tm, k), lambda i, j: (i, 0)),
            _layer_spec((k, tn), lambda i, j: (0, j), layer),
            pl.BlockSpec((tm, tn), lambda i, j: (i, j)),
        ],
        out_specs=pl.BlockSpec((tm, tn), lambda i, j: (i, j)),
        out_shape=jax.ShapeDtypeStruct((m, n), F32),
        compiler_params=_cparams("parallel", "arbitrary"),
        name="matmul_residual",
    )(x, w, res)


def _ple_kernel(final, x_ref, nw_ref, wg_ref, p_ref, wu_ref, fw_ref, o_ref):
    x = x_ref[...]
    u = (x * _rms_scale(x) * nw_ref[...]).astype(BF16)
    y = x + _sigmoid(_dot(u, wg_ref[...])) * _dot(p_ref[...].astype(BF16), wu_ref[...])
    o_ref[...] = y * _rms_scale(y) * fw_ref[...] if final else y


def ple_layer(x, nw, w_gate, p, w_up, layer=None, final_nw=None, tm=512):
    m, k = x.shape
    n = w_gate.shape[-1]
    kp = p.shape[-1]
    final = final_nw is not None
    fw = final_nw.reshape(1, n) if final else jnp.ones((1, n), F32)
    return pl.pallas_call(
        functools.partial(_ple_kernel, final),
        grid=(m // tm,),
        in_specs=[
            pl.BlockSpec((tm, k), lambda i: (i, 0)),
            pl.BlockSpec((1, k), lambda i: (0, 0)),
            _layer_spec((k, n), lambda i: (0, 0), layer),
            _layer_spec((tm, kp), lambda i: (i, 0), layer),
            _layer_spec((kp, n), lambda i: (0, 0), layer),
            pl.BlockSpec((1, n), lambda i: (0, 0)),
        ],
        out_specs=pl.BlockSpec((tm, n), lambda i: (i, 0)),
        out_shape=jax.ShapeDtypeStruct((m, n), F32),
        compiler_params=_cparams("parallel"),
        name="ple_layer",
    )(x, nw.reshape(1, k), w_gate, p, w_up, fw)


def _causal_conv(pad_ref, w_ref, b_ref, rows, lo, width):
    taps = w_ref.shape[0]
    x = pad_ref[0:CONV_HIST + rows, lo:lo + width]
    acc = w_ref[0:1, lo:lo + width] * x
    for k in range(1, taps):
        acc = pltpu.roll(acc, 1, axis=0) + w_ref[k:k + 1, lo:lo + width] * x
    return acc[CONV_HIST:CONV_HIST + rows, :] + b_ref[:, lo:lo + width]


def _ssd_kernel(z_ref, xs_ref, bc_ref, dt_ref, cwx_ref, cbx_ref, cwbc_ref, cbbc_ref, dtb_ref, alog_ref,
                dskip_ref, nw_ref, tril_ref, ehead_ref, o_ref,
                xpad_ref, bcpad_ref, bcact_ref, state_ref, acs_ref, acst_ref, xdt_ref, ydiag_ref):
    q = SSD_CHUNK
    gw = xs_ref.shape[1] // SSD_GROUPS
    heads_per_group = gw // SSD_HEAD_DIM
    n_bc = bc_ref.shape[1] // 2

    @pl.when(pl.program_id(1) == 0)
    def _():
        xpad_ref[0:CONV_HIST, :] = jnp.zeros((CONV_HIST, xpad_ref.shape[1]), F32)
        bcpad_ref[0:CONV_HIST, :] = jnp.zeros((CONV_HIST, bcpad_ref.shape[1]), F32)
        state_ref[...] = jnp.zeros(state_ref.shape, F32)

    xpad_ref[CONV_HIST:CONV_HIST + q, :] = xs_ref[...]
    bcpad_ref[CONV_HIST:CONV_HIST + q, :] = bc_ref[...]

    for lo in range(0, bc_ref.shape[1], 512):
        v = _causal_conv(bcpad_ref, cwbc_ref, cbbc_ref, q, lo, 512)
        bcact_ref[:, lo:lo + 512] = _silu(v)

    dt = _softplus(dt_ref[...] + dtb_ref[...])
    adt = dt * (-jnp.exp(alog_ref[...]))
    acs = _dot_exact_lhs(tril_ref[...], adt) * LOG2_E
    acs_ref[...] = acs
    acst_ref[...] = acs.T
    dt_parts = _split3(dt)
    acs_parts = _split3(acs)

    li = lax.broadcasted_iota(jnp.int32, (q, q), 0)
    si = lax.broadcasted_iota(jnp.int32, (q, q), 1)
    causal = li >= si

    for g in range(SSD_GROUPS):
        glo = g * gw
        e_g = ehead_ref[:, glo:glo + gw]
        expand = lambda parts: _dot(parts[0], e_g) + _dot(parts[1], e_g) + _dot(parts[2], e_g)
        xc = _causal_conv(xpad_ref, cwx_ref, cbx_ref, q, glo, gw)
        xs = _silu(xc)
        dt_g = expand(dt_parts)
        a_g = expand(acs_parts)
        a_last = a_g[q - 1:q, :]
        xdt = xs * dt_g
        xdt_ref[...] = xdt.astype(BF16)

        bm = bcact_ref[:, g * SSD_STATE:(g + 1) * SSD_STATE]
        cm = bcact_ref[:, n_bc + g * SSD_STATE:n_bc + (g + 1) * SSD_STATE]
        bm16 = bm.astype(BF16)
        cm16 = cm.astype(BF16)
        cb = _dot_nt(cm16, bm16)

        prev = state_ref[:, glo:glo + gw]
        y_off = _dot(cm16, prev.astype(BF16)) * jnp.exp2(a_g)
        st_new = _dot(bm.T.astype(BF16), (xdt * jnp.exp2(a_last - a_g)).astype(BF16))
        state_ref[:, glo:glo + gw] = prev * jnp.exp2(a_last) + st_new

        for r in range(heads_per_group):
            h = g * heads_per_group + r
            seg = acs_ref[:, h:h + 1] - acst_ref[h:h + 1, :]
            decay = jnp.exp2(jnp.where(causal, seg, NEG_INF))
            m_h = (cb * decay).astype(BF16)
            ydiag_ref[:, r * SSD_HEAD_DIM:(r + 1) * SSD_HEAD_DIM] = _dot(
                m_h, xdt_ref[:, r * SSD_HEAD_DIM:(r + 1) * SSD_HEAD_DIM])

        y = ydiag_ref[...] + y_off + dskip_ref[:, glo:glo + gw] * xs
        zg = z_ref[:, glo:glo + gw]
        y = y * _silu(zg)
        y = y * _rms_scale(y) * nw_ref[:, glo:glo + gw]
        o_ref[:, glo:glo + gw] = y.astype(o_ref.dtype)

    xpad_ref[0:CONV_HIST, :] = xpad_ref[q:q + CONV_HIST, :]
    bcpad_ref[0:CONV_HIST, :] = bcpad_ref[q:q + CONV_HIST, :]


def ssd_core(zxd, batch, seq, conv_w, conv_b, dt_bias, a_log, d_skip, norm_w):
    m = zxd.shape[0]
    n_heads = dt_bias.shape[0]
    inner = n_heads * SSD_HEAD_DIM
    n_bc = SSD_GROUPS * SSD_STATE
    q = SSD_CHUNK
    nc = seq // q
    assert inner % (2 * n_bc) == 0 and n_heads <= LANES
    pad_h = LANES - n_heads

    row = lambda v: v.reshape(1, -1).astype(F32)
    padh = lambda v: jnp.pad(v.astype(F32), (0, pad_h)).reshape(1, LANES)
    tril = jnp.asarray(np.tril(np.ones((q, q), np.float32)), BF16)
    ehead = jnp.asarray(np.repeat(np.eye(LANES, n_heads, dtype=np.float32), SSD_HEAD_DIM, axis=1), BF16)

    rows = lambda b, c: b * nc + c
    full = lambda a: pl.BlockSpec(a.shape, lambda b, c: (0,) * a.ndim)
    consts = [conv_w[:, :inner].astype(F32), row(conv_b[:inner]), conv_w[:, inner:].astype(F32), row(conv_b[inner:]),
              padh(dt_bias), padh(a_log), row(jnp.repeat(d_skip, SSD_HEAD_DIM)), row(norm_w), tril, ehead]
    return pl.pallas_call(
        _ssd_kernel,
        grid=(batch, nc),
        in_specs=[
            pl.BlockSpec((q, inner), lambda b, c: (rows(b, c), 0)),
            pl.BlockSpec((q, inner), lambda b, c: (rows(b, c), 1)),
            pl.BlockSpec((q, 2 * n_bc), lambda b, c: (rows(b, c), 2 * inner // (2 * n_bc))),
            pl.BlockSpec((q, LANES), lambda b, c: (rows(b, c), (2 * inner + 2 * n_bc) // LANES)),
        ] + [full(a) for a in consts],
        out_specs=pl.BlockSpec((q, inner), lambda b, c: (rows(b, c), 0)),
        out_shape=jax.ShapeDtypeStruct((m, inner), BF16),
        scratch_shapes=[
            pltpu.VMEM((q + CONV_HIST, inner), F32),
            pltpu.VMEM((q + CONV_HIST, 2 * n_bc), F32),
            pltpu.VMEM((q, 2 * n_bc), F32),
            pltpu.VMEM((SSD_STATE, inner), F32),
            pltpu.VMEM((q, LANES), F32),
            pltpu.VMEM((LANES, q), F32),
            pltpu.VMEM((q, inner // SSD_GROUPS), BF16),
            pltpu.VMEM((q, inner // SSD_GROUPS), F32),
        ],
        compiler_params=_cparams("parallel", "arbitrary"),
        name="ssd_core",
    )(zxd, zxd, zxd, zxd, *consts)


def _lru_kernel(gate_ref, xr_ref, cw_ref, cb_ref, wa_ref, ba_ref, wx_ref, bx_ref, ap_ref, o_ref,
                xpad_ref, a_ref, b_ref, h_ref, carry_ref):
    rows = gate_ref.shape[0]
    width = gate_ref.shape[1]

    @pl.when(pl.program_id(1) == 0)
    def _():
        xpad_ref[0:CONV_HIST, :] = jnp.zeros((CONV_HIST, width), F32)
        carry_ref[...] = jnp.zeros(carry_ref.shape, F32)

    xpad_ref[CONV_HIST:CONV_HIST + rows, :] = xr_ref[...]
    for k in range(width // LRU_BLOCK_DIM):
        lo = k * LRU_BLOCK_DIM
        sl = slice(lo, lo + LRU_BLOCK_DIM)
        xc = _causal_conv(xpad_ref, cw_ref, cb_ref, rows, lo, LRU_BLOCK_DIM)
        x16 = xc.astype(BF16)
        r_t = _sigmoid(_dot(x16, wa_ref[k]) + ba_ref[:, sl])
        i_t = _sigmoid(_dot(x16, wx_ref[k]) + bx_ref[:, sl])
        log_a = -LRU_C * r_t * _softplus(-ap_ref[:, sl])
        a_t = jnp.exp(log_a)
        a_ref[:, sl] = a_t
        b_ref[:, sl] = jnp.sqrt(1.0 - a_t * a_t) * (i_t * xc)
    xpad_ref[0:CONV_HIST, :] = xpad_ref[rows:rows + CONV_HIST, :]

    def step(t, h):
        h = a_ref[pl.ds(t, 1), :] * h + b_ref[pl.ds(t, 1), :]
        h_ref[pl.ds(t, 1), :] = h
        return h

    carry_ref[0:1, :] = lax.fori_loop(0, rows, step, carry_ref[0:1, :], unroll=8)
    o_ref[...] = (jax.nn.gelu(gate_ref[...], approximate=True) * h_ref[...]).astype(o_ref.dtype)


def lru_core(gx, batch, seq, conv_w, conv_b, wa, ba, wx, bx, a_param):
    m = gx.shape[0]
    width = gx.shape[1] // 2
    rows = LRU_ROWS
    nt = seq // rows
    row = lambda v: v.reshape(1, -1).astype(F32)
    consts = [conv_w.astype(F32), row(conv_b), wa.astype(BF16), row(ba), wx.astype(BF16), row(bx), row(a_param)]
    full = lambda a: pl.BlockSpec(a.shape, lambda b, t: (0,) * a.ndim)
    return pl.pallas_call(
        _lru_kernel,
        grid=(batch, nt),
        in_specs=[
            pl.BlockSpec((rows, width), lambda b, t: (b * nt + t, 0)),
            pl.BlockSpec((rows, width), lambda b, t: (b * nt + t, 1)),
        ] + [full(a) for a in consts],
        out_specs=pl.BlockSpec((rows, width), lambda b, t: (b * nt + t, 0)),
        out_shape=jax.ShapeDtypeStruct((m, width), BF16),
        scratch_shapes=[
            pltpu.VMEM((rows + CONV_HIST, width), F32),
            pltpu.VMEM((rows, width), F32),
            pltpu.VMEM((rows, width), F32),
            pltpu.VMEM((rows, width), F32),
            pltpu.VMEM((8, width), F32),
        ],
        compiler_params=_cparams("parallel", "arbitrary"),
        name="lru_core",
    )(gx, gx, *consts)


def _rope_tables(pos):
    half = ROT_DIM // 2
    inv = ROPE_THETA ** (-jnp.arange(half, dtype=F32) * 2.0 / ROT_DIM)
    ang = pos.astype(F32)[..., None] * inv
    cos, sin = jnp.cos(ang), jnp.sin(ang)
    shape = pos.shape + (NSA_HEAD_DIM - ROT_DIM,)
    zeros_h = jnp.zeros(pos.shape + (half,), F32)
    cos_t = jnp.concatenate([cos, cos, jnp.ones(shape, F32)], axis=-1)
    sin_lo = jnp.concatenate([-sin, zeros_h, jnp.zeros(shape, F32)], axis=-1)
    sin_hi = jnp.concatenate([zeros_h, sin, jnp.zeros(shape, F32)], axis=-1)
    return cos_t, sin_lo, sin_hi


def _apply_rope(x, cos_t, sin_lo, sin_hi):
    n = x.shape[1] // NSA_HEAD_DIM
    half = ROT_DIM // 2
    tile = lambda t: jnp.concatenate([t] * n, axis=1) if n > 1 else t
    up = pltpu.roll(x, x.shape[1] - half, axis=1)
    down = pltpu.roll(x, half, axis=1)
    return x * tile(cos_t) + up * tile(sin_lo) + down * tile(sin_hi)


def _norm_mm_rope_kernel(x_ref, nw_ref, w_ref, cos_ref, slo_ref, shi_ref, cs_ref, o_ref, u_ref):
    rows = u_ref.shape[0] // 4
    chunks = [slice(lo, lo + rows) for lo in range(0, u_ref.shape[0], rows)]

    def project(sl, u):
        y = _apply_rope(_dot(u, w_ref[...]), cos_ref[sl, :], slo_ref[sl, :], shi_ref[sl, :])
        o_ref[sl, :] = (y * cs_ref[...]).astype(o_ref.dtype)

    @pl.when(pl.program_id(1) == 0)
    def _():
        for sl in chunks:
            x = x_ref[sl, :]
            u = (x * _rms_scale(x) * nw_ref[...]).astype(BF16)
            u_ref[sl, :] = u
            project(sl, u)

    @pl.when(pl.program_id(1) != 0)
    def _():
        for sl in chunks:
            project(sl, u_ref[sl, :])


def norm_matmul_rope(x, nw, w, tables, col_scale, tm=1024, tn=768):
    m, k = x.shape
    n = w.shape[1]
    tspec = pl.BlockSpec((tm, NSA_HEAD_DIM), lambda i, j: (i, 0))
    return pl.pallas_call(
        _norm_mm_rope_kernel,
        grid=(m // tm, n // tn),
        in_specs=[
            pl.BlockSpec((tm, k), lambda i, j: (i, 0)),
            pl.BlockSpec((1, k), lambda i, j: (0, 0)),
            pl.BlockSpec((k, tn), lambda i, j: (0, j)),
            tspec, tspec, tspec,
            pl.BlockSpec((1, tn), lambda i, j: (0, j)),
        ],
        out_specs=pl.BlockSpec((tm, tn), lambda i, j: (i, j)),
        out_shape=jax.ShapeDtypeStruct((m, n), BF16),
        scratch_shapes=[pltpu.VMEM((tm, k), BF16)],
        compiler_params=_cparams("parallel", "arbitrary"),
        name="norm_matmul_rope",
    )(x, nw.reshape(1, k), w, *tables, col_scale.reshape(1, n))


def _compress_kernel(is_key, tok_ref, pe_ref, w1_ref, w2_ref, cos_ref, slo_ref, shi_ref, o_ref):
    d = w2_ref.shape[0]
    nrows = tok_ref.shape[0] // CMP_STRIDE
    top = jnp.zeros((nrows, d), F32)
    bot = jnp.zeros((nrows, d), F32)
    for pos in range(CMP_STRIDE):
        tok = tok_ref[pl.ds(pos, nrows, stride=CMP_STRIDE), :]
        lo, hi = pos * d, (CMP_STRIDE + pos) * d
        top = top + _dot((tok + pe_ref[pos:pos + 1, :]).astype(BF16), w1_ref[lo:lo + d, :])
        bot = bot + _dot((tok + pe_ref[CMP_STRIDE + pos:CMP_STRIDE + pos + 1, :]).astype(BF16), w1_ref[hi:hi + d, :])
    pre = top + pltpu.roll(bot, nrows - 1, axis=0)
    out = _dot(jax.nn.gelu(pre, approximate=True).astype(BF16), w2_ref[...])
    if is_key:
        o_ref[...] = _apply_rope(out, cos_ref[...], slo_ref[...], shi_ref[...]).astype(o_ref.dtype)
    else:
        o_ref[...] = out.T.astype(o_ref.dtype)


def nsa_compress(proj, col, batch, seq, groups, pe, w1, w2, tables, is_key):
    d = w2.shape[0]
    nrows = seq // CMP_STRIDE
    assert CMP_LEN == 2 * CMP_STRIDE and col % d == 0
    tspec = pl.BlockSpec((None, nrows, d), lambda i, j: (i, 0, 0))
    const = lambda a: pl.BlockSpec(a.shape, lambda i, j: (0,) * a.ndim)
    out_block = (None, None, nrows, d) if is_key else (None, None, d, nrows)
    out_dims = (batch, groups, nrows, d) if is_key else (batch, groups, d, nrows)
    consts = [pe.astype(F32), w1.astype(BF16), w2.astype(BF16)]
    return pl.pallas_call(
        functools.partial(_compress_kernel, is_key),
        grid=(batch, groups),
        in_specs=[pl.BlockSpec((seq, d), lambda i, j: (i, col // d + j))] + [const(a) for a in consts]
        + [tspec, tspec, tspec],
        out_specs=pl.BlockSpec(out_block, lambda i, j: (i, j, 0, 0)),
        out_shape=jax.ShapeDtypeStruct(out_dims, BF16),
        compiler_params=_cparams("parallel", "arbitrary"),
        name="nsa_compress_k" if is_key else "nsa_compress_v",
    )(proj, *consts, *tables)


def _nsa_attn_kernel(n_sel, q_ref, kcmp_ref, vcmp_ref, ks_ref, vs_ref, kw_ref, vw_ref, gate_ref, c2s_ref, e_ref,
                     o_ref, impt_ref, m_ref, l_ref, acc_ref, out_ref, q4_ref, sa_ref, sb_ref):
    tq = Q_BLOCK
    d = NSA_HEAD_DIM
    gps = kcmp_ref.shape[0]
    hpg = q_ref.shape[1] // (gps * d)
    n_cmp_pad = kcmp_ref.shape[1]
    n_sel_pad = c2s_ref.shape[0]
    groups = range(gps)
    t0 = pl.multiple_of(pl.program_id(2) * tq, tq)
    t_row = t0 + lax.broadcasted_iota(jnp.int32, (1, tq), 1)
    gates = [_sigmoid(gate_ref[:, gi * LANES:(gi + 1) * LANES]).T for gi in groups]
    grp = lambda gi: slice(gi * d, (gi + 1) * d)
    head = lambda gi, r: slice((gi * hpg + r) * d, (gi * hpg + r + 1) * d)
    cols = lambda r: slice(r * tq, (r + 1) * tq)
    per_head = lambda a: jnp.concatenate([a] * hpg, axis=1)
    for gi in groups:
        for r in range(hpg):
            q4_ref[gi, cols(r), 0:d] = q_ref[:, head(gi, r)]
    q4 = [q4_ref[gi, :, 0:d] for gi in groups]

    n_col = lax.broadcasted_iota(jnp.int32, (n_cmp_pad, 1), 0)
    ok_c = (n_col * CMP_STRIDE + (CMP_LEN - 1) <= t_row) & (n_col < n_cmp_pad - 1)
    bias_c = per_head(jnp.where(ok_c, 0.0, NEG_INF))
    keep_c = per_head(jnp.where(ok_c, 1.0, 0.0))
    n_win = WINDOW + tq
    w0 = pl.multiple_of(jnp.maximum(t0 - WINDOW, 0), tq)
    kp = w0 + lax.broadcasted_iota(jnp.int32, (n_win, 1), 0)
    bias_w = per_head(jnp.where((kp <= t_row) & (kp > t_row - WINDOW), 0.0, NEG_INF))
    causal = (lax.broadcasted_iota(jnp.int32, (tq, 1), 0) <= lax.broadcasted_iota(jnp.int32, (1, tq), 1))
    bias_d = per_head(jnp.where(causal, 0.0, NEG_INF))

    s_c = [_dot_nt(kcmp_ref[gi], q4[gi]) + bias_c for gi in groups]
    s_w = [_dot_nt(kw_ref[pl.ds(w0, n_win), grp(gi)], q4[gi]) + bias_w for gi in groups]
    s_d = [_dot_nt(ks_ref[pl.ds(t0, tq), grp(gi)], q4[gi]) + bias_d for gi in groups]

    j_col = lax.broadcasted_iota(jnp.int32, (n_sel_pad, 1), 0)
    cur = t_row // SEL_LEN
    forced = (j_col == 0) | (j_col == cur) | (j_col == cur - 1)
    sub = lax.broadcasted_iota(jnp.int32, (8, 1), 0)
    n_blk = n_sel // 8
    for gi in groups:
        e = jnp.exp2(s_c[gi] - jnp.max(s_c[gi], axis=0, keepdims=True)) * keep_c
        den = jnp.sum(e, axis=0, keepdims=True)
        p = e * jnp.where(den > 0.0, 1.0 / den, 0.0)
        o_c = _dot(vcmp_ref[gi], p.astype(BF16))
        p_sum = p[:, cols(0)]
        for r in range(1, hpg):
            p_sum = p_sum + p[:, cols(r)]
        for r in range(hpg):
            out_ref[gi, :, cols(r)] = gates[gi][3 * r:3 * r + 1, :] * o_c[:, cols(r)]

        imp = _dot_exact_lhs(c2s_ref[...], p_sum)
        imp = jnp.where(forced, FORCED_SCORE, imp)
        impt_ref[gi] = jnp.where(j_col <= cur, imp, -jnp.inf)
        blks = [impt_ref[gi, jb * 8:(jb + 1) * 8, :] for jb in range(n_blk)]
        cnt = [jnp.zeros((8, tq), F32) for _ in range(n_blk)]
        for i in range(n_sel):
            other = impt_ref[gi, i:i + 1, :]
            for jb in range(n_blk):
                if jb * 8 > i:
                    ahead = other >= blks[jb]
                elif jb * 8 + 7 < i:
                    ahead = other > blks[jb]
                else:
                    ahead = (other > blks[jb]) | ((other == blks[jb]) & (sub > i - jb * 8))
                cnt[jb] = cnt[jb] + jnp.where(ahead, 1.0, 0.0)
        sel_t = [jnp.where(c < float(SEL_TOPK), 1.0, 0.0) for c in cnt]
        if n_sel_pad > n_sel:
            sel_t.append(jnp.zeros((n_sel_pad - n_sel, tq), F32))
        sel = jnp.concatenate(sel_t, axis=0)
        sel_bias = jnp.where((sel > 0.5) & (j_col * SEL_LEN < t0), 0.0, NEG_INF).T.astype(BF16)
        for r in range(hpg):
            q4_ref[gi, cols(r), d:2 * d] = sel_bias

        m_0 = jnp.max(s_d[gi], axis=0, keepdims=True)
        p = jnp.exp2(s_d[gi] - m_0)
        m_ref[gi] = m_0
        l_ref[gi] = jnp.sum(p, axis=0, keepdims=True)
        acc_ref[gi] = _dot(vs_ref[grp(gi), pl.ds(t0, tq)], p.astype(BF16))

    kt_n = SEL_KEY_TILE
    last_tile = ks_ref.shape[0] // kt_n - 1
    n_tiles = (t0 + kt_n - 1) // kt_n

    def sel_scores(gi, kt):
        k0 = pl.multiple_of(jnp.minimum(kt, last_tile) * kt_n, kt_n)
        lhs = jnp.concatenate([ks_ref[pl.ds(k0, kt_n), grp(gi)], e_ref[pl.ds(k0, kt_n), :]], axis=1)
        return _dot_nt(lhs, q4_ref[gi])

    def sel_update(gi, kt, s):
        k0 = pl.multiple_of(kt * kt_n, kt_n)
        m_old = m_ref[gi]
        m_new = jnp.maximum(m_old, jnp.max(s, axis=0, keepdims=True))
        alpha = jnp.exp2(m_old - m_new)
        p = jnp.exp2(s - m_new)
        l_ref[gi] = alpha * l_ref[gi] + jnp.sum(p, axis=0, keepdims=True)
        acc_ref[gi] = alpha * acc_ref[gi] + _dot(vs_ref[grp(gi), pl.ds(k0, kt_n)], p.astype(BF16))
        m_ref[gi] = m_new

    for gi in groups:
        sa_ref[gi] = sel_scores(gi, 0)

    for gi in groups:
        e = jnp.exp2(s_w[gi] - jnp.max(s_w[gi], axis=0, keepdims=True))
        p = e * (1.0 / jnp.sum(e, axis=0, keepdims=True))
        o_w = _dot(vw_ref[grp(gi), pl.ds(w0, n_win)], p.astype(BF16))
        for r in range(hpg):
            out_ref[gi, :, cols(r)] = out_ref[gi, :, cols(r)] + gates[gi][3 * r + 2:3 * r + 3, :] * o_w[:, cols(r)]

    def sel_step(pair, carry):
        for gi in groups:
            sb_ref[gi] = sel_scores(gi, 2 * pair + 1)
        ahead = [sel_scores(gi, 2 * pair + 2) for gi in groups]
        for gi in groups:
            sel_update(gi, 2 * pair, sa_ref[gi])
        for gi in groups:
            sel_update(gi, 2 * pair + 1, sb_ref[gi])
        for gi in groups:
            sa_ref[gi] = ahead[gi]
        return carry

    lax.fori_loop(0, n_tiles // 2, sel_step, 0)

    @pl.when(n_tiles % 2 == 1)
    def _():
        for gi in groups:
            sel_update(gi, n_tiles - 1, sa_ref[gi])

    for gi in groups:
        o_s = acc_ref[gi] * (1.0 / l_ref[gi])
        for r in range(hpg):
            o_r = out_ref[gi, :, cols(r)] + gates[gi][3 * r + 1:3 * r + 2, :] * o_s[:, cols(r)]
            o_ref[:, head(gi, r)] = o_r.T.astype(o_ref.dtype)


def nsa_attention(qk, v_sw_t, k_cmp, v_cmp_t, proj, c2s_t, batch, seq, qdim, kvdim, gate_col):
    m = qk.shape[0]
    d = NSA_HEAD_DIM
    groups = kvdim // d
    gq = qdim // groups
    hpg = gq // d
    nq = seq // Q_BLOCK
    n_sel = seq // SEL_LEN
    gps = NSA_GROUPS_PER_STEP
    assert n_sel % 8 == 0 and seq % SEL_KEY_TILE == 0 and seq >= WINDOW + Q_BLOCK and groups % gps == 0
    n_cmp_pad = k_cmp.shape[2]
    n_sel_pad = c2s_t.shape[0]
    assert n_sel_pad == d
    block_onehot = jnp.asarray(np.arange(seq)[:, None] // SEL_LEN == np.arange(n_sel_pad)[None, :], BF16)
    k_blk = lambda off: pl.BlockSpec((seq, gps * d), lambda b, g, t: (b, off // (gps * d) + g))
    v_blk = lambda off: pl.BlockSpec((gps * d, seq), lambda b, g, t: (off // (gps * d) + g, b))
    lanes_q = hpg * Q_BLOCK
    return pl.pallas_call(
        functools.partial(_nsa_attn_kernel, n_sel),
        grid=(batch, groups // gps, nq),
        in_specs=[
            pl.BlockSpec((Q_BLOCK, gps * gq), lambda b, g, t: (b * nq + t, g)),
            pl.BlockSpec((None, gps, n_cmp_pad, d), lambda b, g, t: (b, g, 0, 0)),
            pl.BlockSpec((None, gps, d, n_cmp_pad), lambda b, g, t: (b, g, 0, 0)),
            k_blk(qdim), v_blk(0), k_blk(qdim + kvdim), v_blk(kvdim),
            pl.BlockSpec((Q_BLOCK, gps * LANES), lambda b, g, t: (b * nq + t, gate_col // (gps * LANES) + g)),
            pl.BlockSpec(c2s_t.shape, lambda b, g, t: (0, 0)),
            pl.BlockSpec(block_onehot.shape, lambda b, g, t: (0, 0)),
        ],
        out_specs=pl.BlockSpec((Q_BLOCK, gps * gq), lambda b, g, t: (b * nq + t, g)),
        out_shape=jax.ShapeDtypeStruct((m, qdim), BF16),
        scratch_shapes=[
            pltpu.VMEM((gps, n_sel_pad, Q_BLOCK), F32),
            pltpu.VMEM((gps, 1, lanes_q), F32),
            pltpu.VMEM((gps, 1, lanes_q), F32),
            pltpu.VMEM((gps, d, lanes_q), F32),
            pltpu.VMEM((gps, d, lanes_q), F32),
            pltpu.VMEM((gps, lanes_q, 2 * d), BF16),
            pltpu.VMEM((gps, SEL_KEY_TILE, lanes_q), F32),
            pltpu.VMEM((gps, SEL_KEY_TILE, lanes_q), F32),
        ],
        compiler_params=_cparams("parallel", "parallel", "arbitrary"),
        name="nsa_attention",
    )(qk, k_cmp, v_cmp_t, qk, v_sw_t, qk, v_sw_t, proj, c2s_t, block_onehot)


def _cmp_to_sel(seq):
    n_cmp = (seq - CMP_LEN) // CMP_STRIDE + 1
    n_sel = seq // SEL_LEN
    c_start = np.arange(n_cmp)[:, None] * CMP_STRIDE
    s_start = np.arange(n_sel)[None, :] * SEL_LEN
    overlap = np.clip(np.minimum(c_start + CMP_LEN, s_start + SEL_LEN) - np.maximum(c_start, s_start), 0, None)
    out = np.zeros((max(n_sel, LANES), seq // CMP_STRIDE), np.float32)
    out[:n_sel, :n_cmp] = (overlap / CMP_STRIDE).T
    return jnp.asarray(out, BF16)


def ssd_mixer(h, nw, batch, seq, in_proj, conv_w, conv_b, dt_bias, a_log, d_skip, norm_w, out_proj, layer=None):
    used = in_proj.shape[-1]
    tn = 1536
    w = jnp.pad(in_proj, [(0, 0)] * (in_proj.ndim - 1) + [(0, -(-used // tn) * tn - used)]).astype(BF16)
    zxd = norm_matmul(h, nw, w, F32, tn=tn, layer=layer)
    y = ssd_core(zxd, batch, seq, conv_w, conv_b, dt_bias, a_log, d_skip, norm_w)
    return matmul_residual(y, out_proj.astype(BF16), h, layer=layer)


def rglru_mixer(h, nw, batch, seq, in_proj, conv_w, conv_b, wa, ba, wx, bx, a_param, out_proj):
    gx = norm_matmul(h, nw, in_proj.astype(BF16), F32, tn=1024)
    y = lru_core(gx, batch, seq, conv_w, conv_b, wa, ba, wx, bx, a_param)
    return matmul_residual(y, out_proj.astype(BF16), h, tm=512, tn=out_proj.shape[1])


def nsa_mixer(h, nw, positions, batch, seq, in_proj, cmp_pe, cmp_w1, cmp_w2, out_proj):
    d = NSA_HEAD_DIM
    groups = NSA_KV_GROUPS
    kvdim = groups * d
    qdim = out_proj.shape[0]
    n_heads = qdim // d
    hpg = n_heads // groups
    col = lambda k: qdim + k * kvdim
    gate_col = 2 * kvdim

    tables = tuple(t.reshape(batch * seq, d) for t in _rope_tables(positions))
    w_rot = jnp.concatenate([in_proj[:, :qdim], in_proj[:, col(2):col(3)], in_proj[:, col(4):col(5)]], axis=1)
    col_scale = jnp.concatenate([jnp.full((qdim,), d ** -0.5 * LOG2_E, F32), jnp.ones((2 * kvdim,), F32)])
    qk = norm_matmul_rope(h, nw, w_rot.astype(BF16), tables, col_scale)

    wg = in_proj[:, col(6):].reshape(-1, groups, 3 * hpg)
    wg = jnp.pad(wg, ((0, 0), (0, 0), (0, LANES - 3 * hpg))).reshape(-1, groups * LANES)
    w = jnp.concatenate([in_proj[:, col(0):col(2)], wg], axis=1)
    proj = norm_matmul(h, nw, w.astype(BF16), F32, tn=w.shape[1])
    w_v_t = jnp.concatenate([in_proj[:, col(3):col(4)], in_proj[:, col(5):col(6)]], axis=1).T.astype(BF16)
    v_sw_t = norm_matmul_t(h, nw, w_v_t, BF16)

    n_rows = seq // CMP_STRIDE
    cmp_end = jnp.minimum(jnp.arange(n_rows) * CMP_STRIDE + CMP_LEN - 1, seq - 1)
    cmp_tables = _rope_tables(positions[:, cmp_end])

    k_cmp = nsa_compress(proj, 0, batch, seq, groups, cmp_pe[0], cmp_w1[0], cmp_w2[0], cmp_tables, True)
    v_cmp_t = nsa_compress(proj, kvdim, batch, seq, groups, cmp_pe[1], cmp_w1[1], cmp_w2[1], cmp_tables, False)

    o = nsa_attention(qk, v_sw_t, k_cmp, v_cmp_t, proj, _cmp_to_sel(seq), batch, seq, qdim, kvdim, gate_col)
    return matmul_residual(o, out_proj.astype(BF16), h, tm=512, tn=out_proj.shape[1])


def kernel(x, p, positions, norm_mix, norm_ffn, norm_ple, w_ple_up, w_ple_gate, w_ffn_in, w_ffn_out, norm_final, ssd_in_proj, ssd_conv_w, ssd_conv_b, ssd_dt_bias, ssd_a_log, ssd_d, ssd_norm, ssd_out_proj, lru_in_proj, lru_conv_w, lru_conv_b, lru_wa, lru_ba, lru_wx, lru_bx, lru_a_param, lru_out_proj, nsa_in_proj, nsa_cmp_pe, nsa_cmp_w1, nsa_cmp_w2, nsa_out_proj):
    batch, seq, d_model = x.shape
    depth = norm_mix.shape[0]
    n_mixers = 3
    m = batch * seq
    h = x.reshape(m, d_model)
    for i in range(depth):
        kind, j = i % n_mixers, i // n_mixers
        if kind == 0:
            h = ssd_mixer(h, norm_mix[i], batch, seq, ssd_in_proj, ssd_conv_w[j], ssd_conv_b[j], ssd_dt_bias[j],
                          ssd_a_log[j], ssd_d[j], ssd_norm[j], ssd_out_proj, layer=j)
        elif kind == 1:
            h = rglru_mixer(h, norm_mix[i], batch, seq, lru_in_proj[j], lru_conv_w[j], lru_conv_b[j], lru_wa[j],
                            lru_ba[j], lru_wx[j], lru_bx[j], lru_a_param[j], lru_out_proj[j])
        else:
            h = nsa_mixer(h, norm_mix[i], positions, batch, seq, nsa_in_proj[j], nsa_cmp_pe[j], nsa_cmp_w1[j],
                          nsa_cmp_w2[j], nsa_out_proj[j])
        hidden = norm_swiglu_in(h, norm_ffn[i], w_ffn_in.astype(BF16), layer=i)
        h = matmul_residual(hidden, w_ffn_out.astype(BF16), h, layer=i)
        h = ple_layer(h, norm_ple[i], w_ple_gate.astype(BF16), p.reshape(depth, m, -1), w_ple_up.astype(BF16),
                      layer=i, final_nw=norm_final if i == depth - 1 else None)
    return h.reshape(batch, seq, d_model)
```
